```python
import jax, jax.numpy as jnp
from jax import lax
import numpy as np

D_MODEL = 1024
BATCH = 8
SEQ = 4096
DEPTH = 2

CHUNK = 64
Q_BLOCK = 128
HEAD_DIM = 64
SB_HEADS = D_MODEL // 256
SB_WIDTH = SB_HEADS * HEAD_DIM
RET_HEADS = D_MODEL // 128
RET_WIDTH = RET_HEADS * HEAD_DIM
CONV_WIDTH = D_MODEL - SB_WIDTH - RET_WIDTH
CONV_KERNEL = 31
D_FF = 4 * D_MODEL
EPS = 1e-6
ROPE_BASE = 10000.0
IN_COLS = 3 * SB_WIDTH + 4 * RET_WIDTH + 2 * CONV_WIDTH

kernel_name = "hybrid_sb_retention_conformer_block"


def rms_norm(x, g):
    x32 = x.astype(jnp.float32)
    y = x32 * lax.rsqrt(jnp.mean(x32 * x32, axis=-1, keepdims=True) + EPS)
    return (y * g.astype(jnp.float32)).astype(x.dtype)


def layer_norm(x, g, b):
    x32 = x.astype(jnp.float32)
    mu = jnp.mean(x32, axis=-1, keepdims=True)
    xc = x32 - mu
    y = xc * lax.rsqrt(jnp.mean(xc * xc, axis=-1, keepdims=True) + EPS)
    return y * g.astype(jnp.float32) + b.astype(jnp.float32)


def to_heads(t, n_heads):
    B, T, _ = t.shape
    return t.reshape(B, T, n_heads, HEAD_DIM).transpose(0, 2, 1, 3)


def from_heads(t):
    B, H, T, d = t.shape
    return t.transpose(0, 2, 1, 3).reshape(B, T, H * d)


def stick_breaking_attention(q, k, v):
    B, H, T, d = q.shape
    nb = T // Q_BLOCK
    q_blocks = jnp.moveaxis(q.reshape(B, H, nb, Q_BLOCK, d), 2, 0)
    key_pos = jnp.arange(T)
    scale = d ** -0.5

    def one_block(args):
        q_blk, b_idx = args
        z = jnp.einsum('bhqd,bhkd->bhqk', q_blk, k) * scale
        q_pos = b_idx * Q_BLOCK + jnp.arange(Q_BLOCK)
        mask = key_pos[None, :] < q_pos[:, None]
        log_beta = jax.nn.log_sigmoid(z)
        log_keep = jnp.where(mask, jax.nn.log_sigmoid(-z), 0.0)
        shifted = jnp.concatenate([log_keep[..., 1:], jnp.zeros_like(log_keep[..., :1])], axis=-1)
        tail = lax.cumsum(shifted, axis=3, reverse=True)
        w = jnp.where(mask, jnp.exp(log_beta + tail), 0.0)
        return jnp.einsum('bhqk,bhkd->bhqd', w, v)

    out = lax.map(one_block, (q_blocks, jnp.arange(nb)))
    return jnp.moveaxis(out, 0, 2).reshape(B, H, T, d)


def retention_rotary(x, pos):
    d = x.shape[-1]
    inv = 1.0 / (ROPE_BASE ** jnp.linspace(0.0, 1.0, d // 2, dtype=jnp.float32))
    ang = pos.astype(jnp.float32)[:, None] * inv[None, :]
    cos, sin = jnp.cos(ang), jnp.sin(ang)
    x1, x2 = x[..., 0::2], x[..., 1::2]
    return jnp.stack([x1 * cos - x2 * sin, x1 * sin + x2 * cos], axis=-1).reshape(x.shape)


def multiscale_retention(q, k, v):
    B, H, T, d = q.shape
    C = CHUNK
    nc = T // C
    log_g = jnp.log(1.0 - jnp.exp2(-5.0 - jnp.arange(H, dtype=jnp.float32)))
    j = jnp.arange(C, dtype=jnp.float32)
    rel = j[:, None] - j[None, :]
    intra_decay = jnp.where(rel >= 0, jnp.exp(log_g[:, None, None] * jnp.maximum(rel, 0.0)), 0.0)

    qc = q.reshape(B, H, nc, C, d)
    kc = k.reshape(B, H, nc, C, d)
    vc = v.reshape(B, H, nc, C, d)

    scores = jnp.einsum('bhncd,bhnsd->bhncs', qc, kc) * intra_decay[None, :, None]
    out_intra = jnp.einsum('bhncs,bhnsd->bhncd', scores, vc)

    k_dec = kc * jnp.exp(log_g[:, None] * (C - 1.0 - j)[None, :])[None, :, None, :, None]
    chunk_kv = jnp.einsum('bhncd,bhnce->bhnde', k_dec, vc)
    chunk_decay = jnp.exp(log_g * C)[None, :, None, None]

    def step(state, kv):
        return state * chunk_decay + kv, state

    _, prev_states = lax.scan(step, jnp.zeros((B, H, d, d), jnp.float32), jnp.moveaxis(chunk_kv, 2, 0))
    prev_states = jnp.moveaxis(prev_states, 0, 2)

    q_dec = qc * jnp.exp(log_g[:, None] * (j + 1.0)[None, :])[None, :, None, :, None]
    out_inter = jnp.einsum('bhncd,bhnde->bhnce', q_dec, prev_states)
    return (out_intra + out_inter).reshape(B, H, T, d)


def conformer_conv(a, gate, pw_b, dw_w, dw_b, ln_g, ln_b):
    c = a.shape[-1]
    pw_b = pw_b.astype(jnp.float32)
    h = (a.astype(jnp.float32) + pw_b[:c]) * jax.nn.sigmoid(gate.astype(jnp.float32) + pw_b[c:])
    h = lax.conv_general_dilated(
        h, dw_w.astype(jnp.float32)[:, None, :], window_strides=(1,),
        padding=[(CONV_KERNEL - 1, 0)], dimension_numbers=('NWC', 'WIO', 'NWC'),
        feature_group_count=c) + dw_b.astype(jnp.float32)
    return jax.nn.silu(layer_norm(h, ln_g, ln_b))


def setup_inputs(seed: int = 0) -> dict:
    key = jax.random.key(seed)
    ks = jax.random.split(key, 16)
    f32 = jnp.float32
    nrm = lambda k, shape, s: jax.random.normal(k, shape, f32) * s
    return {
        "x": nrm(ks[0], (BATCH, SEQ, D_MODEL), 1.0),
        "mix_norm_g": 1.0 + nrm(ks[1], (DEPTH, D_MODEL), 0.02),
        "w_in": nrm(ks[2], (DEPTH, D_MODEL, IN_COLS), D_MODEL ** -0.5),
        "sb_q_norm_g": 1.0 + nrm(ks[3], (DEPTH, HEAD_DIM), 0.02),
        "sb_k_norm_g": 1.0 + nrm(ks[4], (DEPTH, HEAD_DIM), 0.02),
        "ret_norm_g": 1.0 + nrm(ks[5], (DEPTH, RET_WIDTH), 0.02),
        "conv_pw_b": nrm(ks[6], (DEPTH, 2 * CONV_WIDTH), 0.02),
        "conv_dw_w": nrm(ks[7], (DEPTH, CONV_KERNEL, CONV_WIDTH), CONV_KERNEL ** -0.5),
        "conv_dw_b": nrm(ks[8], (DEPTH, CONV_WIDTH), 0.02),
        "conv_ln_g": 1.0 + nrm(ks[9], (DEPTH, CONV_WIDTH), 0.02),
        "conv_ln_b": nrm(ks[10], (DEPTH, CONV_WIDTH), 0.02),
        "w_out": nrm(ks[11], (DEPTH, D_MODEL, D_MODEL), D_MODEL ** -0.5),
        "mlp_norm_g": 1.0 + nrm(ks[12], (DEPTH, D_MODEL), 0.02),
        "w_ff1": nrm(ks[13], (DEPTH, D_MODEL, D_FF), D_MODEL ** -0.5),
        "w_ff2": nrm(ks[14], (DEPTH, D_FF, D_MODEL), D_FF ** -0.5),
    }


def reference(x, mix_norm_g, w_in, sb_q_norm_g, sb_k_norm_g, ret_norm_g, conv_pw_b,
              conv_dw_w, conv_dw_b, conv_ln_g, conv_ln_b, w_out, mlp_norm_g, w_ff1, w_ff2):
    B, T, _ = x.shape
    pos = jnp.arange(T)
    split_at = np.cumsum([SB_WIDTH] * 3 + [RET_WIDTH] * 4 + [CONV_WIDTH])
    for l in range(DEPTH):
        h = rms_norm(x, mix_norm_g[l])
        u = jnp.einsum('btd,dc->btc', h, w_in[l])
        sb_q, sb_k, sb_v, r_q, r_k, r_v, r_g, c_a, c_gate = jnp.split(u, split_at, axis=-1)

        q = to_heads(rms_norm(sb_q.reshape(B, T, SB_HEADS, HEAD_DIM), sb_q_norm_g[l]).reshape(B, T, SB_WIDTH), SB_HEADS)
        k = to_heads(rms_norm(sb_k.reshape(B, T, SB_HEADS, HEAD_DIM), sb_k_norm_g[l]).reshape(B, T, SB_WIDTH), SB_HEADS)
        v = to_heads(sb_v, SB_HEADS)
        y_sb = from_heads(stick_breaking_attention(q.astype(jnp.float32), k.astype(jnp.float32),
                                                   v.astype(jnp.float32)))

        rq = retention_rotary(to_heads(r_q, RET_HEADS).astype(jnp.float32), pos)
        rk = retention_rotary(to_heads(r_k, RET_HEADS).astype(jnp.float32), pos) * (HEAD_DIM ** -0.5)
        rv = to_heads(r_v, RET_HEADS).astype(jnp.float32)
        ret = from_heads(multiscale_retention(rq, rk, rv)).reshape(B, T, RET_HEADS, HEAD_DIM)
        ret = rms_norm(ret, ret_norm_g[l].reshape(RET_HEADS, HEAD_DIM)).reshape(B, T, RET_WIDTH)
        y_ret = jax.nn.silu(r_g.astype(jnp.float32)) * ret

        y_conv = conformer_conv(c_a, c_gate, conv_pw_b[l], conv_dw_w[l], conv_dw_b[l],
                                conv_ln_g[l], conv_ln_b[l])

        mixed = jnp.concatenate([y_sb, y_ret, y_conv], axis=-1).astype(x.dtype)
        x = x + jnp.einsum('btc,cd->btd', mixed, w_out[l])

        h = rms_norm(x, mlp_norm_g[l])
        f = jnp.square(jax.nn.relu(jnp.einsum('btd,df->btf', h, w_ff1[l])))
        x = x + jnp.einsum('btf,fd->btd', f, w_ff2[l])
    return x
```

```python
import functools
import math

import jax
import jax.numpy as jnp
import numpy as np
from jax import lax
from jax.experimental import pallas as pl
from jax.experimental.pallas import tpu as pltpu

F32 = jnp.float32
BF16 = jnp.bfloat16

HEAD_DIM = 64
CONV_KERNEL = 31
EPS = 1e-6
ROPE_BASE = 10000.0
LOG2E = 1.4426950408889634
LN2 = 0.6931471805599453

V7X_LANES = 128
V7X_MXU_DIM = 256
V7X_VMEM_BYTES = 64 * 1024 * 1024

SB_DEAD_LOG2 = -150.0

TOKEN_TILE = 512
SB_BLOCK = 128
RET_CHUNK = 256
CONV_TILE = 512
CONV_HALO = 32
FF_CHUNK = 1024


def _vmem_limit(nbytes):
    return int(min(nbytes, V7X_VMEM_BYTES - 4 * 1024 * 1024))


def _const_spec(shape):
    nd = len(shape)
    return pl.BlockSpec(shape, lambda *_: (0,) * nd, pipeline_mode=pl.Buffered(1))


def _in_proj_kernel(x_ref, g_ref, w_ref, ones_ref, cos_ref, sin_ref, gq_ref, gk_ref, pb_ref,
                    q_ref, k_ref, v_ref, rq_ref, rk_ref, rv_ref, sg_ref, hg_ref, *, widths):
    sbw, retw, cw = widths
    x = x_ref[0]
    ms = jnp.mean(x * x, axis=-1, keepdims=True)
    h = (x * lax.rsqrt(ms + EPS) * g_ref[...]).astype(BF16)

    def proj(c0, width):
        return jnp.dot(h, w_ref[:, c0:c0 + width], preferred_element_type=F32)

    def head_norm(u, gain):
        ss = jnp.dot((u * u).astype(BF16), ones_ref[...], preferred_element_type=F32)
        return u * lax.rsqrt(ss * (1.0 / HEAD_DIM) + EPS) * gain

    def rotary(u):
        n = u.shape[1]
        nxt = pltpu.roll(u, n - 1, axis=1)
        prv = pltpu.roll(u, 1, axis=1)
        lane = lax.broadcasted_iota(jnp.int32, u.shape, 1)
        partner = jnp.where((lane & 1) == 0, nxt, prv)
        return u * cos_ref[...] + partner * sin_ref[...]

    c = 0
    q_ref[0] = (head_norm(proj(c, sbw), gq_ref[...]) * (LOG2E * HEAD_DIM ** -0.5)).astype(BF16)
    c += sbw
    k_ref[0] = head_norm(proj(c, sbw), gk_ref[...]).astype(BF16)
    c += sbw
    v_ref[0] = proj(c, sbw).astype(BF16)
    c += sbw
    rq_ref[0] = rotary(proj(c, retw)).astype(BF16)
    c += retw
    rk_ref[0] = rotary(proj(c, retw)).astype(BF16)
    c += retw
    rv_ref[0] = proj(c, retw).astype(BF16)
    c += retw
    g = proj(c, retw)
    sg_ref[0] = (g * jax.nn.sigmoid(g)).astype(BF16)
    c += retw
    a = proj(c, cw) + pb_ref[:, :cw]
    c += cw
    gate = proj(c, cw) + pb_ref[:, cw:]
    hg_ref[0] = (a * jax.nn.sigmoid(gate)).astype(BF16)


def _in_proj(x, g, w, ones_bd, cos_t, sin_t, gq, gk, pb, widths):
    B, T, D = x.shape
    sbw, retw, cw = widths
    tm = min(TOKEN_TILE, T)
    ncols = w.shape[1]
    grid = (T // tm, B)
    tok = lambda width: pl.BlockSpec((1, tm, width), lambda t, b: (b, t, 0))
    tab = pl.BlockSpec((tm, retw), lambda t, b: (t, 0))
    out_widths = (sbw, sbw, sbw, retw, retw, retw, retw, cw)
    vmem = (2 * tm * D * 4 + D * ncols * 2 + 4 * tm * retw * 4
            + 2 * sum(out_widths) * tm * 2 + 6 * tm * D * 4 + 8 * 1024 * 1024)
    return pl.pallas_call(
        functools.partial(_in_proj_kernel, widths=widths),
        grid=grid,
        in_specs=[tok(D), _const_spec((1, D)), _const_spec((D, ncols)), _const_spec(ones_bd.shape),
                  tab, tab, _const_spec((1, sbw)), _const_spec((1, sbw)), _const_spec((1, 2 * cw))],
        out_specs=[tok(wd) for wd in out_widths],
        out_shape=[jax.ShapeDtypeStruct((B, T, wd), BF16) for wd in out_widths],
        compiler_params=pltpu.CompilerParams(
            dimension_semantics=("arbitrary", "arbitrary"), vmem_limit_bytes=_vmem_limit(vmem)),
        name="in_proj",
    )(x, g, w, ones_bd, cos_t, sin_t, gq, gk, pb)


def _sb_kernel(q_ref, k_ref, v_ref, m_ref, o_ref, acc_ref, c_ref, *, n_heads):
    blk = SB_BLOCK
    q0 = pl.multiple_of(pl.program_id(1) * blk, blk)
    q = q_ref[0]
    width = q.shape[1]
    lane = lax.broadcasted_iota(jnp.int32, (blk, width), 1)
    row = lax.broadcasted_iota(jnp.int32, (blk, blk), 0)
    col = lax.broadcasted_iota(jnp.int32, (blk, blk), 1)
    causal = col < row

    def sweep_block(qm, kstart, carry, mask):
        kb = k_ref[0, pl.ds(kstart, blk), :]
        vb = v_ref[0, pl.ds(kstart, blk), :]
        z = lax.dot_general(qm, kb, (((1,), (1,)), ((), ())), preferred_element_type=F32)
        sp = jnp.maximum(z, 0.0) + jnp.log(1.0 + jnp.exp2(-jnp.abs(z))) * (1.0 / LN2)
        log_beta = z - sp
        if mask is not None:
            sp = jnp.where(mask, sp, 0.0)
        hi = sp.astype(BF16)
        lo = (sp - hi.astype(F32)).astype(BF16)
        r = jnp.dot(jnp.concatenate([hi, lo], axis=1), m_ref[...], preferred_element_type=F32)
        w = jnp.exp2(carry + r[:, :blk] + log_beta)
        if mask is not None:
            w = jnp.where(mask, w, 0.0)
        pv = jnp.dot(w.astype(BF16), vb, preferred_element_type=F32)
        return pv, carry + r[:, blk:]

    y = jnp.zeros((blk, width), F32)
    for h in range(n_heads):
        in_head = (lane >= h * HEAD_DIM) & (lane < (h + 1) * HEAD_DIM)
        qm = jnp.where(in_head, q, jnp.zeros_like(q))
        pv, carry = sweep_block(qm, q0, jnp.zeros((blk, blk), F32), causal)
        acc_ref[...] = pv
        c_ref[...] = carry

        def cond(state):
            i, alive = state
            return jnp.logical_and(i * blk <= q0, alive)

        def body(state):
            i, _ = state
            kstart = pl.multiple_of(q0 - i * blk, blk)
            pv, carry = sweep_block(qm, kstart, c_ref[...], None)
            acc_ref[...] += pv
            c_ref[...] = carry
            return i + 1, jnp.max(carry) > SB_DEAD_LOG2

        lax.while_loop(cond, body, (jnp.int32(1), jnp.max(carry) > SB_DEAD_LOG2))
        y = jnp.where(in_head, acc_ref[...], y)
    o_ref[0] = y.astype(o_ref.dtype)


def _sb_attention(q, k, v, m):
    B, T, W = q.shape
    blk = SB_BLOCK
    grid = (B, T // blk)
    seq = pl.BlockSpec((1, T, W), lambda b, i: (b, 0, 0))
    qblk = pl.BlockSpec((1, blk, W), lambda b, i: (b, i, 0))
    vmem = 2 * 2 * T * W * 2 + 16 * 1024 * 1024
    return pl.pallas_call(
        functools.partial(_sb_kernel, n_heads=W // HEAD_DIM),
        grid=grid,
        in_specs=[qblk, seq, seq, _const_spec(m.shape)],
        out_specs=qblk,
        out_shape=jax.ShapeDtypeStruct((B, T, W), BF16),
        scratch_shapes=[pltpu.VMEM((blk, W), F32), pltpu.VMEM((blk, blk), F32)],
        compiler_params=pltpu.CompilerParams(
            dimension_semantics=("arbitrary", "arbitrary"), vmem_limit_bytes=_vmem_limit(vmem)),
        name="sb_attn",
    )(q, k, v, m)


def _ret_kernel(q_ref, k_ref, v_ref, sg_ref, qdec_ref, kdec_ref, dmat_ref, cdec_ref, bd_ref, ones_ref,
                g_ref, o_ref, state_ref, *, group):
    ch = q_ref.shape[1]
    width = q_ref.shape[2]
    heads_per_group = group // HEAD_DIM

    @pl.when(pl.program_id(1) == 0)
    def _():
        state_ref[...] = jnp.zeros_like(state_ref)

    lane = lax.broadcasted_iota(jnp.int32, (ch, group), 1)
    for gi in range(width // group):
        cols = slice(gi * group, (gi + 1) * group)
        q = q_ref[0, :, cols]
        k = k_ref[0, :, cols]
        v = v_ref[0, :, cols]
        state = state_ref[gi]
        qd = (q.astype(F32) * qdec_ref[:, cols]).astype(BF16)
        out = jnp.dot(qd, state.astype(BF16), preferred_element_type=F32)
        for hh in range(heads_per_group):
            in_head = (lane >= hh * HEAD_DIM) & (lane < (hh + 1) * HEAD_DIM)
            qm = jnp.where(in_head, q, jnp.zeros_like(q))
            sc = lax.dot_general(qm, k, (((1,), (1,)), ((), ())), preferred_element_type=F32)
            sc = sc * dmat_ref[gi * heads_per_group + hh]
            oh = jnp.dot(sc.astype(BF16), v, preferred_element_type=F32)
            out = out + jnp.where(in_head, oh, 0.0)
        kd = (k.astype(F32) * kdec_ref[:, cols]).astype(BF16)
        kv = lax.dot_general(kd, v, (((0,), (0,)), ((), ())), preferred_element_type=F32)
        state_ref[gi] = state * cdec_ref[:, cols] + kv * bd_ref[...]
        ss = jnp.dot((out * out).astype(BF16), ones_ref[...], preferred_element_type=F32)
        yn = out * lax.rsqrt(ss * (1.0 / HEAD_DIM) + EPS) * g_ref[:, cols]
        o_ref[0, :, cols] = (yn * sg_ref[0, :, cols].astype(F32)).astype(o_ref.dtype)


def _retention(rq, rk, rv, sg, qdec, kdec, dmat, cdec, bd, ones_bd, g):
    B, T, W = rq.shape
    ch = qdec.shape[0]
    group = V7X_MXU_DIM
    grid = (B, T // ch)
    tok = pl.BlockSpec((1, ch, W), lambda b, i: (b, i, 0))
    vmem = 2 * 5 * ch * W * 2 + 2 * ch * W * 4 + dmat.size * 4 + 24 * 1024 * 1024
    return pl.pallas_call(
        functools.partial(_ret_kernel, group=group),
        grid=grid,
        in_specs=[tok, tok, tok, tok, _const_spec(qdec.shape), _const_spec(kdec.shape), _const_spec(dmat.shape),
                  _const_spec(cdec.shape), _const_spec(bd.shape), _const_spec(ones_bd.shape), _const_spec(g.shape)],
        out_specs=tok,
        out_shape=jax.ShapeDtypeStruct((B, T, W), BF16),
        scratch_shapes=[pltpu.VMEM((W // group, group, group), F32)],
        compiler_params=pltpu.CompilerParams(
            dimension_semantics=("arbitrary", "arbitrary"), vmem_limit_bytes=_vmem_limit(vmem)),
        name="retention",
    )(rq, rk, rv, sg, qdec, kdec, dmat, cdec, bd, ones_bd, g)


def _conv_kernel(h_ref, w_ref, b_ref, lg_ref, lb_ref, o_ref, pad_ref, *, row_chunk):
    tc = h_ref.shape[1]
    halo = CONV_HALO

    @pl.when(pl.program_id(1) == 0)
    def _():
        pad_ref[0:halo, :] = jnp.zeros((halo, pad_ref.shape[1]), F32)

    pad_ref[halo:halo + tc, :] = h_ref[0].astype(F32)
    first = halo - (CONV_KERNEL - 1)
    for r0 in range(0, tc, row_chunk):
        acc = jnp.zeros((row_chunk, pad_ref.shape[1]), F32) + b_ref[...]
        for tap in range(CONV_KERNEL):
            acc = acc + pad_ref[r0 + first + tap:r0 + first + tap + row_chunk, :] * w_ref[tap:tap + 1, :]
        mu = jnp.mean(acc, axis=-1, keepdims=True)
        xc = acc - mu
        var = jnp.mean(xc * xc, axis=-1, keepdims=True)
        yn = xc * lax.rsqrt(var + EPS) * lg_ref[...] + lb_ref[...]
        o_ref[0, r0:r0 + row_chunk, :] = (yn * jax.nn.sigmoid(yn)).astype(o_ref.dtype)
    pad_ref[0:halo, :] = pad_ref[tc:tc + halo, :]


def _conv(hglu, dw_w, dw_b, ln_g, ln_b):
    B, T, C = hglu.shape
    tc = min(CONV_TILE, T)
    grid = (B, T // tc)
    tok = pl.BlockSpec((1, tc, C), lambda b, i: (b, i, 0))
    return pl.pallas_call(
        functools.partial(_conv_kernel, row_chunk=64),
        grid=grid,
        in_specs=[tok, _const_spec(dw_w.shape), _const_spec((1, C)), _const_spec((1, C)), _const_spec((1, C))],
        out_specs=tok,
        out_shape=jax.ShapeDtypeStruct((B, T, C), BF16),
        scratch_shapes=[pltpu.VMEM((tc + CONV_HALO, C), F32)],
        compiler_params=pltpu.CompilerParams(dimension_semantics=("arbitrary", "arbitrary")),
        name="conv",
    )(hglu, dw_w, dw_b, ln_g, ln_b)


def _out_mlp_kernel(x_ref, ysb_ref, yret_ref, yconv_ref, wo_ref, g_ref, w1_ref, w2_ref, o_ref):
    mixed = jnp.concatenate([ysb_ref[0], yret_ref[0], yconv_ref[0]], axis=1)
    x1 = x_ref[0] + jnp.dot(mixed, wo_ref[...], preferred_element_type=F32)
    ms = jnp.mean(x1 * x1, axis=-1, keepdims=True)
    h = (x1 * lax.rsqrt(ms + EPS) * g_ref[...]).astype(BF16)
    acc = x1
    for c0 in range(0, w1_ref.shape[1], FF_CHUNK):
        f = jnp.maximum(jnp.dot(h, w1_ref[:, c0:c0 + FF_CHUNK], preferred_element_type=F32), 0.0)
        acc = acc + jnp.dot((f * f).astype(BF16), w2_ref[c0:c0 + FF_CHUNK, :], preferred_element_type=F32)
    o_ref[0] = acc


def _out_mlp(x, ysb, yret, yconv, wo, g, w1, w2):
    B, T, D = x.shape
    tm = min(TOKEN_TILE, T)
    grid = (B, T // tm)
    tok = lambda width: pl.BlockSpec((1, tm, width), lambda b, t: (b, t, 0))
    dff = w1.shape[1]
    vmem = (D * D + 2 * D * dff) * 2 + 4 * tm * D * 4 + 4 * tm * D * 2 + tm * FF_CHUNK * 12 + 12 * 1024 * 1024
    return pl.pallas_call(
        _out_mlp_kernel,
        grid=grid,
        in_specs=[tok(D), tok(ysb.shape[2]), tok(yret.shape[2]), tok(yconv.shape[2]),
                  _const_spec(wo.shape), _const_spec((1, D)), _const_spec(w1.shape), _const_spec(w2.shape)],
        out_specs=tok(D),
        out_shape=jax.ShapeDtypeStruct((B, T, D), F32),
        compiler_params=pltpu.CompilerParams(
            dimension_semantics=("arbitrary", "arbitrary"), vmem_limit_bytes=_vmem_limit(vmem)),
        name="out_mlp",
    )(x, ysb, yret, yconv, wo, g, w1, w2)


def _block_diag_ones(n, block):
    i = np.arange(n)
    return (i[:, None] // block == i[None, :] // block).astype(np.float32)


def _suffix_sum_matrix(blk):
    j = np.arange(2 * blk)[:, None] % blk
    s = np.arange(2 * blk)[None, :]
    return -((s >= blk) | (j > s)).astype(np.float32)


def _rotary_tables(T, n_heads):
    inv = 1.0 / (ROPE_BASE ** jnp.linspace(0.0, 1.0, HEAD_DIM // 2, dtype=F32))
    ang = jnp.arange(T).astype(F32)[:, None] * inv[None, :]
    cos = jnp.repeat(jnp.cos(ang), 2, axis=1)
    sin = jnp.repeat(jnp.sin(ang), 2, axis=1)
    sign = jnp.where(jnp.arange(HEAD_DIM) % 2 == 0, -1.0, 1.0).astype(F32)
    return jnp.tile(cos, (1, n_heads)), jnp.tile(sin * sign[None, :], (1, n_heads))


def _retention_tables(n_heads, ch):
    log_g = jnp.log(1.0 - jnp.exp2(-5.0 - jnp.arange(n_heads, dtype=F32)))
    j = jnp.arange(ch, dtype=F32)
    per_lane = lambda t: jnp.repeat(t, HEAD_DIM, axis=-1)
    qdec = per_lane(jnp.exp(log_g[None, :] * (j + 1.0)[:, None]))
    kdec = per_lane(jnp.exp(log_g[None, :] * (ch - 1.0 - j)[:, None]))
    cdec = per_lane(jnp.exp(log_g * ch)[None, :])
    rel = j[:, None] - j[None, :]
    dmat = jnp.where(rel >= 0, jnp.exp(log_g[:, None, None] * jnp.maximum(rel, 0.0)), 0.0)
    return qdec, kdec, dmat, cdec


def kernel(x, mix_norm_g, w_in, sb_q_norm_g, sb_k_norm_g, ret_norm_g, conv_pw_b, conv_dw_w, conv_dw_b,
           conv_ln_g, conv_ln_b, w_out, mlp_norm_g, w_ff1, w_ff2):
    B, T, D = x.shape
    depth = w_in.shape[0]
    sbw = (D // 256) * HEAD_DIM
    retw = (D // 128) * HEAD_DIM
    cw = D - sbw - retw
    assert T % TOKEN_TILE == 0 or T < TOKEN_TILE
    assert T % RET_CHUNK == 0 and T % SB_BLOCK == 0
    widths = (sbw, retw, cw)
    n_ret_heads = retw // HEAD_DIM

    ones_bd = jnp.asarray(_block_diag_ones(V7X_MXU_DIM, HEAD_DIM), BF16)
    bd_mask = jnp.asarray(_block_diag_ones(V7X_MXU_DIM, HEAD_DIM), F32)
    sfx = jnp.asarray(_suffix_sum_matrix(SB_BLOCK), BF16)
    cos_t, sin_t = _rotary_tables(T, n_ret_heads)
    qdec, kdec, dmat, cdec = _retention_tables(n_ret_heads, RET_CHUNK)
    rk0 = 3 * sbw + retw
    col_scale = jnp.ones((w_in.shape[2],), F32).at[rk0:rk0 + retw].set(HEAD_DIM ** -0.5)

    row = lambda a: a.reshape(1, -1).astype(F32)
    for l in range(depth):
        w_l = (w_in[l] * col_scale[None, :]).astype(BF16)
        q, k, v, rq, rk, rv, sg, hg = _in_proj(
            x, row(mix_norm_g[l]), w_l, ones_bd, cos_t, sin_t,
            row(jnp.tile(sb_q_norm_g[l], sbw // HEAD_DIM)), row(jnp.tile(sb_k_norm_g[l], sbw // HEAD_DIM)),
            row(conv_pw_b[l]), widths)
        y_sb = _sb_attention(q, k, v, sfx)
        y_ret = _retention(rq, rk, rv, sg, qdec, kdec, dmat, cdec, bd_mask, ones_bd, row(ret_norm_g[l]))
        y_conv = _conv(hg, conv_dw_w[l].astype(F32), row(conv_dw_b[l]), row(conv_ln_g[l]), row(conv_ln_b[l]))
        x = _out_mlp(x, y_sb, y_ret, y_conv, w_out[l].astype(BF16), row(mlp_norm_g[l]),
                     w_ff1[l].astype(BF16), w_ff2[l].astype(BF16))
    return x
```

```python
import functools
import math

import jax
import jax.numpy as jnp
import numpy as np
from jax import lax
from jax.experimental import pallas as pl
from jax.experimental.pallas import tpu as pltpu

F32 = jnp.float32
BF16 = jnp.bfloat16

HEAD_DIM = 64
CONV_KERNEL = 31
EPS = 1e-6
ROPE_BASE = 10000.0
LOG2E = 1.4426950408889634
LN2 = 0.6931471805599453

V7X_LANES = 128
V7X_MXU_DIM = 256
V7X_VMEM_BYTES = 64 * 1024 * 1024

SB_DEAD_LOG2 = -150.0

TOKEN_TILE = 512
SB_BLOCK = 128
SB_TILE = 512
SB_LOOKBACK = 2
RET_CHUNK = 256
CONV_TILE = 512
CONV_HALO = 32
FF_CHUNK = 1024


def _vmem_limit(nbytes):
    return int(min(nbytes, V7X_VMEM_BYTES - 4 * 1024 * 1024))


def _const_spec(shape):
    nd = len(shape)
    return pl.BlockSpec(shape, lambda *_: (0,) * nd, pipeline_mode=pl.Buffered(1))


def _in_proj_kernel(x_ref, g_ref, w_ref, ones_ref, cos_ref, sin_ref, gq_ref, gk_ref, pb_ref,
                    q_ref, k_ref, v_ref, rq_ref, rk_ref, rv_ref, sg_ref, hg_ref, *, widths):
    sbw, retw, cw = widths
    x = x_ref[0]
    ms = jnp.mean(x * x, axis=-1, keepdims=True)
    h = (x * lax.rsqrt(ms + EPS) * g_ref[...]).astype(BF16)

    def proj(c0, width):
        return jnp.dot(h, w_ref[:, c0:c0 + width], preferred_element_type=F32)

    def head_norm(u, gain):
        ss = jnp.dot((u * u).astype(BF16), ones_ref[...], preferred_element_type=F32)
        return u * lax.rsqrt(ss * (1.0 / HEAD_DIM) + EPS) * gain

    def rotary(u):
        n = u.shape[1]
        nxt = pltpu.roll(u, n - 1, axis=1)
        prv = pltpu.roll(u, 1, axis=1)
        lane = lax.broadcasted_iota(jnp.int32, u.shape, 1)
        partner = jnp.where((lane & 1) == 0, nxt, prv)
        return u * cos_ref[...] + partner * sin_ref[...]

    c = 0
    q_ref[0] = (head_norm(proj(c, sbw), gq_ref[...]) * (LOG2E * HEAD_DIM ** -0.5)).astype(BF16)
    c += sbw
    k_ref[0] = head_norm(proj(c, sbw), gk_ref[...]).astype(BF16)
    c += sbw
    v_ref[0] = proj(c, sbw).astype(BF16)
    c += sbw
    rq_ref[0] = rotary(proj(c, retw)).astype(BF16)
    c += retw
    rk_ref[0] = rotary(proj(c, retw)).astype(BF16)
    c += retw
    rv_ref[0] = proj(c, retw).astype(BF16)
    c += retw
    g = proj(c, retw)
    sg_ref[0] = (g * jax.nn.sigmoid(g)).astype(BF16)
    c += retw
    a = proj(c, cw) + pb_ref[:, :cw]
    c += cw
    gate = proj(c, cw) + pb_ref[:, cw:]
    hg_ref[0] = (a * jax.nn.sigmoid(gate)).astype(BF16)


def _in_proj(x, g, w, ones_bd, cos_t, sin_t, gq, gk, pb, widths):
    B, T, D = x.shape
    sbw, retw, cw = widths
    tm = min(TOKEN_TILE, T)
    ncols = w.shape[1]
    grid = (T // tm, B)
    tok = lambda width: pl.BlockSpec((1, tm, width), lambda t, b: (b, t, 0))
    tab = pl.BlockSpec((tm, retw), lambda t, b: (t, 0))
    out_widths = (sbw, sbw, sbw, retw, retw, retw, retw, cw)
    vmem = (2 * tm * D * 4 + D * ncols * 2 + 4 * tm * retw * 4
            + 2 * sum(out_widths) * tm * 2 + 6 * tm * D * 4 + 8 * 1024 * 1024)
    return pl.pallas_call(
        functools.partial(_in_proj_kernel, widths=widths),
        grid=grid,
        in_specs=[tok(D), _const_spec((1, D)), _const_spec((D, ncols)), _const_spec(ones_bd.shape),
                  tab, tab, _const_spec((1, sbw)), _const_spec((1, sbw)), _const_spec((1, 2 * cw))],
        out_specs=[tok(wd) for wd in out_widths],
        out_shape=[jax.ShapeDtypeStruct((B, T, wd), BF16) for wd in out_widths],
        compiler_params=pltpu.CompilerParams(
            dimension_semantics=("arbitrary", "arbitrary"), vmem_limit_bytes=_vmem_limit(vmem)),
        name="in_proj",
    )(x, g, w, ones_bd, cos_t, sin_t, gq, gk, pb)


def _sb_kernel(q_ref, k_ref, v_ref, m_ref, o_ref, vm_ref, qs_ref, acc_ref, c_ref, *, n_heads):
    blk = SB_BLOCK
    tile = q_ref.shape[1]
    seq = k_ref.shape[1]
    t0 = pl.program_id(1) * tile
    row = lax.broadcasted_iota(jnp.int32, (blk, blk), 0)
    col = lax.broadcasted_iota(jnp.int32, (blk, blk), 1)
    causal = col < row

    def head_lanes(shape, h):
        lane = lax.broadcasted_iota(jnp.int32, shape, 1)
        return (lane >= h * HEAD_DIM) & (lane < (h + 1) * HEAD_DIM)

    @pl.when(pl.program_id(1) == 0)
    def _():
        rows = min(SB_TILE, seq)

        def fill(i, _):
            r0 = pl.multiple_of(i * rows, rows)
            vb = v_ref[0, pl.ds(r0, rows), :]
            for h in range(n_heads):
                vm_ref[h, pl.ds(r0, rows), :] = jnp.where(head_lanes(vb.shape, h), vb, jnp.zeros_like(vb))
            return 0

        lax.fori_loop(0, seq // rows, fill, 0)

    def sweep(kstart, nblk, diag_last, carries):
        kwin = k_ref[0, pl.ds(kstart, nblk * blk), :]
        z_all = lax.dot_general(qs_ref[...], kwin, (((1,), (1,)), ((), ())), preferred_element_type=F32)
        log_beta, split = {}, []
        for h in range(n_heads):
            for j in range(nblk):
                z = z_all[h * blk:(h + 1) * blk, j * blk:(j + 1) * blk]
                sp = jnp.maximum(z, 0.0) + jnp.log(1.0 + jnp.exp2(-jnp.abs(z))) * (1.0 / LN2)
                log_beta[h, j] = z - sp
                if diag_last and j == nblk - 1:
                    sp = jnp.where(causal, sp, 0.0)
                hi = sp.astype(BF16)
                lo = (sp - hi.astype(F32)).astype(BF16)
                split.append(jnp.concatenate([hi, lo], axis=1))
        r = jnp.dot(jnp.concatenate(split, axis=0), m_ref[...], preferred_element_type=F32)
        weights, values, out_carries = [], [], []
        for h in range(n_heads):
            carry = carries[h]
            w = [None] * nblk
            for j in reversed(range(nblk)):
                rr = r[(h * nblk + j) * blk:(h * nblk + j + 1) * blk, :]
                arg = rr[:, :blk] + log_beta[h, j]
                if carry is not None:
                    arg = arg + carry
                wj = jnp.exp2(arg)
                if diag_last and j == nblk - 1:
                    wj = jnp.where(causal, wj, 0.0)
                w[j] = wj.astype(BF16)
                carry = rr[:, blk:] if carry is None else carry + rr[:, blk:]
            out_carries.append(carry)
            weights.extend(w)
            values.append(vm_ref[h, pl.ds(kstart, nblk * blk), :])
        pv = jnp.dot(jnp.concatenate(weights, axis=1), jnp.concatenate(values, axis=0),
                     preferred_element_type=F32)
        return pv, out_carries

    def any_alive(carries):
        top = carries[0]
        for c in carries[1:]:
            top = jnp.maximum(top, c)
        return jnp.max(top) > SB_DEAD_LOG2

    def sub_block(s, _):
        off = pl.multiple_of(s * blk, blk)
        q0 = pl.multiple_of(t0 + off, blk)
        q = q_ref[0, pl.ds(off, blk), :]
        for h in range(n_heads):
            qs_ref[h * blk:(h + 1) * blk, :] = jnp.where(head_lanes(q.shape, h), q, jnp.zeros_like(q))

        def start(lookback):
            def run():
                pv, carries = sweep(pl.multiple_of(q0 - lookback * blk, blk), lookback + 1, True, [None] * n_heads)
                acc_ref[...] = pv
                for h in range(n_heads):
                    c_ref[h] = carries[h]
                return jnp.int32(lookback + 1), any_alive(carries)
            return run

        state = lax.cond(q0 >= SB_LOOKBACK * blk, start(SB_LOOKBACK), start(0))

        def cond(state):
            i, alive = state
            return jnp.logical_and(i * blk <= q0, alive)

        def body(state):
            i, _ = state
            kstart = pl.multiple_of(q0 - i * blk, blk)
            pv, carries = sweep(kstart, 1, False, [c_ref[h] for h in range(n_heads)])
            acc_ref[...] += pv
            for h in range(n_heads):
                c_ref[h] = carries[h]
            return i + 1, any_alive(carries)

        lax.while_loop(cond, body, state)
        o_ref[0, pl.ds(off, blk), :] = acc_ref[...].astype(o_ref.dtype)
        return 0

    lax.fori_loop(0, tile // blk, sub_block, 0)


def _sb_attention(q, k, v, m):
    B, T, W = q.shape
    blk = SB_BLOCK
    tile = min(SB_TILE, T)
    n_heads = W // HEAD_DIM
    grid = (B, T // tile)
    seq = pl.BlockSpec((1, T, W), lambda b, i: (b, 0, 0))
    qblk = pl.BlockSpec((1, tile, W), lambda b, i: (b, i, 0))
    vmem = (4 + n_heads) * T * W * 2 + 24 * 1024 * 1024
    return pl.pallas_call(
        functools.partial(_sb_kernel, n_heads=n_heads),
        grid=grid,
        in_specs=[qblk, seq, seq, _const_spec(m.shape)],
        out_specs=qblk,
        out_shape=jax.ShapeDtypeStruct((B, T, W), BF16),
        scratch_shapes=[pltpu.VMEM((n_heads, T, W), BF16), pltpu.VMEM((n_heads * blk, W), BF16),
                        pltpu.VMEM((blk, W), F32), pltpu.VMEM((n_heads, blk, blk), F32)],
        compiler_params=pltpu.CompilerParams(
            dimension_semantics=("arbitrary", "arbitrary"), vmem_limit_bytes=_vmem_limit(vmem)),
        name="sb_attn",
    )(q, k, v, m)


def _ret_kernel(q_ref, k_ref, v_ref, sg_ref, qdec_ref, kdec_ref, dmat_ref, cdec_ref, bd_ref, ones_ref,
                g_ref, o_ref, state_ref, *, group):
    ch = q_ref.shape[1]
    width = q_ref.shape[2]
    heads_per_group = group // HEAD_DIM

    @pl.when(pl.program_id(1) == 0)
    def _():
        state_ref[...] = jnp.zeros_like(state_ref)

    lane = lax.broadcasted_iota(jnp.int32, (ch, group), 1)
    for gi in range(width // group):
        cols = slice(gi * group, (gi + 1) * group)
        q = q_ref[0, :, cols]
        k = k_ref[0, :, cols]
        v = v_ref[0, :, cols]
        state = state_ref[gi]
        qd = (q.astype(F32) * qdec_ref[:, cols]).astype(BF16)
        out = jnp.dot(qd, state.astype(BF16), preferred_element_type=F32)
        for hh in range(heads_per_group):
            in_head = (lane >= hh * HEAD_DIM) & (lane < (hh + 1) * HEAD_DIM)
            qm = jnp.where(in_head, q, jnp.zeros_like(q))
            sc = lax.dot_general(qm, k, (((1,), (1,)), ((), ())), preferred_element_type=F32)
            sc = sc * dmat_ref[gi * heads_per_group + hh]
            oh = jnp.dot(sc.astype(BF16), v, preferred_element_type=F32)
            out = out + jnp.where(in_head, oh, 0.0)
        kd = (k.astype(F32) * kdec_ref[:, cols]).astype(BF16)
        kv = lax.dot_general(kd, v, (((0,), (0,)), ((), ())), preferred_element_type=F32)
        state_ref[gi] = state * cdec_ref[:, cols] + kv * bd_ref[...]
        ss = jnp.dot((out * out).astype(BF16), ones_ref[...], preferred_element_type=F32)
        yn = out * lax.rsqrt(ss * (1.0 / HEAD_DIM) + EPS) * g_ref[:, cols]
        o_ref[0, :, cols] = (yn * sg_ref[0, :, cols].astype(F32)).astype(o_ref.dtype)


def _retention(rq, rk, rv, sg, qdec, kdec, dmat, cdec, bd, ones_bd, g):
    B, T, W = rq.shape
    ch = qdec.shape[0]
    group = V7X_MXU_DIM
    grid = (B, T // ch)
    tok = pl.BlockSpec((1, ch, W), lambda b, i: (b, i, 0))
    vmem = 2 * 5 * ch * W * 2 + 2 * ch * W * 4 + dmat.size * 4 + 24 * 1024 * 1024
    return pl.pallas_call(
        functools.partial(_ret_kernel, group=group),
        grid=grid,
        in_specs=[tok, tok, tok, tok, _const_spec(qdec.shape), _const_spec(kdec.shape), _const_spec(dmat.shape),
                  _const_spec(cdec.shape), _const_spec(bd.shape), _const_spec(ones_bd.shape), _const_spec(g.shape)],
        out_specs=tok,
        out_shape=jax.ShapeDtypeStruct((B, T, W), BF16),
        scratch_shapes=[pltpu.VMEM((W // group, group, group), F32)],
        compiler_params=pltpu.CompilerParams(
            dimension_semantics=("arbitrary", "arbitrary"), vmem_limit_bytes=_vmem_limit(vmem)),
        name="retention",
    )(rq, rk, rv, sg, qdec, kdec, dmat, cdec, bd, ones_bd, g)


def _conv_kernel(h_ref, w_ref, b_ref, lg_ref, lb_ref, o_ref, pad_ref, *, row_chunk):
    tc = h_ref.shape[1]
    halo = CONV_HALO

    @pl.when(pl.program_id(1) == 0)
    def _():
        pad_ref[0:halo, :] = jnp.zeros((halo, pad_ref.shape[1]), F32)

    pad_ref[halo:halo + tc, :] = h_ref[0].astype(F32)
    first = halo - (CONV_KERNEL - 1)
    for r0 in range(0, tc, row_chunk):
        acc = jnp.zeros((row_chunk, pad_ref.shape[1]), F32) + b_ref[...]
        for tap in range(CONV_KERNEL):
            acc = acc + pad_ref[r0 + first + tap:r0 + first + tap + row_chunk, :] * w_ref[tap:tap + 1, :]
        mu = jnp.mean(acc, axis=-1, keepdims=True)
        xc = acc - mu
        var = jnp.mean(xc * xc, axis=-1, keepdims=True)
        yn = xc * lax.rsqrt(var + EPS) * lg_ref[...] + lb_ref[...]
        o_ref[0, r0:r0 + row_chunk, :] = (yn * jax.nn.sigmoid(yn)).astype(o_ref.dtype)
    pad_ref[0:halo, :] = pad_ref[tc:tc + halo, :]


def _conv(hglu, dw_w, dw_b, ln_g, ln_b):
    B, T, C = hglu.shape
    tc = min(CONV_TILE, T)
    grid = (B, T // tc)
    tok = pl.BlockSpec((1, tc, C), lambda b, i: (b, i, 0))
    return pl.pallas_call(
        functools.partial(_conv_kernel, row_chunk=64),
        grid=grid,
        in_specs=[tok, _const_spec(dw_w.shape), _const_spec((1, C)), _const_spec((1, C)), _const_spec((1, C))],
        out_specs=tok,
        out_shape=jax.ShapeDtypeStruct((B, T, C), BF16),
        scratch_shapes=[pltpu.VMEM((tc + CONV_HALO, C), F32)],
        compiler_params=pltpu.CompilerParams(dimension_semantics=("arbitrary", "arbitrary")),
        name="conv",
    )(hglu, dw_w, dw_b, ln_g, ln_b)


def _out_mlp_kernel(x_ref, ysb_ref, yret_ref, yconv_ref, wo_ref, g_ref, w1_ref, w2_ref, o_ref):
    mixed = jnp.concatenate([ysb_ref[0], yret_ref[0], yconv_ref[0]], axis=1)
    x1 = x_ref[0] + jnp.dot(mixed, wo_ref[...], preferred_element_type=F32)
    ms = jnp.mean(x1 * x1, axis=-1, keepdims=True)
    h = (x1 * lax.rsqrt(ms + EPS) * g_ref[...]).astype(BF16)
    acc = x1
    for c0 in range(0, w1_ref.shape[1], FF_CHUNK):
        f = jnp.maximum(jnp.dot(h, w1_ref[:, c0:c0 + FF_CHUNK], preferred_element_type=F32), 0.0)
        acc = acc + jnp.dot((f * f).astype(BF16), w2_ref[c0:c0 + FF_CHUNK, :], preferred_element_type=F32)
    o_ref[0] = acc


def _out_mlp(x, ysb, yret, yconv, wo, g, w1, w2):
    B, T, D = x.shape
    tm = min(TOKEN_TILE, T)
    grid = (B, T // tm)
    tok = lambda width: pl.BlockSpec((1, tm, width), lambda b, t: (b, t, 0))
    dff = w1.shape[1]
    vmem = (D * D + 2 * D * dff) * 2 + 4 * tm * D * 4 + 4 * tm * D * 2 + tm * FF_CHUNK * 12 + 12 * 1024 * 1024
    return pl.pallas_call(
        _out_mlp_kernel,
        grid=grid,
        in_specs=[tok(D), tok(ysb.shape[2]), tok(yret.shape[2]), tok(yconv.shape[2]),
                  _const_spec(wo.shape), _const_spec((1, D)), _const_spec(w1.shape), _const_spec(w2.shape)],
        out_specs=tok(D),
        out_shape=jax.ShapeDtypeStruct((B, T, D), F32),
        compiler_params=pltpu.CompilerParams(
            dimension_semantics=("arbitrary", "arbitrary"), vmem_limit_bytes=_vmem_limit(vmem)),
        name="out_mlp",
    )(x, ysb, yret, yconv, wo, g, w1, w2)


def _block_diag_ones(n, block):
    i = np.arange(n)
    return (i[:, None] // block == i[None, :] // block).astype(np.float32)


def _suffix_sum_matrix(blk):
    j = np.arange(2 * blk)[:, None] % blk
    s = np.arange(2 * blk)[None, :]
    return -((s >= blk) | (j > s)).astype(np.float32)


def _rotary_tables(T, n_heads):
    inv = 1.0 / (ROPE_BASE ** jnp.linspace(0.0, 1.0, HEAD_DIM // 2, dtype=F32))
    ang = jnp.arange(T).astype(F32)[:, None] * inv[None, :]
    cos = jnp.repeat(jnp.cos(ang), 2, axis=1)
    sin = jnp.repeat(jnp.sin(ang), 2, axis=1)
    sign = jnp.where(jnp.arange(HEAD_DIM) % 2 == 0, -1.0, 1.0).astype(F32)
    return jnp.tile(cos, (1, n_heads)), jnp.tile(sin * sign[None, :], (1, n_heads))


def _retention_tables(n_heads, ch):
    log_g = jnp.log(1.0 - jnp.exp2(-5.0 - jnp.arange(n_heads, dtype=F32)))
    j = jnp.arange(ch, dtype=F32)
    per_lane = lambda t: jnp.repeat(t, HEAD_DIM, axis=-1)
    qdec = per_lane(jnp.exp(log_g[None, :] * (j + 1.0)[:, None]))
    kdec = per_lane(jnp.exp(log_g[None, :] * (ch - 1.0 - j)[:, None]))
    cdec = per_lane(jnp.exp(log_g * ch)[None, :])
    rel = j[:, None] - j[None, :]
    dmat = jnp.where(rel >= 0, jnp.exp(log_g[:, None, None] * jnp.maximum(rel, 0.0)), 0.0)
    return qdec, kdec, dmat, cdec


def kernel(x, mix_norm_g, w_in, sb_q_norm_g, sb_k_norm_g, ret_norm_g, conv_pw_b, conv_dw_w, conv_dw_b,
           conv_ln_g, conv_ln_b, w_out, mlp_norm_g, w_ff1, w_ff2):
    B, T, D = x.shape
    depth = w_in.shape[0]
    sbw = (D // 256) * HEAD_DIM
    retw = (D // 128) * HEAD_DIM
    cw = D - sbw - retw
    assert T % TOKEN_TILE == 0 or T < TOKEN_TILE
    assert T % RET_CHUNK == 0 and T % SB_BLOCK == 0
    widths = (sbw, retw, cw)
    n_ret_heads = retw // HEAD_DIM

    ones_bd = jnp.asarray(_block_diag_ones(V7X_MXU_DIM, HEAD_DIM), BF16)
    bd_mask = jnp.asarray(_block_diag_ones(V7X_MXU_DIM, HEAD_DIM), F32)
    sfx = jnp.asarray(_suffix_sum_matrix(SB_BLOCK), BF16)
    cos_t, sin_t = _rotary_tables(T, n_ret_heads)
    qdec, kdec, dmat, cdec = _retention_tables(n_ret_heads, RET_CHUNK)
    rk0 = 3 * sbw + retw
    col_scale = jnp.ones((w_in.shape[2],), F32).at[rk0:rk0 + retw].set(HEAD_DIM ** -0.5)

    row = lambda a: a.reshape(1, -1).astype(F32)
    for l in range(depth):
        w_l = (w_in[l] * col_scale[None, :]).astype(BF16)
        q, k, v, rq, rk, rv, sg, hg = _in_proj(
            x, row(mix_norm_g[l]), w_l, ones_bd, cos_t, sin_t,
            row(jnp.tile(sb_q_norm_g[l], sbw // HEAD_DIM)), row(jnp.tile(sb_k_norm_g[l], sbw // HEAD_DIM)),
            row(conv_pw_b[l]), widths)
        y_sb = _sb_attention(q, k, v, sfx)
        y_ret = _retention(rq, rk, rv, sg, qdec, kdec, dmat, cdec, bd_mask, ones_bd, row(ret_norm_g[l]))
        y_conv = _conv(hg, conv_dw_w[l].astype(F32), row(conv_dw_b[l]), row(conv_ln_g[l]), row(conv_ln_b[l]))
        x = _out_mlp(x, y_sb, y_ret, y_conv, w_out[l].astype(BF16), row(mlp_norm_g[l]),
                     w_ff1[l].astype(BF16), w_ff2[l].astype(BF16))
    return x
```

```python
import functools
import math

import jax
import jax.numpy as jnp
import numpy as np
from jax import lax
from jax.experimental import pallas as pl
from jax.experimental.pallas import tpu as pltpu

F32 = jnp.float32
BF16 = jnp.bfloat16

HEAD_DIM = 64
CONV_KERNEL = 31
EPS = 1e-6
ROPE_BASE = 10000.0
LOG2E = 1.4426950408889634
LN2 = 0.6931471805599453

V7X_LANES = 128
V7X_MXU_DIM = 256
V7X_VMEM_BYTES = 64 * 1024 * 1024

SB_DEAD_LOG2 = -150.0
SB_MASKED_SCORE = -1e30

TOKEN_TILE = 512
SB_BLOCK = 128
SB_TILE = 512
SB_LOOKBACK = 2
SB_HEAD_GROUP = 4
RET_CHUNK = 256
CONV_TILE = 512
CONV_HALO = 32
CONV_BLOCK = 128
CONV_WINDOW = 256
CONV_SHIFTS = 8
FF_CHUNK = 1024


def _vmem_limit(nbytes):
    return int(min(nbytes, V7X_VMEM_BYTES - 4 * 1024 * 1024))


def _const_spec(shape):
    nd = len(shape)
    return pl.BlockSpec(shape, lambda *_: (0,) * nd, pipeline_mode=pl.Buffered(1))


def _in_proj_kernel(x_ref, g_ref, w_ref, ones_ref, cos_ref, sin_ref, gq_ref, gk_ref, pb_ref,
                    q_ref, k_ref, v_ref, rq_ref, rk_ref, rv_ref, sg_ref, hg_ref, *, widths):
    sbw, retw, cw = widths
    half = x_ref.shape[1] // 2
    hs = []
    for r0 in (0, half):
        x = x_ref[0, r0:r0 + half, :]
        ms = jnp.mean(x * x, axis=-1, keepdims=True)
        hs.append((x * lax.rsqrt(ms + EPS) * g_ref[...]).astype(BF16))

    def proj(c0, width):
        w = w_ref[:, c0:c0 + width]
        return jnp.concatenate([jnp.dot(h, w, preferred_element_type=F32) for h in hs], axis=0)

    def head_norm(u, gain):
        ss = jnp.dot((u * u).astype(BF16), ones_ref[...], preferred_element_type=F32)
        return u * lax.rsqrt(ss * (1.0 / HEAD_DIM) + EPS) * gain

    def rotary(u):
        n = u.shape[1]
        nxt = pltpu.roll(u, n - 1, axis=1)
        prv = pltpu.roll(u, 1, axis=1)
        lane = lax.broadcasted_iota(jnp.int32, u.shape, 1)
        partner = jnp.where((lane & 1) == 0, nxt, prv)
        return u * cos_ref[...] + partner * sin_ref[...]

    c_q, c_k, c_v = 0, sbw, 2 * sbw
    c_rq, c_rk, c_rv, c_rg = (3 * sbw + i * retw for i in range(4))
    c_a, c_gate = 3 * sbw + 4 * retw, 3 * sbw + 4 * retw + cw
    q_ref[0] = (head_norm(proj(c_q, sbw), gq_ref[...]) * (LOG2E * HEAD_DIM ** -0.5)).astype(BF16)
    k_ref[0] = head_norm(proj(c_k, sbw), gk_ref[...]).astype(BF16)
    a = proj(c_a, cw) + pb_ref[:, :cw]
    gate = proj(c_gate, cw) + pb_ref[:, cw:]
    hg_ref[0] = (a * jax.nn.sigmoid(gate)).astype(BF16)
    g = proj(c_rg, retw)
    sg_ref[0] = (g * jax.nn.sigmoid(g)).astype(BF16)
    rq_ref[0] = rotary(proj(c_rq, retw)).astype(BF16)
    rk_ref[0] = rotary(proj(c_rk, retw)).astype(BF16)
    rv_ref[0] = proj(c_rv, retw).astype(BF16)
    v_ref[0] = proj(c_v, sbw).astype(BF16)


def _in_proj(x, g, w, ones_bd, cos_t, sin_t, gq, gk, pb, widths):
    B, T, D = x.shape
    sbw, retw, cw = widths
    tm = min(TOKEN_TILE, T)
    ncols = w.shape[1]
    grid = (T // tm, B)
    tok = lambda width: pl.BlockSpec((1, tm, width), lambda t, b: (b, t, 0))
    tab = pl.BlockSpec((tm, retw), lambda t, b: (t, 0))
    out_widths = (sbw, sbw, sbw, retw, retw, retw, retw, cw)
    vmem = (2 * tm * D * 4 + D * ncols * 2 + 4 * tm * retw * 4
            + 2 * sum(out_widths) * tm * 2 + 6 * tm * D * 4 + 8 * 1024 * 1024)
    return pl.pallas_call(
        functools.partial(_in_proj_kernel, widths=widths),
        grid=grid,
        in_specs=[tok(D), _const_spec((1, D)), _const_spec((D, ncols)), _const_spec(ones_bd.shape),
                  tab, tab, _const_spec((1, sbw)), _const_spec((1, sbw)), _const_spec((1, 2 * cw))],
        out_specs=[tok(wd) for wd in out_widths],
        out_shape=[jax.ShapeDtypeStruct((B, T, wd), BF16) for wd in out_widths],
        compiler_params=pltpu.CompilerParams(
            dimension_semantics=("arbitrary", "arbitrary"), vmem_limit_bytes=_vmem_limit(vmem)),
        name="in_proj",
    )(x, g, w, ones_bd, cos_t, sin_t, gq, gk, pb)


def _sb_kernel(q_ref, k_ref, v_ref, m_ref, o_ref, vm_ref, acc_ref, c_ref, *, n_heads):
    blk = SB_BLOCK
    tile = q_ref.shape[1]
    seq = k_ref.shape[1]
    t0 = pl.program_id(1) * tile
    row = lax.broadcasted_iota(jnp.int32, (blk, blk), 0)
    col = lax.broadcasted_iota(jnp.int32, (blk, blk), 1)
    causal = col < row

    def head_lanes(shape, h):
        lane = lax.broadcasted_iota(jnp.int32, shape, 1)
        return (lane >= h * HEAD_DIM) & (lane < (h + 1) * HEAD_DIM)

    @pl.when(pl.program_id(1) == 0)
    def _():
        rows = min(SB_TILE, seq)

        def fill(i, _):
            r0 = pl.multiple_of(i * rows, rows)
            vb = v_ref[0, pl.ds(r0, rows), :]
            for h in range(n_heads):
                vm_ref[h, pl.ds(r0, rows), :] = jnp.where(head_lanes(vb.shape, h), vb, jnp.zeros_like(vb))
            return 0

        lax.fori_loop(0, seq // rows, fill, 0)

    all_heads = tuple(range(n_heads))

    def masked_q(off, heads):
        q = q_ref[0, pl.ds(off, blk), :]
        return jnp.concatenate(
            [jnp.where(head_lanes(q.shape, h), q, jnp.zeros_like(q)) for h in heads], axis=0)


    def scores(heads, off, kstart, nblk, diag_last):
        kwin = k_ref[0, pl.ds(kstart, nblk * blk), :]
        z_all = lax.dot_general(masked_q(off, heads), kwin, (((1,), (1,)), ((), ())),
                                preferred_element_type=F32)
        log_beta, split = {}, []
        for h in range(len(heads)):
            for j in range(nblk):
                z = z_all[h * blk:(h + 1) * blk, j * blk:(j + 1) * blk]
                if diag_last and j == nblk - 1:
                    z = jnp.where(causal, z, SB_MASKED_SCORE)
                sp = jnp.maximum(z, 0.0) + jnp.log(1.0 + jnp.exp2(-jnp.abs(z))) * (1.0 / LN2)
                log_beta[h, j] = z - sp
                hi = sp.astype(BF16)
                lo = (sp - hi.astype(F32)).astype(BF16)
                split.append(jnp.concatenate([hi, lo], axis=1))
        return jnp.concatenate(split, axis=0), log_beta

    def weights(split, log_beta, nblk, carries):
        r = jnp.dot(split, m_ref[...], preferred_element_type=F32)
        ws, out_carries = [], []
        for h in range(len(carries)):
            carry = carries[h]
            w = [None] * nblk
            for j in reversed(range(nblk)):
                rr = r[(h * nblk + j) * blk:(h * nblk + j + 1) * blk, :]
                arg = rr[:, :blk] + log_beta[h, j]
                if carry is not None:
                    arg = arg + carry
                w[j] = jnp.exp2(arg).astype(BF16)
                carry = rr[:, blk:] if carry is None else carry + rr[:, blk:]
            out_carries.append(carry)
            ws.extend(w)
        return jnp.concatenate(ws, axis=1), out_carries

    def values(heads, ws, kstart, nblk):
        vs = [vm_ref[h, pl.ds(kstart, nblk * blk), :] for h in heads]
        return jnp.dot(ws, jnp.concatenate(vs, axis=0), preferred_element_type=F32)

    def top_carry(carries):
        top = carries[0]
        for c in carries[1:]:
            top = jnp.maximum(top, c)
        return top

    def straight(lookbacks):
        def run():
            groups = [all_heads[g:g + SB_HEAD_GROUP] for g in range(0, n_heads, SB_HEAD_GROUP)]
            items = [(s, heads) for s in range(len(lookbacks)) for heads in groups]
            kstart = [pl.multiple_of(t0 + (s - lookbacks[s]) * blk, blk) for s in range(len(lookbacks))]
            stage1, stage2, partial, tops = {}, {}, {}, []
            for step in range(len(items) + 2):
                if step < len(items):
                    s, heads = items[step]
                    stage1[step] = scores(heads, s * blk, kstart[s], lookbacks[s] + 1, True)
                if 0 <= step - 1 < len(items):
                    s, heads = items[step - 1]
                    stage2[step - 1] = weights(*stage1.pop(step - 1), lookbacks[s] + 1, [None] * len(heads))
                if 0 <= step - 2 < len(items):
                    s, heads = items[step - 2]
                    ws, carries = stage2.pop(step - 2)
                    pv = values(heads, ws, kstart[s], lookbacks[s] + 1)
                    for h, carry in zip(heads, carries):
                        c_ref[s, h] = carry
                    tops.append(top_carry(carries))
                    partial[s] = pv if s not in partial else partial[s] + pv
                    if heads is groups[-1]:
                        acc_ref[s] = partial[s]
                        o_ref[0, s * blk:(s + 1) * blk, :] = partial.pop(s).astype(o_ref.dtype)
            return jnp.max(top_carry(tops)) > SB_DEAD_LOG2
        return run

    n_sub = tile // blk
    alive = lax.cond(pl.program_id(1) == 0,
                     straight([min(s, SB_LOOKBACK) for s in range(n_sub)]),
                     straight([SB_LOOKBACK] * n_sub))

    @pl.when(alive)
    def _():
        def sub_block(s, _):
            off = pl.multiple_of(s * blk, blk)
            q0 = t0 + off

            def cond(state):
                i, live = state
                return jnp.logical_and(i * blk <= q0, live)

            def body(state):
                i, _ = state
                kstart = pl.multiple_of(q0 - i * blk, blk)
                split, log_beta = scores(all_heads, off, kstart, 1, False)
                ws, carries = weights(split, log_beta, 1, [c_ref[s, h] for h in range(n_heads)])
                acc_ref[s] += values(all_heads, ws, kstart, 1)
                for h in range(n_heads):
                    c_ref[s, h] = carries[h]
                return i + 1, jnp.max(top_carry(carries)) > SB_DEAD_LOG2

            swept = jnp.minimum(q0 // blk, SB_LOOKBACK) + 1
            live = jnp.max(top_carry([c_ref[s, h] for h in range(n_heads)])) > SB_DEAD_LOG2
            lax.while_loop(cond, body, (swept, live))
            o_ref[0, pl.ds(off, blk), :] = acc_ref[s].astype(o_ref.dtype)
            return 0

        lax.fori_loop(0, n_sub, sub_block, 0)


def _sb_attention(q, k, v, m):
    B, T, W = q.shape
    blk = SB_BLOCK
    tile = min(SB_TILE, T)
    n_heads = W // HEAD_DIM
    grid = (B, T // tile)
    seq = pl.BlockSpec((1, T, W), lambda b, i: (b, 0, 0))
    qblk = pl.BlockSpec((1, tile, W), lambda b, i: (b, i, 0))
    vmem = (4 + n_heads) * T * W * 2 + 24 * 1024 * 1024
    return pl.pallas_call(
        functools.partial(_sb_kernel, n_heads=n_heads),
        grid=grid,
        in_specs=[qblk, seq, seq, _const_spec(m.shape)],
        out_specs=qblk,
        out_shape=jax.ShapeDtypeStruct((B, T, W), BF16),
        scratch_shapes=[pltpu.VMEM((n_heads, T, W), BF16), pltpu.VMEM((tile // blk, blk, W), F32),
                        pltpu.VMEM((tile // blk, n_heads, blk, blk), F32)],
        compiler_params=pltpu.CompilerParams(
            dimension_semantics=("arbitrary", "arbitrary"), vmem_limit_bytes=_vmem_limit(vmem)),
        name="sb_attn",
    )(q, k, v, m)


def _ret_kernel(q_ref, k_ref, v_ref, sg_ref, qdec_ref, kdec_ref, dmat_ref, cdec_ref, bd_ref, ones_ref,
                g_ref, o_ref, state_ref, *, group):
    ch = q_ref.shape[1]
    width = q_ref.shape[2]
    heads_per_group = group // HEAD_DIM

    @pl.when(pl.program_id(1) == 0)
    def _():
        state_ref[...] = jnp.zeros_like(state_ref)

    lane = lax.broadcasted_iota(jnp.int32, (ch, group), 1)
    for gi in range(width // group):
        cols = slice(gi * group, (gi + 1) * group)
        q = q_ref[0, :, cols]
        k = k_ref[0, :, cols]
        v = v_ref[0, :, cols]
        state = state_ref[gi]
        qd = (q.astype(F32) * qdec_ref[:, cols]).astype(BF16)
        out = jnp.dot(qd, state.astype(BF16), preferred_element_type=F32)
        for hh in range(heads_per_group):
            in_head = (lane >= hh * HEAD_DIM) & (lane < (hh + 1) * HEAD_DIM)
            qm = jnp.where(in_head, q, jnp.zeros_like(q))
            sc = lax.dot_general(qm, k, (((1,), (1,)), ((), ())), preferred_element_type=F32)
            sc = sc * dmat_ref[gi * heads_per_group + hh]
            oh = jnp.dot(sc.astype(BF16), v, preferred_element_type=F32)
            out = out + jnp.where(in_head, oh, 0.0)
        kd = (k.astype(F32) * kdec_ref[:, cols]).astype(BF16)
        kv = lax.dot_general(kd, v, (((0,), (0,)), ((), ())), preferred_element_type=F32)
        state_ref[gi] = state * cdec_ref[:, cols] + kv * bd_ref[...]
        ss = jnp.dot((out * out).astype(BF16), ones_ref[...], preferred_element_type=F32)
        yn = out * lax.rsqrt(ss * (1.0 / HEAD_DIM) + EPS) * g_ref[:, cols]
        o_ref[0, :, cols] = (yn * sg_ref[0, :, cols].astype(F32)).astype(o_ref.dtype)


def _retention(rq, rk, rv, sg, qdec, kdec, dmat, cdec, bd, ones_bd, g):
    B, T, W = rq.shape
    ch = qdec.shape[0]
    group = V7X_MXU_DIM
    grid = (B, T // ch)
    tok = pl.BlockSpec((1, ch, W), lambda b, i: (b, i, 0))
    vmem = 2 * 5 * ch * W * 2 + 2 * ch * W * 4 + dmat.size * 4 + 24 * 1024 * 1024
    return pl.pallas_call(
        functools.partial(_ret_kernel, group=group),
        grid=grid,
        in_specs=[tok, tok, tok, tok, _const_spec(qdec.shape), _const_spec(kdec.shape), _const_spec(dmat.shape),
                  _const_spec(cdec.shape), _const_spec(bd.shape), _const_spec(ones_bd.shape), _const_spec(g.shape)],
        out_specs=tok,
        out_shape=jax.ShapeDtypeStruct((B, T, W), BF16),
        scratch_shapes=[pltpu.VMEM((W // group, group, group), F32)],
        compiler_params=pltpu.CompilerParams(
            dimension_semantics=("arbitrary", "arbitrary"), vmem_limit_bytes=_vmem_limit(vmem)),
        name="retention",
    )(rq, rk, rv, sg, qdec, kdec, dmat, cdec, bd, ones_bd, g)


def _conv_kernel(h_ref, s_ref, w_ref, b_ref, lg_ref, lb_ref, o_ref, pad_ref):
    tc = h_ref.shape[1]
    halo = CONV_HALO
    rows = CONV_BLOCK
    n_shift = s_ref.shape[0]

    @pl.when(pl.program_id(1) == 0)
    def _():
        pad_ref[0:halo, :] = jnp.zeros((halo, pad_ref.shape[1]), pad_ref.dtype)
        pad_ref[halo + tc:, :] = jnp.zeros((pad_ref.shape[0] - halo - tc, pad_ref.shape[1]), pad_ref.dtype)

    pad_ref[halo:halo + tc, :] = h_ref[0]
    first = halo - (CONV_KERNEL - 1)
    for r0 in range(0, tc, rows):
        window = pad_ref[r0:r0 + CONV_WINDOW, :]
        acc = jnp.zeros((rows, pad_ref.shape[1]), F32) + b_ref[...]
        for shift in range(n_shift):
            moved = jnp.dot(s_ref[shift], window, preferred_element_type=F32)
            for tap in range(CONV_KERNEL):
                if (first + tap) % n_shift == shift:
                    a = (first + tap) // n_shift * n_shift
                    acc = acc + moved[a:a + rows, :] * w_ref[tap:tap + 1, :]
        mu = jnp.mean(acc, axis=-1, keepdims=True)
        xc = acc - mu
        var = jnp.mean(xc * xc, axis=-1, keepdims=True)
        yn = xc * lax.rsqrt(var + EPS) * lg_ref[...] + lb_ref[...]
        o_ref[0, r0:r0 + rows, :] = (yn * jax.nn.sigmoid(yn)).astype(o_ref.dtype)
    pad_ref[0:halo, :] = pad_ref[tc:tc + halo, :]


def _conv(hglu, shifts, dw_w, dw_b, ln_g, ln_b):
    B, T, C = hglu.shape
    tc = min(CONV_TILE, T)
    grid = (B, T // tc)
    tok = pl.BlockSpec((1, tc, C), lambda b, i: (b, i, 0))
    return pl.pallas_call(
        _conv_kernel,
        grid=grid,
        in_specs=[tok, _const_spec(shifts.shape), _const_spec(dw_w.shape),
                  _const_spec((1, C)), _const_spec((1, C)), _const_spec((1, C))],
        out_specs=tok,
        out_shape=jax.ShapeDtypeStruct((B, T, C), BF16),
        scratch_shapes=[pltpu.VMEM((tc + CONV_WINDOW - CONV_BLOCK, C), BF16)],
        compiler_params=pltpu.CompilerParams(dimension_semantics=("arbitrary", "arbitrary")),
        name="conv",
    )(hglu, shifts, dw_w, dw_b, ln_g, ln_b)


def _out_mlp_kernel(x_ref, ysb_ref, yret_ref, yconv_ref, wo_ref, g_ref, w1_ref, w2_ref, o_ref):
    mixed = jnp.concatenate([ysb_ref[0], yret_ref[0], yconv_ref[0]], axis=1)
    x1 = x_ref[0] + jnp.dot(mixed, wo_ref[...], preferred_element_type=F32)
    ms = jnp.mean(x1 * x1, axis=-1, keepdims=True)
    h = (x1 * lax.rsqrt(ms + EPS) * g_ref[...]).astype(BF16)
    acc = x1
    for c0 in range(0, w1_ref.shape[1], FF_CHUNK):
        f = jnp.maximum(jnp.dot(h, w1_ref[:, c0:c0 + FF_CHUNK], preferred_element_type=F32), 0.0)
        acc = acc + jnp.dot((f * f).astype(BF16), w2_ref[c0:c0 + FF_CHUNK, :], preferred_element_type=F32)
    o_ref[0] = acc


def _out_mlp(x, ysb, yret, yconv, wo, g, w1, w2):
    B, T, D = x.shape
    tm = min(TOKEN_TILE, T)
    grid = (B, T // tm)
    tok = lambda width: pl.BlockSpec((1, tm, width), lambda b, t: (b, t, 0))
    dff = w1.shape[1]
    vmem = (D * D + 2 * D * dff) * 2 + 4 * tm * D * 4 + 4 * tm * D * 2 + tm * FF_CHUNK * 12 + 12 * 1024 * 1024
    return pl.pallas_call(
        _out_mlp_kernel,
        grid=grid,
        in_specs=[tok(D), tok(ysb.shape[2]), tok(yret.shape[2]), tok(yconv.shape[2]),
                  _const_spec(wo.shape), _const_spec((1, D)), _const_spec(w1.shape), _const_spec(w2.shape)],
        out_specs=tok(D),
        out_shape=jax.ShapeDtypeStruct((B, T, D), F32),
        compiler_params=pltpu.CompilerParams(
            dimension_semantics=("arbitrary", "arbitrary"), vmem_limit_bytes=_vmem_limit(vmem)),
        name="out_mlp",
    )(x, ysb, yret, yconv, wo, g, w1, w2)


def _block_diag_ones(n, block):
    i = np.arange(n)
    return (i[:, None] // block == i[None, :] // block).astype(np.float32)


def _shift_matrices():
    r = np.arange(CONV_BLOCK + CONV_HALO)[None, :, None]
    j = np.arange(CONV_WINDOW)[None, None, :]
    b = np.arange(CONV_SHIFTS)[:, None, None]
    return (j == r + b).astype(np.float32)


def _suffix_sum_matrix(blk):
    j = np.arange(2 * blk)[:, None] % blk
    s = np.arange(2 * blk)[None, :]
    return -((s >= blk) | (j > s)).astype(np.float32)


def _rotary_tables(T, n_heads):
    inv = 1.0 / (ROPE_BASE ** jnp.linspace(0.0, 1.0, HEAD_DIM // 2, dtype=F32))
    ang = jnp.arange(T).astype(F32)[:, None] * inv[None, :]
    cos = jnp.repeat(jnp.cos(ang), 2, axis=1)
    sin = jnp.repeat(jnp.sin(ang), 2, axis=1)
    sign = jnp.where(jnp.arange(HEAD_DIM) % 2 == 0, -1.0, 1.0).astype(F32)
    return jnp.tile(cos, (1, n_heads)), jnp.tile(sin * sign[None, :], (1, n_heads))


def _retention_tables(n_heads, ch):
    log_g = jnp.log(1.0 - jnp.exp2(-5.0 - jnp.arange(n_heads, dtype=F32)))
    j = jnp.arange(ch, dtype=F32)
    per_lane = lambda t: jnp.repeat(t, HEAD_DIM, axis=-1)
    qdec = per_lane(jnp.exp(log_g[None, :] * (j + 1.0)[:, None]))
    kdec = per_lane(jnp.exp(log_g[None, :] * (ch - 1.0 - j)[:, None]))
    cdec = per_lane(jnp.exp(log_g * ch)[None, :])
    rel = j[:, None] - j[None, :]
    dmat = jnp.where(rel >= 0, jnp.exp(log_g[:, None, None] * jnp.maximum(rel, 0.0)), 0.0)
    return qdec, kdec, dmat, cdec


def kernel(x, mix_norm_g, w_in, sb_q_norm_g, sb_k_norm_g, ret_norm_g, conv_pw_b, conv_dw_w, conv_dw_b,
           conv_ln_g, conv_ln_b, w_out, mlp_norm_g, w_ff1, w_ff2):
    B, T, D = x.shape
    depth = w_in.shape[0]
    sbw = (D // 256) * HEAD_DIM
    retw = (D // 128) * HEAD_DIM
    cw = D - sbw - retw
    assert T % TOKEN_TILE == 0 or T < TOKEN_TILE
    assert T % RET_CHUNK == 0 and T % SB_BLOCK == 0
    widths = (sbw, retw, cw)
    n_ret_heads = retw // HEAD_DIM

    ones_bd = jnp.asarray(_block_diag_ones(V7X_MXU_DIM, HEAD_DIM), BF16)
    bd_mask = jnp.asarray(_block_diag_ones(V7X_MXU_DIM, HEAD_DIM), F32)
    sfx = jnp.asarray(_suffix_sum_matrix(SB_BLOCK), BF16)
    shifts = jnp.asarray(_shift_matrices(), BF16)
    cos_t, sin_t = _rotary_tables(T, n_ret_heads)
    qdec, kdec, dmat, cdec = _retention_tables(n_ret_heads, RET_CHUNK)
    rk0 = 3 * sbw + retw
    col_scale = jnp.ones((w_in.shape[2],), F32).at[rk0:rk0 + retw].set(HEAD_DIM ** -0.5)

    row = lambda a: a.reshape(1, -1).astype(F32)
    for l in range(depth):
        w_l = (w_in[l] * col_scale[None, :]).astype(BF16)
        q, k, v, rq, rk, rv, sg, hg = _in_proj(
            x, row(mix_norm_g[l]), w_l, ones_bd, cos_t, sin_t,
            row(jnp.tile(sb_q_norm_g[l], sbw // HEAD_DIM)), row(jnp.tile(sb_k_norm_g[l], sbw // HEAD_DIM)),
            row(conv_pw_b[l]), widths)
        y_sb = _sb_attention(q, k, v, sfx)
        y_ret = _retention(rq, rk, rv, sg, qdec, kdec, dmat, cdec, bd_mask, ones_bd, row(ret_norm_g[l]))
        y_conv = _conv(hg, shifts, conv_dw_w[l].astype(F32), row(conv_dw_b[l]), row(conv_ln_g[l]),
                       row(conv_ln_b[l]))
        x = _out_mlp(x, y_sb, y_ret, y_conv, w_out[l].astype(BF16), row(mlp_norm_g[l]),
                     w_ff1[l].astype(BF16), w_ff2[l].astype(BF16))
    return x
```

```python
import functools
import math

import jax
import jax.numpy as jnp
import numpy as np
from jax import lax
from jax.experimental import pallas as pl
from jax.experimental.pallas import tpu as pltpu

F32 = jnp.float32
BF16 = jnp.bfloat16

HEAD_DIM = 64
CONV_KERNEL = 31
EPS = 1e-6
ROPE_BASE = 10000.0
LOG2E = 1.4426950408889634
LN2 = 0.6931471805599453

V7X_LANES = 128
V7X_MXU_DIM = 256
V7X_VMEM_BYTES = 64 * 1024 * 1024

SB_DEAD_LOG2 = -150.0
SB_MASKED_SCORE = -1e30

TOKEN_TILE = 512
SB_BLOCK = 128
SB_TILE = 512
SB_LOOKBACK = 2
SB_HEAD_GROUP = 4
RET_CHUNK = 256
CONV_HALO = 32
CONV_BLOCK = 128
CONV_WINDOW = 256
CONV_SHIFTS = 8
FF_CHUNK = 1024


def _vmem_limit(nbytes):
    return int(min(nbytes, V7X_VMEM_BYTES - 4 * 1024 * 1024))


def _const_spec(shape):
    nd = len(shape)
    return pl.BlockSpec(shape, lambda *_: (0,) * nd, pipeline_mode=pl.Buffered(1))


def _mix_kernel(x_ref, g_ref, w_ref, ones_ref, cos_ref, sin_ref, gq_ref, gk_ref, pb_ref,
                shift_ref, dw_ref, db_ref, lg_ref, lb_ref,
                qdec_ref, kdec_ref, dmat_ref, cdec_ref, bd_ref, rg_ref,
                q_ref, k_ref, v_ref, yret_ref, yconv_ref,
                pad_ref, state_ref, rq_s, rk_s, rv_s, sg_s, *, widths):
    sbw, retw, cw = widths
    tm = x_ref.shape[1]
    halo = CONV_HALO

    @pl.when(pl.program_id(1) == 0)
    def _():
        state_ref[...] = jnp.zeros_like(state_ref)
        pad_ref[0:halo, :] = jnp.zeros((halo, cw), pad_ref.dtype)
        pad_ref[halo + tm:, :] = jnp.zeros((pad_ref.shape[0] - halo - tm, cw), pad_ref.dtype)

    x = x_ref[0]
    ms = jnp.mean(x * x, axis=-1, keepdims=True)
    h = (x * lax.rsqrt(ms + EPS) * g_ref[...]).astype(BF16)

    def proj(c0, width):
        return jnp.dot(h, w_ref[:, c0:c0 + width], preferred_element_type=F32)

    def head_norm(u, gain):
        ss = jnp.dot((u * u).astype(BF16), ones_ref[...], preferred_element_type=F32)
        return u * lax.rsqrt(ss * (1.0 / HEAD_DIM) + EPS) * gain

    def rotary(u):
        n = u.shape[1]
        nxt = pltpu.roll(u, n - 1, axis=1)
        prv = pltpu.roll(u, 1, axis=1)
        lane = lax.broadcasted_iota(jnp.int32, u.shape, 1)
        partner = jnp.where((lane & 1) == 0, nxt, prv)
        return u * cos_ref[...] + partner * sin_ref[...]

    c_q, c_v = 0, 2 * sbw
    c_rq, c_rk, c_rv, c_rg = (3 * sbw + i * retw for i in range(4))
    c_a = 3 * sbw + 4 * retw

    def project_conv():
        ag = proj(c_a, 2 * cw) + pb_ref[...]
        pad_ref[halo:halo + tm, :] = (ag[:, :cw] * jax.nn.sigmoid(ag[:, cw:])).astype(pad_ref.dtype)

    def project_ret_gate():
        gate = proj(c_rg, retw)
        sg_s[...] = (gate * jax.nn.sigmoid(gate)).astype(BF16)

    def project_ret_q():
        rq_s[...] = rotary(proj(c_rq, retw)).astype(BF16)

    def project_ret_k():
        rk_s[...] = rotary(proj(c_rk, retw)).astype(BF16)

    def project_ret_v():
        rv_s[...] = proj(c_rv, retw).astype(BF16)

    def project_sb_qk():
        qk = proj(c_q, 2 * sbw)
        q_ref[0] = (head_norm(qk[:, :sbw], gq_ref[...]) * (LOG2E * HEAD_DIM ** -0.5)).astype(BF16)
        k_ref[0] = head_norm(qk[:, sbw:], gk_ref[...]).astype(BF16)

    def project_sb_v():
        v_ref[0] = proj(c_v, sbw).astype(BF16)

    n_shift = shift_ref.shape[0]
    first = halo - (CONV_KERNEL - 1)

    def conv_block(r0):
        window = pad_ref[r0:r0 + CONV_WINDOW, :]
        acc = jnp.zeros((CONV_BLOCK, cw), F32) + db_ref[...]
        for shift in range(n_shift):
            moved = jnp.dot(shift_ref[shift], window, preferred_element_type=F32)
            for tap in range(CONV_KERNEL):
                if (first + tap) % n_shift == shift:
                    a = (first + tap) // n_shift * n_shift
                    acc = acc + moved[a:a + CONV_BLOCK, :] * dw_ref[tap:tap + 1, :]
        mu = jnp.mean(acc, axis=-1, keepdims=True)
        xc = acc - mu
        var = jnp.mean(xc * xc, axis=-1, keepdims=True)
        yn = xc * lax.rsqrt(var + EPS) * lg_ref[...] + lb_ref[...]
        yconv_ref[0, r0:r0 + CONV_BLOCK, :] = (yn * jax.nn.sigmoid(yn)).astype(yconv_ref.dtype)

    ch = qdec_ref.shape[0]
    group = bd_ref.shape[0]
    heads_per_group = group // HEAD_DIM
    lane = lax.broadcasted_iota(jnp.int32, (ch, group), 1)

    def retention_scores(c0, gi):
        cols = slice(gi * group, (gi + 1) * group)
        q = rq_s[c0:c0 + ch, cols]
        k = rk_s[c0:c0 + ch, cols]
        scores = []
        for hh in range(heads_per_group):
            in_head = (lane >= hh * HEAD_DIM) & (lane < (hh + 1) * HEAD_DIM)
            qm = jnp.where(in_head, q, jnp.zeros_like(q))
            sc = lax.dot_general(qm, k, (((1,), (1,)), ((), ())), preferred_element_type=F32)
            scores.append((sc * dmat_ref[gi * heads_per_group + hh]).astype(BF16))
        return scores

    def retention_chunk(c0, gi, scores):
        cols = slice(gi * group, (gi + 1) * group)
        q = rq_s[c0:c0 + ch, cols]
        k = rk_s[c0:c0 + ch, cols]
        v = rv_s[c0:c0 + ch, cols]
        state = state_ref[gi]
        qd = (q.astype(F32) * qdec_ref[:, cols]).astype(BF16)
        out = jnp.dot(qd, state.astype(BF16), preferred_element_type=F32)
        for hh in range(heads_per_group):
            in_head = (lane >= hh * HEAD_DIM) & (lane < (hh + 1) * HEAD_DIM)
            oh = jnp.dot(scores[hh], v, preferred_element_type=F32)
            out = out + jnp.where(in_head, oh, 0.0)
        kd = (k.astype(F32) * kdec_ref[:, cols]).astype(BF16)
        kv = lax.dot_general(kd, v, (((0,), (0,)), ((), ())), preferred_element_type=F32)
        state_ref[gi] = state * cdec_ref[:, cols] + kv * bd_ref[...]
        ss = jnp.dot((out * out).astype(BF16), ones_ref[...], preferred_element_type=F32)
        yn = out * lax.rsqrt(ss * (1.0 / HEAD_DIM) + EPS) * rg_ref[:, cols]
        yret_ref[0, c0:c0 + ch, cols] = (yn * sg_s[c0:c0 + ch, cols].astype(F32)).astype(yret_ref.dtype)

    conv_blocks = list(range(0, tm, CONV_BLOCK))
    ret_chunks = [(c0, gi) for c0 in range(0, tm, ch) for gi in range(retw // group)]
    project_conv()
    fillers = [project_ret_gate, project_ret_q, project_ret_k, project_ret_v]
    for i, r0 in enumerate(conv_blocks):
        conv_block(r0)
        if i < len(fillers):
            fillers[i]()
    for f in fillers[len(conv_blocks):]:
        f()
    pad_ref[0:halo, :] = pad_ref[tm:tm + halo, :]
    fillers = [project_sb_qk, project_sb_v]
    pending = {}
    for i in range(len(ret_chunks) + 1):
        if i < len(ret_chunks):
            pending[i] = retention_scores(*ret_chunks[i])
        if i < len(fillers):
            fillers[i]()
        if i >= 1:
            retention_chunk(*ret_chunks[i - 1], pending.pop(i - 1))
    for f in fillers[len(ret_chunks) + 1:]:
        f()


def _mix_proj(x, g, w, ones_bd, cos_t, sin_t, gq, gk, pb, shifts, dw_w, dw_b, ln_g, ln_b,
              qdec, kdec, dmat, cdec, bd, ret_g, widths):
    B, T, D = x.shape
    sbw, retw, cw = widths
    tm = min(TOKEN_TILE, T)
    assert tm % qdec.shape[0] == 0 and tm % CONV_BLOCK == 0
    grid = (B, T // tm)
    tok = lambda width: pl.BlockSpec((1, tm, width), lambda b, t: (b, t, 0))
    tab = pl.BlockSpec((tm, retw), lambda b, t: (t, 0))
    consts = (g, w, ones_bd)
    consts2 = (gq, gk, pb, shifts, dw_w, dw_b, ln_g, ln_b, qdec, kdec, dmat, cdec, bd, ret_g)
    out_widths = (sbw, sbw, sbw, retw, cw)
    vmem = (2 * tm * D * 4 + w.size * 2 + 4 * tm * retw * 4 + dmat.size * 4 + 4 * tm * retw * 2
            + 2 * sum(out_widths) * tm * 2 + 24 * 1024 * 1024)
    return pl.pallas_call(
        functools.partial(_mix_kernel, widths=widths),
        grid=grid,
        in_specs=[tok(D)] + [_const_spec(a.shape) for a in consts] + [tab, tab]
                 + [_const_spec(a.shape) for a in consts2],
        out_specs=[tok(wd) for wd in out_widths],
        out_shape=[jax.ShapeDtypeStruct((B, T, wd), BF16) for wd in out_widths],
        scratch_shapes=[pltpu.VMEM((tm + CONV_WINDOW - CONV_BLOCK, cw), BF16),
                        pltpu.VMEM((retw // bd.shape[0], bd.shape[0], bd.shape[0]), F32)]
                       + [pltpu.VMEM((tm, retw), BF16)] * 4,
        compiler_params=pltpu.CompilerParams(
            dimension_semantics=("arbitrary", "arbitrary"), vmem_limit_bytes=_vmem_limit(vmem)),
        name="mix_proj",
    )(x, g, w, ones_bd, cos_t, sin_t, gq, gk, pb, shifts, dw_w, dw_b, ln_g, ln_b,
      qdec, kdec, dmat, cdec, bd, ret_g)


def _sb_kernel(q_ref, k_ref, v_ref, m_ref, o_ref, vm_ref, acc_ref, c_ref, *, n_heads):
    blk = SB_BLOCK
    tile = q_ref.shape[1]
    seq = k_ref.shape[1]
    t0 = pl.program_id(1) * tile
    row = lax.broadcasted_iota(jnp.int32, (blk, blk), 0)
    col = lax.broadcasted_iota(jnp.int32, (blk, blk), 1)
    causal = col < row

    def head_lanes(shape, h):
        lane = lax.broadcasted_iota(jnp.int32, shape, 1)
        return (lane >= h * HEAD_DIM) & (lane < (h + 1) * HEAD_DIM)

    @pl.when(pl.program_id(1) == 0)
    def _():
        rows = min(SB_TILE, seq)

        def fill(i, _):
            r0 = pl.multiple_of(i * rows, rows)
            vb = v_ref[0, pl.ds(r0, rows), :]
            for h in range(n_heads):
                vm_ref[h, pl.ds(r0, rows), :] = jnp.where(head_lanes(vb.shape, h), vb, jnp.zeros_like(vb))
            return 0

        lax.fori_loop(0, seq // rows, fill, 0)

    all_heads = tuple(range(n_heads))

    def masked_q(off, heads):
        q = q_ref[0, pl.ds(off, blk), :]
        return jnp.concatenate(
            [jnp.where(head_lanes(q.shape, h), q, jnp.zeros_like(q)) for h in heads], axis=0)


    def scores(heads, off, kstart, nblk, diag_last):
        kwin = k_ref[0, pl.ds(kstart, nblk * blk), :]
        z_all = lax.dot_general(masked_q(off, heads), kwin, (((1,), (1,)), ((), ())),
                                preferred_element_type=F32)
        log_beta, split = {}, []
        for h in range(len(heads)):
            for j in range(nblk):
                z = z_all[h * blk:(h + 1) * blk, j * blk:(j + 1) * blk]
                if diag_last and j == nblk - 1:
                    z = jnp.where(causal, z, SB_MASKED_SCORE)
                sp = jnp.maximum(z, 0.0) + jnp.log(1.0 + jnp.exp2(-jnp.abs(z))) * (1.0 / LN2)
                log_beta[h, j] = z - sp
                hi = sp.astype(BF16)
                lo = (sp - hi.astype(F32)).astype(BF16)
                split.append(jnp.concatenate([hi, lo], axis=1))
        return jnp.concatenate(split, axis=0), log_beta

    def weights(split, log_beta, nblk, carries):
        r = jnp.dot(split, m_ref[...], preferred_element_type=F32)
        ws, out_carries = [], []
        for h in range(len(carries)):
            carry = carries[h]
            w = [None] * nblk
            for j in reversed(range(nblk)):
                rr = r[(h * nblk + j) * blk:(h * nblk + j + 1) * blk, :]
                arg = rr[:, :blk] + log_beta[h, j]
                if carry is not None:
                    arg = arg + carry
                w[j] = jnp.exp2(arg).astype(BF16)
                carry = rr[:, blk:] if carry is None else carry + rr[:, blk:]
            out_carries.append(carry)
            ws.extend(w)
        return jnp.concatenate(ws, axis=1), out_carries

    def values(heads, ws, kstart, nblk):
        vs = [vm_ref[h, pl.ds(kstart, nblk * blk), :] for h in heads]
        return jnp.dot(ws, jnp.concatenate(vs, axis=0), preferred_element_type=F32)

    def top_carry(carries):
        top = carries[0]
        for c in carries[1:]:
            top = jnp.maximum(top, c)
        return top

    def straight(lookbacks):
        def run():
            groups = [all_heads[g:g + SB_HEAD_GROUP] for g in range(0, n_heads, SB_HEAD_GROUP)]
            items = [(s, heads) for s in range(len(lookbacks)) for heads in groups]
            kstart = [pl.multiple_of(t0 + (s - lookbacks[s]) * blk, blk) for s in range(len(lookbacks))]
            stage1, stage2, partial, tops = {}, {}, {}, []
            for step in range(len(items) + 2):
                if step < len(items):
                    s, heads = items[step]
                    stage1[step] = scores(heads, s * blk, kstart[s], lookbacks[s] + 1, True)
                if 0 <= step - 1 < len(items):
                    s, heads = items[step - 1]
                    stage2[step - 1] = weights(*stage1.pop(step - 1), lookbacks[s] + 1, [None] * len(heads))
                if 0 <= step - 2 < len(items):
                    s, heads = items[step - 2]
                    ws, carries = stage2.pop(step - 2)
                    pv = values(heads, ws, kstart[s], lookbacks[s] + 1)
                    for h, carry in zip(heads, carries):
                        c_ref[s, h] = carry
                    tops.append(top_carry(carries))
                    partial[s] = pv if s not in partial else partial[s] + pv
                    if heads is groups[-1]:
                        acc_ref[s] = partial[s]
                        o_ref[0, s * blk:(s + 1) * blk, :] = partial.pop(s).astype(o_ref.dtype)
            return jnp.max(top_carry(tops)) > SB_DEAD_LOG2
        return run

    n_sub = tile // blk
    alive = lax.cond(pl.program_id(1) == 0,
                     straight([min(s, SB_LOOKBACK) for s in range(n_sub)]),
                     straight([SB_LOOKBACK] * n_sub))

    @pl.when(alive)
    def _():
        def sub_block(s, _):
            off = pl.multiple_of(s * blk, blk)
            q0 = t0 + off

            def cond(state):
                i, live = state
                return jnp.logical_and(i * blk <= q0, live)

            def body(state):
                i, _ = state
                kstart = pl.multiple_of(q0 - i * blk, blk)
                split, log_beta = scores(all_heads, off, kstart, 1, False)
                ws, carries = weights(split, log_beta, 1, [c_ref[s, h] for h in range(n_heads)])
                acc_ref[s] += values(all_heads, ws, kstart, 1)
                for h in range(n_heads):
                    c_ref[s, h] = carries[h]
                return i + 1, jnp.max(top_carry(carries)) > SB_DEAD_LOG2

            swept = jnp.minimum(q0 // blk, SB_LOOKBACK) + 1
            live = jnp.max(top_carry([c_ref[s, h] for h in range(n_heads)])) > SB_DEAD_LOG2
            lax.while_loop(cond, body, (swept, live))
            o_ref[0, pl.ds(off, blk), :] = acc_ref[s].astype(o_ref.dtype)
            return 0

        lax.fori_loop(0, n_sub, sub_block, 0)


def _sb_attention(q, k, v, m):
    B, T, W = q.shape
    blk = SB_BLOCK
    tile = min(SB_TILE, T)
    n_heads = W // HEAD_DIM
    grid = (B, T // tile)
    seq = pl.BlockSpec((1, T, W), lambda b, i: (b, 0, 0))
    qblk = pl.BlockSpec((1, tile, W), lambda b, i: (b, i, 0))
    vmem = (4 + n_heads) * T * W * 2 + 24 * 1024 * 1024
    return pl.pallas_call(
        functools.partial(_sb_kernel, n_heads=n_heads),
        grid=grid,
        in_specs=[qblk, seq, seq, _const_spec(m.shape)],
        out_specs=qblk,
        out_shape=jax.ShapeDtypeStruct((B, T, W), BF16),
        scratch_shapes=[pltpu.VMEM((n_heads, T, W), BF16), pltpu.VMEM((tile // blk, blk, W), F32),
                        pltpu.VMEM((tile // blk, n_heads, blk, blk), F32)],
        compiler_params=pltpu.CompilerParams(
            dimension_semantics=("arbitrary", "arbitrary"), vmem_limit_bytes=_vmem_limit(vmem)),
        name="sb_attn",
    )(q, k, v, m)


def _out_mlp_kernel(x_ref, ysb_ref, yret_ref, yconv_ref, wo_ref, g_ref, w1_ref, w2_ref, o_ref):
    mixed = jnp.concatenate([ysb_ref[0], yret_ref[0], yconv_ref[0]], axis=1)
    x1 = x_ref[0] + jnp.dot(mixed, wo_ref[...], preferred_element_type=F32)
    ms = jnp.mean(x1 * x1, axis=-1, keepdims=True)
    h = (x1 * lax.rsqrt(ms + EPS) * g_ref[...]).astype(BF16)
    acc = x1
    for c0 in range(0, w1_ref.shape[1], FF_CHUNK):
        f = jnp.maximum(jnp.dot(h, w1_ref[:, c0:c0 + FF_CHUNK], preferred_element_type=F32), 0.0)
        acc = acc + jnp.dot((f * f).astype(BF16), w2_ref[c0:c0 + FF_CHUNK, :], preferred_element_type=F32)
    o_ref[0] = acc


def _out_mlp(x, ysb, yret, yconv, wo, g, w1, w2):
    B, T, D = x.shape
    tm = min(TOKEN_TILE, T)
    grid = (B, T // tm)
    tok = lambda width: pl.BlockSpec((1, tm, width), lambda b, t: (b, t, 0))
    dff = w1.shape[1]
    vmem = (D * D + 2 * D * dff) * 2 + 4 * tm * D * 4 + 4 * tm * D * 2 + tm * FF_CHUNK * 12 + 12 * 1024 * 1024
    return pl.pallas_call(
        _out_mlp_kernel,
        grid=grid,
        in_specs=[tok(D), tok(ysb.shape[2]), tok(yret.shape[2]), tok(yconv.shape[2]),
                  _const_spec(wo.shape), _const_spec((1, D)), _const_spec(w1.shape), _const_spec(w2.shape)],
        out_specs=tok(D),
        out_shape=jax.ShapeDtypeStruct((B, T, D), F32),
        compiler_params=pltpu.CompilerParams(
            dimension_semantics=("arbitrary", "arbitrary"), vmem_limit_bytes=_vmem_limit(vmem)),
        name="out_mlp",
    )(x, ysb, yret, yconv, wo, g, w1, w2)


def _block_diag_ones(n, block):
    i = np.arange(n)
    return (i[:, None] // block == i[None, :] // block).astype(np.float32)


def _shift_matrices():
    r = np.arange(CONV_BLOCK + CONV_HALO)[None, :, None]
    j = np.arange(CONV_WINDOW)[None, None, :]
    b = np.arange(CONV_SHIFTS)[:, None, None]
    return (j == r + b).astype(np.float32)


def _suffix_sum_matrix(blk):
    j = np.arange(2 * blk)[:, None] % blk
    s = np.arange(2 * blk)[None, :]
    return -((s >= blk) | (j > s)).astype(np.float32)


def _rotary_tables(T, n_heads):
    inv = 1.0 / (ROPE_BASE ** jnp.linspace(0.0, 1.0, HEAD_DIM // 2, dtype=F32))
    ang = jnp.arange(T).astype(F32)[:, None] * inv[None, :]
    cos = jnp.repeat(jnp.cos(ang), 2, axis=1)
    sin = jnp.repeat(jnp.sin(ang), 2, axis=1)
    sign = jnp.where(jnp.arange(HEAD_DIM) % 2 == 0, -1.0, 1.0).astype(F32)
    return jnp.tile(cos, (1, n_heads)), jnp.tile(sin * sign[None, :], (1, n_heads))


def _retention_tables(n_heads, ch):
    log_g = jnp.log(1.0 - jnp.exp2(-5.0 - jnp.arange(n_heads, dtype=F32)))
    j = jnp.arange(ch, dtype=F32)
    per_lane = lambda t: jnp.repeat(t, HEAD_DIM, axis=-1)
    qdec = per_lane(jnp.exp(log_g[None, :] * (j + 1.0)[:, None]))
    kdec = per_lane(jnp.exp(log_g[None, :] * (ch - 1.0 - j)[:, None]))
    cdec = per_lane(jnp.exp(log_g * ch)[None, :])
    rel = j[:, None] - j[None, :]
    dmat = jnp.where(rel >= 0, jnp.exp(log_g[:, None, None] * jnp.maximum(rel, 0.0)), 0.0)
    return qdec, kdec, dmat, cdec


def kernel(x, mix_norm_g, w_in, sb_q_norm_g, sb_k_norm_g, ret_norm_g, conv_pw_b, conv_dw_w, conv_dw_b,
           conv_ln_g, conv_ln_b, w_out, mlp_norm_g, w_ff1, w_ff2):
    B, T, D = x.shape
    depth = w_in.shape[0]
    sbw = (D // 256) * HEAD_DIM
    retw = (D // 128) * HEAD_DIM
    cw = D - sbw - retw
    assert T % TOKEN_TILE == 0 or T < TOKEN_TILE
    assert T % RET_CHUNK == 0 and T % SB_BLOCK == 0
    widths = (sbw, retw, cw)
    n_ret_heads = retw // HEAD_DIM

    ones_bd = jnp.asarray(_block_diag_ones(V7X_MXU_DIM, HEAD_DIM), BF16)
    bd_mask = jnp.asarray(_block_diag_ones(V7X_MXU_DIM, HEAD_DIM), F32)
    sfx = jnp.asarray(_suffix_sum_matrix(SB_BLOCK), BF16)
    shifts = jnp.asarray(_shift_matrices(), BF16)
    cos_t, sin_t = _rotary_tables(T, n_ret_heads)
    qdec, kdec, dmat, cdec = _retention_tables(n_ret_heads, RET_CHUNK)
    rk0 = 3 * sbw + retw
    col_scale = jnp.ones((w_in.shape[2],), F32).at[rk0:rk0 + retw].set(HEAD_DIM ** -0.5)

    row = lambda a: a.reshape(1, -1).astype(F32)
    for l in range(depth):
        w_l = (w_in[l] * col_scale[None, :]).astype(BF16)
        q, k, v, y_ret, y_conv = _mix_proj(
            x, row(mix_norm_g[l]), w_l, ones_bd, cos_t, sin_t,
            row(jnp.tile(sb_q_norm_g[l], sbw // HEAD_DIM)), row(jnp.tile(sb_k_norm_g[l], sbw // HEAD_DIM)),
            row(conv_pw_b[l]), shifts, conv_dw_w[l].astype(F32), row(conv_dw_b[l]), row(conv_ln_g[l]),
            row(conv_ln_b[l]), qdec, kdec, dmat, cdec, bd_mask, row(ret_norm_g[l]), widths)
        y_sb = _sb_attention(q, k, v, sfx)
        x = _out_mlp(x, y_sb, y_ret, y_conv, w_out[l].astype(BF16), row(mlp_norm_g[l]),
                     w_ff1[l].astype(BF16), w_ff2[l].astype(BF16))
    return x
```

```python
import functools
import math

import jax
import jax.numpy as jnp
import numpy as np
from jax import lax
from jax.experimental import pallas as pl
from jax.experimental.pallas import tpu as pltpu

F32 = jnp.float32
BF16 = jnp.bfloat16

HEAD_DIM = 64
CONV_KERNEL = 31
EPS = 1e-6
ROPE_BASE = 10000.0
LOG2E = 1.4426950408889634
LN2 = 0.6931471805599453

V7X_LANES = 128
V7X_MXU_DIM = 256
V7X_VMEM_BYTES = 64 * 1024 * 1024

SB_DEAD_LOG2 = -150.0
SB_MASKED_SCORE = -1e30

TOKEN_TILE = 512
SB_BLOCK = 128
SB_TILE = 512
SB_LOOKBACK = 2
SB_FAR_ROWS = 64
RET_CHUNK = 256
CONV_HALO = 32
CONV_BLOCK = 128
CONV_WINDOW = 256
CONV_SHIFTS = 8
FF_CHUNK = 1024


def _vmem_limit(nbytes):
    return int(min(nbytes, V7X_VMEM_BYTES - 4 * 1024 * 1024))


def _const_spec(shape):
    nd = len(shape)
    return pl.BlockSpec(shape, lambda *_: (0,) * nd, pipeline_mode=pl.Buffered(1))


def _layer_spec(shape, layer):
    nd = len(shape)
    return pl.BlockSpec((None,) + tuple(shape[1:]), lambda *_: (layer,) + (0,) * (nd - 1),
                        pipeline_mode=pl.Buffered(1))


def _mix_kernel(x_ref, g_ref, w_ref, ones_ref, cos_ref, sin_ref, gq_ref, gk_ref, pb_ref,
                shift_ref, dw_ref, db_ref, lg_ref, lb_ref,
                qdec_ref, kdec_ref, dmat_ref, cdec_ref, bd_ref, rg_ref,
                q_ref, k_ref, v_ref, yret_ref, yconv_ref,
                pad_ref, state_ref, rq_s, rk_s, rv_s, sg_s, *, widths):
    sbw, retw, cw = widths
    tm = x_ref.shape[1]
    halo = CONV_HALO

    @pl.when(pl.program_id(1) == 0)
    def _():
        state_ref[...] = jnp.zeros_like(state_ref)
        pad_ref[0:halo, :] = jnp.zeros((halo, cw), pad_ref.dtype)
        pad_ref[halo + tm:, :] = jnp.zeros((pad_ref.shape[0] - halo - tm, cw), pad_ref.dtype)

    x = x_ref[0]
    ms = jnp.mean(x * x, axis=-1, keepdims=True)
    h = (x * lax.rsqrt(ms + EPS) * g_ref[...]).astype(BF16)

    def proj(c0, width):
        return jnp.dot(h, w_ref[:, c0:c0 + width], preferred_element_type=F32)

    def head_norm(u, gain):
        ss = jnp.dot((u * u).astype(BF16), ones_ref[...], preferred_element_type=F32)
        return u * lax.rsqrt(ss * (1.0 / HEAD_DIM) + EPS) * gain

    def rotary(u):
        n = u.shape[1]
        nxt = pltpu.roll(u, n - 1, axis=1)
        prv = pltpu.roll(u, 1, axis=1)
        lane = lax.broadcasted_iota(jnp.int32, u.shape, 1)
        partner = jnp.where((lane & 1) == 0, nxt, prv)
        return u * cos_ref[...] + partner * sin_ref[...]

    c_q, c_v = 0, 2 * sbw
    c_rq, c_rk, c_rv, c_rg = (3 * sbw + i * retw for i in range(4))
    c_a = 3 * sbw + 4 * retw

    def project_conv():
        ag = proj(c_a, 2 * cw) + pb_ref[...]
        pad_ref[halo:halo + tm, :] = (ag[:, :cw] * jax.nn.sigmoid(ag[:, cw:])).astype(pad_ref.dtype)

    def project_ret_gate():
        gate = proj(c_rg, retw)
        sg_s[...] = (gate * jax.nn.sigmoid(gate)).astype(BF16)

    def project_ret_q():
        rq_s[...] = rotary(proj(c_rq, retw)).astype(BF16)

    def project_ret_k():
        rk_s[...] = rotary(proj(c_rk, retw)).astype(BF16)

    def project_ret_v():
        rv_s[...] = proj(c_rv, retw).astype(BF16)

    def project_sb_qk():
        qk = proj(c_q, 2 * sbw)
        q_ref[0] = (head_norm(qk[:, :sbw], gq_ref[...]) * (LOG2E * HEAD_DIM ** -0.5)).astype(BF16)
        k_ref[0] = head_norm(qk[:, sbw:], gk_ref[...]).astype(BF16)

    def project_sb_v():
        v_ref[0] = proj(c_v, sbw).astype(BF16)

    n_shift = CONV_SHIFTS
    first = halo - (CONV_KERNEL - 1)

    def conv_block(r0):
        window = pad_ref[r0:r0 + CONV_WINDOW, :]
        acc = jnp.zeros((CONV_BLOCK, cw), F32) + db_ref[...]
        for shift in range(n_shift):
            if shift == 0:
                moved = window[:CONV_BLOCK + halo].astype(F32)
            else:
                moved = jnp.dot(shift_ref[shift - 1], window, preferred_element_type=F32)
            for tap in range(CONV_KERNEL):
                if (first + tap) % n_shift == shift:
                    a = (first + tap) // n_shift * n_shift
                    acc = acc + moved[a:a + CONV_BLOCK, :] * dw_ref[tap:tap + 1, :]
        mu = jnp.mean(acc, axis=-1, keepdims=True)
        xc = acc - mu
        var = jnp.mean(xc * xc, axis=-1, keepdims=True)
        yn = xc * lax.rsqrt(var + EPS) * lg_ref[...] + lb_ref[...]
        yconv_ref[0, r0:r0 + CONV_BLOCK, :] = (yn * jax.nn.sigmoid(yn)).astype(yconv_ref.dtype)

    ch = qdec_ref.shape[0]
    group = bd_ref.shape[0]
    heads_per_group = group // HEAD_DIM
    lane = lax.broadcasted_iota(jnp.int32, (ch, group), 1)

    def retention_scores(c0, gi):
        cols = slice(gi * group, (gi + 1) * group)
        q = rq_s[c0:c0 + ch, cols]
        k = rk_s[c0:c0 + ch, cols]
        scores = []
        for hh in range(heads_per_group):
            in_head = (lane >= hh * HEAD_DIM) & (lane < (hh + 1) * HEAD_DIM)
            qm = jnp.where(in_head, q, jnp.zeros_like(q))
            sc = lax.dot_general(qm, k, (((1,), (1,)), ((), ())), preferred_element_type=F32)
            scores.append((sc * dmat_ref[gi * heads_per_group + hh]).astype(BF16))
        return scores

    def retention_chunk(c0, gi, scores):
        cols = slice(gi * group, (gi + 1) * group)
        q = rq_s[c0:c0 + ch, cols]
        k = rk_s[c0:c0 + ch, cols]
        v = rv_s[c0:c0 + ch, cols]
        state = state_ref[gi]
        qd = (q.astype(F32) * qdec_ref[:, cols]).astype(BF16)
        out = jnp.dot(qd, state.astype(BF16), preferred_element_type=F32)
        for hh in range(heads_per_group):
            in_head = (lane >= hh * HEAD_DIM) & (lane < (hh + 1) * HEAD_DIM)
            oh = jnp.dot(scores[hh], v, preferred_element_type=F32)
            out = out + jnp.where(in_head, oh, 0.0)
        kd = (k.astype(F32) * kdec_ref[:, cols]).astype(BF16)
        kv = lax.dot_general(kd, v, (((0,), (0,)), ((), ())), preferred_element_type=F32)
        state_ref[gi] = state * cdec_ref[:, cols] + kv * bd_ref[...]
        ss = jnp.dot((out * out).astype(BF16), ones_ref[...], preferred_element_type=F32)
        yn = out * lax.rsqrt(ss * (1.0 / HEAD_DIM) + EPS) * rg_ref[:, cols]
        yret_ref[0, c0:c0 + ch, cols] = (yn * sg_s[c0:c0 + ch, cols].astype(F32)).astype(yret_ref.dtype)

    conv_blocks = list(range(0, tm, CONV_BLOCK))
    ret_chunks = [(c0, gi) for c0 in range(0, tm, ch) for gi in range(retw // group)]
    project_conv()
    fillers = [project_ret_gate, project_ret_q, project_ret_k, project_ret_v]
    for i, r0 in enumerate(conv_blocks):
        conv_block(r0)
        if i < len(fillers):
            fillers[i]()
    for f in fillers[len(conv_blocks):]:
        f()
    pad_ref[0:halo, :] = pad_ref[tm:tm + halo, :]
    fillers = [project_sb_qk, project_sb_v]
    pending = {}
    for i in range(len(ret_chunks) + 1):
        if i < len(ret_chunks):
            pending[i] = retention_scores(*ret_chunks[i])
        if i < len(fillers):
            fillers[i]()
        if i >= 1:
            retention_chunk(*ret_chunks[i - 1], pending.pop(i - 1))
    for f in fillers[len(ret_chunks) + 1:]:
        f()


def _mix_proj(layer, x, g, w, ones_bd, cos_t, sin_t, gq, gk, pb, shifts, dw_w, dw_b, ln_g, ln_b,
              qdec, kdec, dmat, cdec, bd, ret_g, widths):
    B, T, D = x.shape
    sbw, retw, cw = widths
    tm = min(TOKEN_TILE, T)
    assert tm % qdec.shape[0] == 0 and tm % CONV_BLOCK == 0
    grid = (B, T // tm)
    tok = lambda width: pl.BlockSpec((1, tm, width), lambda b, t: (b, t, 0))
    tab = pl.BlockSpec((tm, retw), lambda b, t: (t, 0))
    per_layer = lambda a: _layer_spec(a.shape, layer)
    shared = lambda a: _const_spec(a.shape)
    out_widths = (sbw, sbw, sbw, retw, cw)
    vmem = (2 * tm * D * 4 + w[0].size * 2 + 4 * tm * retw * 4 + dmat.size * 4 + 4 * tm * retw * 2
            + 2 * sum(out_widths) * tm * 2 + 24 * 1024 * 1024)
    return pl.pallas_call(
        functools.partial(_mix_kernel, widths=widths),
        grid=grid,
        in_specs=[tok(D), per_layer(g), per_layer(w), shared(ones_bd), tab, tab,
                  per_layer(gq), per_layer(gk), per_layer(pb), shared(shifts), per_layer(dw_w), per_layer(dw_b),
                  per_layer(ln_g), per_layer(ln_b), shared(qdec), shared(kdec), shared(dmat), shared(cdec),
                  shared(bd), per_layer(ret_g)],
        out_specs=[tok(wd) for wd in out_widths],
        out_shape=[jax.ShapeDtypeStruct((B, T, wd), BF16) for wd in out_widths],
        scratch_shapes=[pltpu.VMEM((tm + CONV_WINDOW - CONV_BLOCK, cw), BF16),
                        pltpu.VMEM((retw // bd.shape[0], bd.shape[0], bd.shape[0]), F32)]
                       + [pltpu.VMEM((tm, retw), BF16)] * 4,
        compiler_params=pltpu.CompilerParams(
            dimension_semantics=("arbitrary", "arbitrary"), vmem_limit_bytes=_vmem_limit(vmem)),
        name="mix_proj",
    )(x, g, w, ones_bd, cos_t, sin_t, gq, gk, pb, shifts, dw_w, dw_b, ln_g, ln_b,
      qdec, kdec, dmat, cdec, bd, ret_g)


def _sb_kernel(q_ref, k_ref, v_ref, m_ref, o_ref, vm_ref, acc_ref, c_ref, *, n_heads):
    blk = SB_BLOCK
    tile = q_ref.shape[1]
    seq = k_ref.shape[1]
    t0 = pl.program_id(1) * tile
    row = lax.broadcasted_iota(jnp.int32, (blk, blk), 0)
    col = lax.broadcasted_iota(jnp.int32, (blk, blk), 1)
    causal = col < row

    def head_lanes(shape, h):
        lane = lax.broadcasted_iota(jnp.int32, shape, 1)
        return (lane >= h * HEAD_DIM) & (lane < (h + 1) * HEAD_DIM)

    @pl.when(pl.program_id(1) == 0)
    def _():
        rows = min(SB_TILE, seq)

        def fill(i, _):
            r0 = pl.multiple_of(i * rows, rows)
            vb = v_ref[0, pl.ds(r0, rows), :]
            for h in range(n_heads):
                vm_ref[h, pl.ds(r0, rows), :] = jnp.where(head_lanes(vb.shape, h), vb, jnp.zeros_like(vb))
            return 0

        lax.fori_loop(0, seq // rows, fill, 0)

    all_heads = tuple(range(n_heads))

    def masked_q(off, nrows):
        q = q_ref[0, pl.ds(off, nrows), :]
        return jnp.concatenate(
            [jnp.where(head_lanes(q.shape, h), q, jnp.zeros_like(q)) for h in all_heads], axis=0)


    def scores(off, nrows, kstart, nblk, diag_last):
        kwin = k_ref[0, pl.ds(kstart, nblk * blk), :]
        z_all = lax.dot_general(masked_q(off, nrows), kwin, (((1,), (1,)), ((), ())),
                                preferred_element_type=F32)
        log_beta, split = {}, []
        for h in all_heads:
            for j in range(nblk):
                z = z_all[h * nrows:(h + 1) * nrows, j * blk:(j + 1) * blk]
                if diag_last and j == nblk - 1:
                    z = jnp.where(causal, z, SB_MASKED_SCORE)
                sp = jnp.maximum(z, 0.0) + jnp.log(1.0 + jnp.exp2(-jnp.abs(z))) * (1.0 / LN2)
                log_beta[h, j] = z - sp
                hi = sp.astype(BF16)
                lo = (sp - hi.astype(F32)).astype(BF16)
                split.append(jnp.concatenate([hi, lo], axis=1))
        return jnp.concatenate(split, axis=0), log_beta

    def weights(split, log_beta, nrows, nblk, carries):
        r = jnp.dot(split, m_ref[...], preferred_element_type=F32)
        ws, out_carries = [], []
        for h in all_heads:
            carry = carries[h]
            w = [None] * nblk
            for j in reversed(range(nblk)):
                rr = r[(h * nblk + j) * nrows:(h * nblk + j + 1) * nrows, :]
                arg = rr[:, :blk] + log_beta[h, j]
                if carry is not None:
                    arg = arg + carry
                w[j] = jnp.exp2(arg).astype(BF16)
                carry = rr[:, blk:] if carry is None else carry + rr[:, blk:]
            out_carries.append(carry)
            ws.extend(w)
        return jnp.concatenate(ws, axis=1), out_carries

    def values(ws, kstart, nblk):
        vs = [vm_ref[h, pl.ds(kstart, nblk * blk), :] for h in all_heads]
        return jnp.dot(ws, jnp.concatenate(vs, axis=0), preferred_element_type=F32)

    far = SB_FAR_ROWS

    def top_rows(carries):
        parts = [c[r0:r0 + far] for c in carries for r0 in range(0, c.shape[0], far)]
        top = parts[0]
        for p in parts[1:]:
            top = jnp.maximum(top, p)
        return top

    def straight(lookbacks, more_keys):
        def run():
            items = []
            for s, lookback in enumerate(lookbacks):
                split_far = lookback == SB_LOOKBACK
                items.append(("main", s, lookback if split_far else lookback + 1))
                if split_far:
                    items.append(("far", s, 1))
            stage1, stage2, main_pv, main_carries, tops = {}, {}, {}, {}, []
            for step in range(len(items) + 2):
                if step < len(items):
                    kind, s, nblk = items[step]
                    if kind == "main":
                        kstart = pl.multiple_of(t0 + (s - nblk + 1) * blk, blk)
                        stage1[step] = scores(s * blk, blk, kstart, nblk, True)
                    else:
                        kstart = pl.multiple_of(t0 + (s - lookbacks[s]) * blk, blk)
                        stage1[step] = scores(s * blk, far, kstart, 1, False)
                if 0 <= step - 1 < len(items):
                    kind, s, nblk = items[step - 1]
                    if kind == "main":
                        stage2[step - 1] = weights(*stage1.pop(step - 1), blk, nblk, [None] * n_heads)
                        main_carries[s] = stage2[step - 1][1]
                    else:
                        stage2[step - 1] = weights(*stage1.pop(step - 1), far, 1,
                                                   [c[:far] for c in main_carries[s]])
                if 0 <= step - 2 < len(items):
                    kind, s, nblk = items[step - 2]
                    ws, carries = stage2.pop(step - 2)
                    split_far = lookbacks[s] == SB_LOOKBACK
                    if kind == "main":
                        pv = values(ws, pl.multiple_of(t0 + (s - nblk + 1) * blk, blk), nblk)
                        if split_far:
                            main_pv[s] = pv
                            tops.append(top_rows([c[far:] for c in carries]))
                        else:
                            o_ref[0, s * blk:(s + 1) * blk, :] = pv.astype(o_ref.dtype)
                            if more_keys[s]:
                                tops.append(top_rows(carries))
                    else:
                        pv_far = values(ws, pl.multiple_of(t0 + (s - lookbacks[s]) * blk, blk), 1)
                        pv = main_pv.pop(s)
                        o_ref[0, s * blk:s * blk + far, :] = (pv[:far] + pv_far).astype(o_ref.dtype)
                        o_ref[0, s * blk + far:(s + 1) * blk, :] = pv[far:].astype(o_ref.dtype)
                        if more_keys[s]:
                            tops.append(top_rows(carries))
            if not tops:
                return jnp.bool_(False)
            return jnp.max(top_rows(tops)) > SB_DEAD_LOG2
        return run

    n_sub = tile // blk
    first_lookbacks = [min(s, SB_LOOKBACK) for s in range(n_sub)]
    alive = lax.cond(pl.program_id(1) == 0,
                     straight(first_lookbacks, [s > lb for s, lb in enumerate(first_lookbacks)]),
                     straight([SB_LOOKBACK] * n_sub, [True] * n_sub))

    @pl.when(alive)
    def _():
        def sub_block(s, _):
            off = pl.multiple_of(s * blk, blk)
            q0 = pl.multiple_of(t0 + off, blk)
            ws, carries = weights(*scores(off, blk, q0, 1, True), blk, 1, [None] * n_heads)
            acc_ref[...] = values(ws, q0, 1)
            for h in all_heads:
                c_ref[h] = carries[h]

            def cond(state):
                i, live = state
                return jnp.logical_and(i * blk <= q0, live)

            def body(state):
                i, _ = state
                kstart = pl.multiple_of(q0 - i * blk, blk)
                ws, carries = weights(*scores(off, blk, kstart, 1, False), blk, 1, [c_ref[h] for h in all_heads])
                acc_ref[...] += values(ws, kstart, 1)
                for h in all_heads:
                    c_ref[h] = carries[h]
                return i + 1, jnp.max(top_rows(carries)) > SB_DEAD_LOG2

            lax.while_loop(cond, body, (jnp.int32(1), jnp.max(top_rows(carries)) > SB_DEAD_LOG2))
            o_ref[0, pl.ds(off, blk), :] = acc_ref[...].astype(o_ref.dtype)
            return 0

        lax.fori_loop(0, n_sub, sub_block, 0)


def _sb_attention(q, k, v, m):
    B, T, W = q.shape
    blk = SB_BLOCK
    tile = min(SB_TILE, T)
    n_heads = W // HEAD_DIM
    grid = (B, T // tile)
    seq = pl.BlockSpec((1, T, W), lambda b, i: (b, 0, 0))
    qblk = pl.BlockSpec((1, tile, W), lambda b, i: (b, i, 0))
    vmem = (4 + n_heads) * T * W * 2 + 24 * 1024 * 1024
    return pl.pallas_call(
        functools.partial(_sb_kernel, n_heads=n_heads),
        grid=grid,
        in_specs=[qblk, seq, seq, _const_spec(m.shape)],
        out_specs=qblk,
        out_shape=jax.ShapeDtypeStruct((B, T, W), BF16),
        scratch_shapes=[pltpu.VMEM((n_heads, T, W), BF16), pltpu.VMEM((blk, W), F32),
                        pltpu.VMEM((n_heads, blk, blk), F32)],
        compiler_params=pltpu.CompilerParams(
            dimension_semantics=("arbitrary", "arbitrary"), vmem_limit_bytes=_vmem_limit(vmem)),
        name="sb_attn",
    )(q, k, v, m)


def _out_mlp_kernel(x_ref, ysb_ref, yret_ref, yconv_ref, wo_ref, g_ref, w1_ref, w2_ref, o_ref):
    mixed = jnp.concatenate([ysb_ref[0], yret_ref[0], yconv_ref[0]], axis=1)
    x1 = x_ref[0] + jnp.dot(mixed, wo_ref[...], preferred_element_type=F32)
    ms = jnp.mean(x1 * x1, axis=-1, keepdims=True)
    h = (x1 * lax.rsqrt(ms + EPS) * g_ref[...]).astype(BF16)
    acc = x1
    for c0 in range(0, w1_ref.shape[1], FF_CHUNK):
        f = jnp.maximum(jnp.dot(h, w1_ref[:, c0:c0 + FF_CHUNK], preferred_element_type=F32), 0.0)
        acc = acc + jnp.dot((f * f).astype(BF16), w2_ref[c0:c0 + FF_CHUNK, :], preferred_element_type=F32)
    o_ref[0] = acc


def _out_mlp(layer, x, ysb, yret, yconv, wo, g, w1, w2):
    B, T, D = x.shape
    tm = min(TOKEN_TILE, T)
    grid = (B, T // tm)
    tok = lambda width: pl.BlockSpec((1, tm, width), lambda b, t: (b, t, 0))
    dff = w1.shape[2]
    vmem = (D * D + 2 * D * dff) * 2 + 4 * tm * D * 4 + 4 * tm * D * 2 + tm * FF_CHUNK * 12 + 12 * 1024 * 1024
    return pl.pallas_call(
        _out_mlp_kernel,
        grid=grid,
        in_specs=[tok(D), tok(ysb.shape[2]), tok(yret.shape[2]), tok(yconv.shape[2])]
                 + [_layer_spec(a.shape, layer) for a in (wo, g, w1, w2)],
        out_specs=tok(D),
        out_shape=jax.ShapeDtypeStruct((B, T, D), F32),
        compiler_params=pltpu.CompilerParams(
            dimension_semantics=("arbitrary", "arbitrary"), vmem_limit_bytes=_vmem_limit(vmem)),
        name="out_mlp",
    )(x, ysb, yret, yconv, wo, g, w1, w2)


def _block_diag_ones(n, block):
    i = np.arange(n)
    return (i[:, None] // block == i[None, :] // block).astype(np.float32)


def _shift_matrices():
    r = np.arange(CONV_BLOCK + CONV_HALO)[None, :, None]
    j = np.arange(CONV_WINDOW)[None, None, :]
    b = np.arange(1, CONV_SHIFTS)[:, None, None]
    return (j == r + b).astype(np.float32)


def _suffix_sum_matrix(blk):
    j = np.arange(2 * blk)[:, None] % blk
    s = np.arange(2 * blk)[None, :]
    return -((s >= blk) | (j > s)).astype(np.float32)


def _rotary_tables(T, n_heads):
    inv = 1.0 / (ROPE_BASE ** jnp.linspace(0.0, 1.0, HEAD_DIM // 2, dtype=F32))
    lane = np.arange(n_heads * HEAD_DIM)
    inv_lane = inv[(lane % HEAD_DIM) // 2]
    ang = jnp.arange(T).astype(F32)[:, None] * inv_lane[None, :]
    sign = jnp.asarray(np.where(lane % 2 == 0, -1.0, 1.0), F32)
    return jnp.cos(ang), jnp.sin(ang) * sign[None, :]


def _retention_tables(n_heads, ch):
    log_g = jnp.log(1.0 - jnp.exp2(-5.0 - jnp.arange(n_heads, dtype=F32)))
    j = jnp.arange(ch, dtype=F32)
    per_lane = lambda t: jnp.repeat(t, HEAD_DIM, axis=-1)
    qdec = per_lane(jnp.exp(log_g[None, :] * (j + 1.0)[:, None]))
    kdec = per_lane(jnp.exp(log_g[None, :] * (ch - 1.0 - j)[:, None]))
    cdec = per_lane(jnp.exp(log_g * ch)[None, :])
    rel = j[:, None] - j[None, :]
    dmat = jnp.where(rel >= 0, jnp.exp(log_g[:, None, None] * jnp.maximum(rel, 0.0)), 0.0)
    return qdec, kdec, dmat, cdec


def kernel(x, mix_norm_g, w_in, sb_q_norm_g, sb_k_norm_g, ret_norm_g, conv_pw_b, conv_dw_w, conv_dw_b,
           conv_ln_g, conv_ln_b, w_out, mlp_norm_g, w_ff1, w_ff2):
    B, T, D = x.shape
    depth = w_in.shape[0]
    sbw = (D // 256) * HEAD_DIM
    retw = (D // 128) * HEAD_DIM
    cw = D - sbw - retw
    assert T % TOKEN_TILE == 0 or T < TOKEN_TILE
    assert T % RET_CHUNK == 0 and T % SB_BLOCK == 0
    widths = (sbw, retw, cw)
    n_ret_heads = retw // HEAD_DIM

    ones_bd = jnp.asarray(_block_diag_ones(V7X_MXU_DIM, HEAD_DIM), BF16)
    bd_mask = jnp.asarray(_block_diag_ones(V7X_MXU_DIM, HEAD_DIM), F32)
    sfx = jnp.asarray(_suffix_sum_matrix(SB_BLOCK), BF16)
    shifts = jnp.asarray(_shift_matrices(), BF16)
    cos_t, sin_t = _rotary_tables(T, n_ret_heads)
    qdec, kdec, dmat, cdec = _retention_tables(n_ret_heads, RET_CHUNK)
    rk0 = 3 * sbw + retw
    col_scale = jnp.ones((w_in.shape[2],), F32).at[rk0:rk0 + retw].set(HEAD_DIM ** -0.5)

    rows = lambda a: a.reshape(depth, 1, -1).astype(F32)
    w_in_b = (w_in * col_scale[None, None, :]).astype(BF16)
    w_out_b, w_ff1_b, w_ff2_b = (w.astype(BF16) for w in (w_out, w_ff1, w_ff2))
    mix_g, mlp_g, ret_g, pw_b = rows(mix_norm_g), rows(mlp_norm_g), rows(ret_norm_g), rows(conv_pw_b)
    gq, gk = (rows(jnp.tile(g, (1, sbw // HEAD_DIM))) for g in (sb_q_norm_g, sb_k_norm_g))
    dw_w, dw_b, ln_g, ln_b = conv_dw_w.astype(F32), rows(conv_dw_b), rows(conv_ln_g), rows(conv_ln_b)
    for l in range(depth):
        q, k, v, y_ret, y_conv = _mix_proj(
            l, x, mix_g, w_in_b, ones_bd, cos_t, sin_t, gq, gk, pw_b, shifts, dw_w, dw_b, ln_g, ln_b,
            qdec, kdec, dmat, cdec, bd_mask, ret_g, widths)
        y_sb = _sb_attention(q, k, v, sfx)
        x = _out_mlp(l, x, y_sb, y_ret, y_conv, w_out_b, mlp_g, w_ff1_b, w_ff2_b)
    return x
```

```python
import functools
import math

import jax
import jax.numpy as jnp
import numpy as np
from jax import lax
from jax.experimental import pallas as pl
from jax.experimental.pallas import tpu as pltpu

F32 = jnp.float32
BF16 = jnp.bfloat16

HEAD_DIM = 64
CONV_KERNEL = 31
EPS = 1e-6
ROPE_BASE = 10000.0
LOG2E = 1.4426950408889634
LN2 = 0.6931471805599453

V7X_LANES = 128
V7X_MXU_DIM = 256
V7X_VMEM_BYTES = 64 * 1024 * 1024

SB_DEAD_LOG2 = -150.0
SB_MASKED_SCORE = -1e30

TOKEN_TILE = 512
MLP_TILE = 1024
SB_BLOCK = 128
SB_TILE = 512
SB_LOOKBACK = 2
RET_CHUNK = 256
CONV_HALO = 32
CONV_BLOCK = 128
CONV_WINDOW = 256
CONV_SHIFTS = 8
FF_CHUNK = 1024


def _vmem_limit(nbytes):
    return int(min(nbytes, V7X_VMEM_BYTES - 4 * 1024 * 1024))


def _const_spec(shape):
    nd = len(shape)
    return pl.BlockSpec(shape, lambda *_: (0,) * nd, pipeline_mode=pl.Buffered(1))


def _layer_spec(shape, layer):
    nd = len(shape)
    return pl.BlockSpec((None,) + tuple(shape[1:]), lambda *_: (layer,) + (0,) * (nd - 1),
                        pipeline_mode=pl.Buffered(1))


def _mix_kernel(x_ref, g_ref, w_ref, ones_ref, cos_ref, sin_ref, gq_ref, gk_ref, pb_ref,
                shift_ref, dw_ref, db_ref, lg_ref, lb_ref,
                qdec_ref, kdec_ref, dmat_ref, cdec_ref, bd_ref, rg_ref,
                q_ref, k_ref, v_ref, yret_ref, yconv_ref,
                pad_ref, state_ref, rq_s, rk_s, rv_s, sg_s, *, widths):
    sbw, retw, cw = widths
    tm = x_ref.shape[1]
    halo = CONV_HALO

    @pl.when(pl.program_id(1) == 0)
    def _():
        state_ref[...] = jnp.zeros_like(state_ref)
        pad_ref[0:halo, :] = jnp.zeros((halo, cw), pad_ref.dtype)
        pad_ref[halo + tm:, :] = jnp.zeros((pad_ref.shape[0] - halo - tm, cw), pad_ref.dtype)

    x = x_ref[0]
    ms = jnp.mean(x * x, axis=-1, keepdims=True)
    h = (x * lax.rsqrt(ms + EPS) * g_ref[...]).astype(BF16)

    def proj(c0, width):
        return jnp.dot(h, w_ref[:, c0:c0 + width], preferred_element_type=F32)

    def head_norm(u, gain):
        ss = jnp.dot((u * u).astype(BF16), ones_ref[...], preferred_element_type=F32)
        return u * lax.rsqrt(ss * (1.0 / HEAD_DIM) + EPS) * gain

    def rotary(u):
        n = cos_ref.shape[1]
        cos, sin = cos_ref[...], sin_ref[...]
        even = (lax.broadcasted_iota(jnp.int32, cos.shape, 1) & 1) == 0
        out = []
        for c0 in range(0, u.shape[1], n):
            uc = u[:, c0:c0 + n]
            nxt = pltpu.roll(uc, n - 1, axis=1)
            prv = pltpu.roll(uc, 1, axis=1)
            out.append(uc * cos + jnp.where(even, nxt, prv) * sin)
        return jnp.concatenate(out, axis=1)

    c_q, c_v = 0, 2 * sbw
    c_rq, c_rk, c_rv, c_rg = (3 * sbw + i * retw for i in range(4))
    c_a = 3 * sbw + 4 * retw

    def project_conv():
        ag = proj(c_a, 2 * cw) + pb_ref[...]
        pad_ref[halo:halo + tm, :] = (ag[:, :cw] * jax.nn.sigmoid(ag[:, cw:])).astype(pad_ref.dtype)

    def project_ret_gate():
        gate = proj(c_rg, retw)
        sg_s[...] = (gate * jax.nn.sigmoid(gate)).astype(BF16)

    def project_ret_q():
        rq_s[...] = rotary(proj(c_rq, retw)).astype(BF16)

    def project_ret_k():
        rk_s[...] = rotary(proj(c_rk, retw)).astype(BF16)

    def project_ret_v():
        rv_s[...] = proj(c_rv, retw).astype(BF16)

    def project_sb_qk():
        qk = proj(c_q, 2 * sbw)
        q_ref[0] = (head_norm(qk[:, :sbw], gq_ref[...]) * (LOG2E * HEAD_DIM ** -0.5)).astype(BF16)
        k_ref[0] = head_norm(qk[:, sbw:], gk_ref[...]).astype(BF16)

    def project_sb_v():
        v_ref[0] = proj(c_v, sbw).astype(BF16)

    n_shift = CONV_SHIFTS
    first = halo - (CONV_KERNEL - 1)

    def conv_block(r0):
        window = pad_ref[r0:r0 + CONV_WINDOW, :]
        acc = jnp.zeros((CONV_BLOCK, cw), F32) + db_ref[...]
        for shift in range(n_shift):
            if shift == 0:
                moved = window[:CONV_BLOCK + halo].astype(F32)
            else:
                moved = jnp.dot(shift_ref[shift - 1], window, preferred_element_type=F32)
            for tap in range(CONV_KERNEL):
                if (first + tap) % n_shift == shift:
                    a = (first + tap) // n_shift * n_shift
                    acc = acc + moved[a:a + CONV_BLOCK, :] * dw_ref[tap:tap + 1, :]
        mu = jnp.mean(acc, axis=-1, keepdims=True)
        xc = acc - mu
        var = jnp.mean(xc * xc, axis=-1, keepdims=True)
        yn = xc * lax.rsqrt(var + EPS) * lg_ref[...] + lb_ref[...]
        yconv_ref[0, r0:r0 + CONV_BLOCK, :] = (yn * jax.nn.sigmoid(yn)).astype(yconv_ref.dtype)

    ch = qdec_ref.shape[0]
    group = bd_ref.shape[0]
    heads_per_group = group // HEAD_DIM
    lane = lax.broadcasted_iota(jnp.int32, (ch, group), 1)

    def retention_scores(c0, gi):
        cols = slice(gi * group, (gi + 1) * group)
        q = rq_s[c0:c0 + ch, cols]
        k = rk_s[c0:c0 + ch, cols]
        scores = []
        for hh in range(heads_per_group):
            in_head = (lane >= hh * HEAD_DIM) & (lane < (hh + 1) * HEAD_DIM)
            qm = jnp.where(in_head, q, jnp.zeros_like(q))
            sc = lax.dot_general(qm, k, (((1,), (1,)), ((), ())), preferred_element_type=F32)
            scores.append((sc * dmat_ref[gi * heads_per_group + hh]).astype(BF16))
        return scores

    def retention_chunk(c0, gi, scores):
        cols = slice(gi * group, (gi + 1) * group)
        q = rq_s[c0:c0 + ch, cols]
        k = rk_s[c0:c0 + ch, cols]
        v = rv_s[c0:c0 + ch, cols]
        state = state_ref[gi]
        qd = (q.astype(F32) * qdec_ref[:, cols]).astype(BF16)
        out = jnp.dot(qd, state.astype(BF16), preferred_element_type=F32)
        for hh in range(heads_per_group):
            in_head = (lane >= hh * HEAD_DIM) & (lane < (hh + 1) * HEAD_DIM)
            oh = jnp.dot(scores[hh], v, preferred_element_type=F32)
            out = out + jnp.where(in_head, oh, 0.0)
        kd = (k.astype(F32) * kdec_ref[:, cols]).astype(BF16)
        kv = lax.dot_general(kd, v, (((0,), (0,)), ((), ())), preferred_element_type=F32)
        state_ref[gi] = state * cdec_ref[:, cols] + kv * bd_ref[...]
        ss = jnp.dot((out * out).astype(BF16), ones_ref[...], preferred_element_type=F32)
        yn = out * lax.rsqrt(ss * (1.0 / HEAD_DIM) + EPS) * rg_ref[:, cols]
        yret_ref[0, c0:c0 + ch, cols] = (yn * sg_s[c0:c0 + ch, cols].astype(F32)).astype(yret_ref.dtype)

    conv_blocks = list(range(0, tm, CONV_BLOCK))
    ret_chunks = [(c0, gi) for c0 in range(0, tm, ch) for gi in range(retw // group)]
    project_conv()
    fillers = [project_ret_gate, project_ret_q, project_ret_k, project_ret_v]
    for i, r0 in enumerate(conv_blocks):
        conv_block(r0)
        if i < len(fillers):
            fillers[i]()
    for f in fillers[len(conv_blocks):]:
        f()
    pad_ref[0:halo, :] = pad_ref[tm:tm + halo, :]
    fillers = [project_sb_qk, project_sb_v]
    pending = {}
    for i in range(len(ret_chunks) + 1):
        if i < len(ret_chunks):
            pending[i] = retention_scores(*ret_chunks[i])
        if i < len(fillers):
            fillers[i]()
        if i >= 1:
            retention_chunk(*ret_chunks[i - 1], pending.pop(i - 1))
    for f in fillers[len(ret_chunks) + 1:]:
        f()


def _mix_proj(layer, x, g, w, ones_bd, cos_t, sin_t, gq, gk, pb, shifts, dw_w, dw_b, ln_g, ln_b,
              qdec, kdec, dmat, cdec, bd, ret_g, widths):
    B, T, D = x.shape
    sbw, retw, cw = widths
    tm = min(TOKEN_TILE, T)
    assert tm % qdec.shape[0] == 0 and tm % CONV_BLOCK == 0
    grid = (B, T // tm)
    tok = lambda width: pl.BlockSpec((1, tm, width), lambda b, t: (b, t, 0))
    tab = pl.BlockSpec((tm, cos_t.shape[1]), lambda b, t: (t, 0))
    per_layer = lambda a: _layer_spec(a.shape, layer)
    shared = lambda a: _const_spec(a.shape)
    out_widths = (sbw, sbw, sbw, retw, cw)
    vmem = (2 * tm * D * 4 + w[0].size * 2 + 4 * tm * retw * 4 + dmat.size * 4 + 4 * tm * retw * 2
            + 2 * sum(out_widths) * tm * 2 + 24 * 1024 * 1024)
    return pl.pallas_call(
        functools.partial(_mix_kernel, widths=widths),
        grid=grid,
        in_specs=[tok(D), per_layer(g), per_layer(w), shared(ones_bd), tab, tab,
                  per_layer(gq), per_layer(gk), per_layer(pb), shared(shifts), per_layer(dw_w), per_layer(dw_b),
                  per_layer(ln_g), per_layer(ln_b), shared(qdec), shared(kdec), shared(dmat), shared(cdec),
                  shared(bd), per_layer(ret_g)],
        out_specs=[tok(wd) for wd in out_widths],
        out_shape=[jax.ShapeDtypeStruct((B, T, wd), BF16) for wd in out_widths],
        scratch_shapes=[pltpu.VMEM((tm + CONV_WINDOW - CONV_BLOCK, cw), BF16),
                        pltpu.VMEM((retw // bd.shape[0], bd.shape[0], bd.shape[0]), F32)]
                       + [pltpu.VMEM((tm, retw), BF16)] * 4,
        compiler_params=pltpu.CompilerParams(
            dimension_semantics=("arbitrary", "arbitrary"), vmem_limit_bytes=_vmem_limit(vmem)),
        name="mix_proj",
    )(x, g, w, ones_bd, cos_t, sin_t, gq, gk, pb, shifts, dw_w, dw_b, ln_g, ln_b,
      qdec, kdec, dmat, cdec, bd, ret_g)


def _sb_kernel(q_ref, k_ref, v_ref, m_ref, o_ref, vm_ref, acc_ref, c_ref, *, n_heads):
    blk = SB_BLOCK
    tile = q_ref.shape[1]
    seq = k_ref.shape[1]
    t0 = pl.program_id(1) * tile
    row = lax.broadcasted_iota(jnp.int32, (blk, blk), 0)
    col = lax.broadcasted_iota(jnp.int32, (blk, blk), 1)
    causal = col < row

    def head_lanes(shape, h):
        lane = lax.broadcasted_iota(jnp.int32, shape, 1)
        return (lane >= h * HEAD_DIM) & (lane < (h + 1) * HEAD_DIM)

    @pl.when(pl.program_id(1) == 0)
    def _():
        rows = min(SB_TILE, seq)

        def fill(i, _):
            r0 = pl.multiple_of(i * rows, rows)
            vb = v_ref[0, pl.ds(r0, rows), :]
            for h in range(n_heads):
                vm_ref[h, pl.ds(r0, rows), :] = jnp.where(head_lanes(vb.shape, h), vb, jnp.zeros_like(vb))
            return 0

        lax.fori_loop(0, seq // rows, fill, 0)

    all_heads = tuple(range(n_heads))

    def masked_q(off):
        q = q_ref[0, pl.ds(off, blk), :]
        return jnp.concatenate(
            [jnp.where(head_lanes(q.shape, h), q, jnp.zeros_like(q)) for h in all_heads], axis=0)


    def scores(off, kstart, nblk, diag_last):
        kwin = k_ref[0, pl.ds(kstart, nblk * blk), :]
        z_all = lax.dot_general(masked_q(off), kwin, (((1,), (1,)), ((), ())), preferred_element_type=F32)
        log_beta, split = {}, []
        for h in all_heads:
            for j in range(nblk):
                z = z_all[h * blk:(h + 1) * blk, j * blk:(j + 1) * blk]
                if diag_last and j == nblk - 1:
                    z = jnp.where(causal, z, SB_MASKED_SCORE)
                sp = jnp.maximum(z, 0.0) + jnp.log(1.0 + jnp.exp2(-jnp.abs(z))) * (1.0 / LN2)
                log_beta[h, j] = z - sp
                hi = sp.astype(BF16)
                lo = (sp - hi.astype(F32)).astype(BF16)
                split.append(jnp.concatenate([hi, lo], axis=1))
        return jnp.concatenate(split, axis=0), log_beta

    def weights(split, log_beta, nblk, carries):
        r = jnp.dot(split, m_ref[...], preferred_element_type=F32)
        ws, out_carries = [], []
        for h in all_heads:
            carry = carries[h]
            w = [None] * nblk
            for j in reversed(range(nblk)):
                rr = r[(h * nblk + j) * blk:(h * nblk + j + 1) * blk, :]
                arg = rr[:, :blk] + log_beta[h, j]
                if carry is not None:
                    arg = arg + carry
                w[j] = jnp.exp2(arg).astype(BF16)
                carry = rr[:, blk:] if carry is None else carry + rr[:, blk:]
            out_carries.append(carry)
            ws.extend(w)
        return jnp.concatenate(ws, axis=1), out_carries

    def values(ws, kstart, nblk):
        vs = [vm_ref[h, pl.ds(kstart, nblk * blk), :] for h in all_heads]
        return jnp.dot(ws, jnp.concatenate(vs, axis=0), preferred_element_type=F32)

    def top_carry(carries):
        top = carries[0]
        for c in carries[1:]:
            top = jnp.maximum(top, c)
        return top

    def straight(lookbacks):
        def run():
            n = len(lookbacks)
            kstart = [pl.multiple_of(t0 + (s - lookbacks[s]) * blk, blk) for s in range(n)]
            stage1, stage2, tops = {}, {}, []
            for step in range(n + 2):
                if step < n:
                    stage1[step] = scores(step * blk, kstart[step], lookbacks[step] + 1, True)
                s = step - 1
                if 0 <= s < n:
                    stage2[s] = weights(*stage1.pop(s), lookbacks[s] + 1, [None] * n_heads)
                s = step - 2
                if 0 <= s < n:
                    ws, carries = stage2.pop(s)
                    pv = values(ws, kstart[s], lookbacks[s] + 1)
                    acc_ref[s] = pv
                    o_ref[0, s * blk:(s + 1) * blk, :] = pv.astype(o_ref.dtype)
                    for h in all_heads:
                        c_ref[s, h] = carries[h]
                    tops.append(top_carry(carries))
            return jnp.max(top_carry(tops)) > SB_DEAD_LOG2
        return run

    n_sub = tile // blk
    alive = lax.cond(pl.program_id(1) == 0,
                     straight([min(s, SB_LOOKBACK) for s in range(n_sub)]),
                     straight([SB_LOOKBACK] * n_sub))

    @pl.when(alive)
    def _():
        def sub_block(s, _):
            off = pl.multiple_of(s * blk, blk)
            q0 = t0 + off

            def cond(state):
                i, live = state
                return jnp.logical_and(i * blk <= q0, live)

            def body(state):
                i, _ = state
                kstart = pl.multiple_of(q0 - i * blk, blk)
                ws, carries = weights(*scores(off, kstart, 1, False), 1, [c_ref[s, h] for h in all_heads])
                acc_ref[s] += values(ws, kstart, 1)
                for h in all_heads:
                    c_ref[s, h] = carries[h]
                return i + 1, jnp.max(top_carry(carries)) > SB_DEAD_LOG2

            swept = jnp.minimum(q0 // blk, SB_LOOKBACK) + 1
            live = jnp.max(top_carry([c_ref[s, h] for h in all_heads])) > SB_DEAD_LOG2
            lax.while_loop(cond, body, (swept, live))
            o_ref[0, pl.ds(off, blk), :] = acc_ref[s].astype(o_ref.dtype)
            return 0

        lax.fori_loop(0, n_sub, sub_block, 0)


def _sb_attention(q, k, v, m):
    B, T, W = q.shape
    blk = SB_BLOCK
    tile = min(SB_TILE, T)
    n_heads = W // HEAD_DIM
    grid = (B, T // tile)
    seq = pl.BlockSpec((1, T, W), lambda b, i: (b, 0, 0))
    qblk = pl.BlockSpec((1, tile, W), lambda b, i: (b, i, 0))
    vmem = (4 + n_heads) * T * W * 2 + 24 * 1024 * 1024
    return pl.pallas_call(
        functools.partial(_sb_kernel, n_heads=n_heads),
        grid=grid,
        in_specs=[qblk, seq, seq, _const_spec(m.shape)],
        out_specs=qblk,
        out_shape=jax.ShapeDtypeStruct((B, T, W), BF16),
        scratch_shapes=[pltpu.VMEM((n_heads, T, W), BF16), pltpu.VMEM((tile // blk, blk, W), F32),
                        pltpu.VMEM((tile // blk, n_heads, blk, blk), F32)],
        compiler_params=pltpu.CompilerParams(
            dimension_semantics=("arbitrary", "arbitrary"), vmem_limit_bytes=_vmem_limit(vmem)),
        name="sb_attn",
    )(q, k, v, m)


def _out_mlp_kernel(x_ref, ysb_ref, yret_ref, yconv_ref, wo_ref, g_ref, w1_ref, w2_ref, o_ref):
    mixed = jnp.concatenate([ysb_ref[0], yret_ref[0], yconv_ref[0]], axis=1)
    x1 = x_ref[0] + jnp.dot(mixed, wo_ref[...], preferred_element_type=F32)
    ms = jnp.mean(x1 * x1, axis=-1, keepdims=True)
    h = (x1 * lax.rsqrt(ms + EPS) * g_ref[...]).astype(BF16)
    acc = x1
    for c0 in range(0, w1_ref.shape[1], FF_CHUNK):
        f = jnp.maximum(jnp.dot(h, w1_ref[:, c0:c0 + FF_CHUNK], preferred_element_type=F32), 0.0)
        acc = acc + jnp.dot((f * f).astype(BF16), w2_ref[c0:c0 + FF_CHUNK, :], preferred_element_type=F32)
    o_ref[0] = acc


def _out_mlp(layer, x, ysb, yret, yconv, wo, g, w1, w2):
    B, T, D = x.shape
    tm = min(MLP_TILE, T)
    grid = (B, T // tm)
    tok = lambda width: pl.BlockSpec((1, tm, width), lambda b, t: (b, t, 0))
    dff = w1.shape[2]
    vmem = (D * D + 2 * D * dff) * 2 + 4 * tm * D * 4 + 4 * tm * D * 2 + tm * FF_CHUNK * 12 + 12 * 1024 * 1024
    return pl.pallas_call(
        _out_mlp_kernel,
        grid=grid,
        in_specs=[tok(D), tok(ysb.shape[2]), tok(yret.shape[2]), tok(yconv.shape[2])]
                 + [_layer_spec(a.shape, layer) for a in (wo, g, w1, w2)],
        out_specs=tok(D),
        out_shape=jax.ShapeDtypeStruct((B, T, D), F32),
        compiler_params=pltpu.CompilerParams(
            dimension_semantics=("arbitrary", "arbitrary"), vmem_limit_bytes=_vmem_limit(vmem)),
        name="out_mlp",
    )(x, ysb, yret, yconv, wo, g, w1, w2)


def _block_diag_ones(n, block):
    i = np.arange(n)
    return (i[:, None] // block == i[None, :] // block).astype(np.float32)


def _shift_matrices():
    r = np.arange(CONV_BLOCK + CONV_HALO)[None, :, None]
    j = np.arange(CONV_WINDOW)[None, None, :]
    b = np.arange(1, CONV_SHIFTS)[:, None, None]
    return (j == r + b).astype(np.float32)


def _suffix_sum_matrix(blk):
    j = np.arange(2 * blk)[:, None] % blk
    s = np.arange(2 * blk)[None, :]
    return -((s >= blk) | (j > s)).astype(np.float32)


def _rotary_tables(T):
    inv = 1.0 / (ROPE_BASE ** jnp.linspace(0.0, 1.0, HEAD_DIM // 2, dtype=F32))
    ang = jnp.arange(T).astype(F32)[:, None] * inv[None, :]
    cos = jnp.repeat(jnp.cos(ang), 2, axis=1)
    sin = jnp.repeat(jnp.sin(ang), 2, axis=1)
    sign = jnp.where(jnp.arange(HEAD_DIM) % 2 == 0, -1.0, 1.0).astype(F32)
    reps = (1, V7X_LANES // HEAD_DIM)
    return jnp.tile(cos, reps), jnp.tile(sin * sign[None, :], reps)


def _retention_tables(n_heads, ch):
    log_g = jnp.log(1.0 - jnp.exp2(-5.0 - jnp.arange(n_heads, dtype=F32)))
    j = jnp.arange(ch, dtype=F32)
    per_lane = lambda t: jnp.repeat(t, HEAD_DIM, axis=-1)
    qdec = per_lane(jnp.exp(log_g[None, :] * (j + 1.0)[:, None]))
    kdec = per_lane(jnp.exp(log_g[None, :] * (ch - 1.0 - j)[:, None]))
    cdec = per_lane(jnp.exp(log_g * ch)[None, :])
    rel = j[:, None] - j[None, :]
    dmat = jnp.where(rel >= 0, jnp.exp(log_g[:, None, None] * jnp.maximum(rel, 0.0)), 0.0)
    return qdec, kdec, dmat, cdec


def kernel(x, mix_norm_g, w_in, sb_q_norm_g, sb_k_norm_g, ret_norm_g, conv_pw_b, conv_dw_w, conv_dw_b,
           conv_ln_g, conv_ln_b, w_out, mlp_norm_g, w_ff1, w_ff2):
    B, T, D = x.shape
    depth = w_in.shape[0]
    sbw = (D // 256) * HEAD_DIM
    retw = (D // 128) * HEAD_DIM
    cw = D - sbw - retw
    assert all(T % tile == 0 or T < tile for tile in (TOKEN_TILE, MLP_TILE, SB_TILE))
    assert T % RET_CHUNK == 0 and T % SB_BLOCK == 0
    widths = (sbw, retw, cw)
    n_ret_heads = retw // HEAD_DIM

    ones_bd = jnp.asarray(_block_diag_ones(V7X_MXU_DIM, HEAD_DIM), BF16)
    bd_mask = jnp.asarray(_block_diag_ones(V7X_MXU_DIM, HEAD_DIM), F32)
    sfx = jnp.asarray(_suffix_sum_matrix(SB_BLOCK), BF16)
    shifts = jnp.asarray(_shift_matrices(), BF16)
    cos_t, sin_t = _rotary_tables(T)
    qdec, kdec, dmat, cdec = _retention_tables(n_ret_heads, RET_CHUNK)
    rk0 = 3 * sbw + retw
    col_scale = jnp.ones((w_in.shape[2],), F32).at[rk0:rk0 + retw].set(HEAD_DIM ** -0.5)

    rows = lambda a: a.reshape(depth, 1, -1).astype(F32)
    w_in_b = (w_in * col_scale[None, None, :]).astype(BF16)
    w_out_b, w_ff1_b, w_ff2_b = (w.astype(BF16) for w in (w_out, w_ff1, w_ff2))
    mix_g, mlp_g, ret_g, pw_b = rows(mix_norm_g), rows(mlp_norm_g), rows(ret_norm_g), rows(conv_pw_b)
    gq, gk = (rows(jnp.tile(g, (1, sbw // HEAD_DIM))) for g in (sb_q_norm_g, sb_k_norm_g))
    dw_w, dw_b, ln_g, ln_b = conv_dw_w.astype(F32), rows(conv_dw_b), rows(conv_ln_g), rows(conv_ln_b)
    for l in range(depth):
        q, k, v, y_ret, y_conv = _mix_proj(
            l, x, mix_g, w_in_b, ones_bd, cos_t, sin_t, gq, gk, pw_b, shifts, dw_w, dw_b, ln_g, ln_b,
            qdec, kdec, dmat, cdec, bd_mask, ret_g, widths)
        y_sb = _sb_attention(q, k, v, sfx)
        x = _out_mlp(l, x, y_sb, y_ret, y_conv, w_out_b, mlp_g, w_ff1_b, w_ff2_b)
    return x
```

```python
import functools
import math

import jax
import jax.numpy as jnp
import numpy as np
from jax import lax
from jax.experimental import pallas as pl
from jax.experimental.pallas import tpu as pltpu

F32 = jnp.float32
BF16 = jnp.bfloat16

HEAD_DIM = 64
CONV_KERNEL = 31
EPS = 1e-6
ROPE_BASE = 10000.0
LOG2E = 1.4426950408889634
LN2 = 0.6931471805599453

V7X_LANES = 128
V7X_MXU_DIM = 256
V7X_VMEM_BYTES = 64 * 1024 * 1024

SB_DEAD_LOG2 = -150.0
SB_MASKED_SCORE = -1e30

TOKEN_TILE = 512
MLP_TILE = 1024
SB_BLOCK = 128
SB_TILE = 1024
SB_LOOKBACK = 2
RET_CHUNK = 128
CONV_HALO = 32
CONV_BLOCK = 128
CONV_WINDOW = 256
CONV_SHIFTS = 8
FF_CHUNK = 1024


def _vmem_limit(nbytes):
    return int(min(nbytes, V7X_VMEM_BYTES - 4 * 1024 * 1024))


def _const_spec(shape):
    nd = len(shape)
    return pl.BlockSpec(shape, lambda *_: (0,) * nd, pipeline_mode=pl.Buffered(1))


def _layer_spec(shape, layer):
    nd = len(shape)
    return pl.BlockSpec((None,) + tuple(shape[1:]), lambda *_: (layer,) + (0,) * (nd - 1),
                        pipeline_mode=pl.Buffered(1))


def _mix_kernel(x_ref, g_ref, w_ref, ones_ref, cos_ref, sin_ref, gq_ref, gk_ref, pb_ref,
                shift_ref, dw_ref, db_ref, lg_ref, lb_ref,
                qdec_ref, kdec_ref, dmat_ref, cdec_ref, bd_ref, rg_ref,
                q_ref, k_ref, v_ref, yret_ref, yconv_ref,
                pad_ref, state_ref, rq_s, rk_s, rv_s, sg_s, *, widths):
    sbw, retw, cw = widths
    tm = x_ref.shape[1]
    halo = CONV_HALO

    @pl.when(pl.program_id(1) == 0)
    def _():
        state_ref[...] = jnp.zeros_like(state_ref)
        pad_ref[0:halo, :] = jnp.zeros((halo, cw), pad_ref.dtype)
        pad_ref[halo + tm:, :] = jnp.zeros((pad_ref.shape[0] - halo - tm, cw), pad_ref.dtype)

    x = x_ref[0]
    ms = jnp.mean(x * x, axis=-1, keepdims=True)
    h = (x * lax.rsqrt(ms + EPS) * g_ref[...]).astype(BF16)

    def proj(c0, width):
        return jnp.dot(h, w_ref[:, c0:c0 + width], preferred_element_type=F32)

    def head_norm(u, gain):
        ss = jnp.dot((u * u).astype(BF16), ones_ref[...], preferred_element_type=F32)
        return u * lax.rsqrt(ss * (1.0 / HEAD_DIM) + EPS) * gain

    def rotary(u):
        n = cos_ref.shape[1]
        cos, sin = cos_ref[...], sin_ref[...]
        even = (lax.broadcasted_iota(jnp.int32, cos.shape, 1) & 1) == 0
        out = []
        for c0 in range(0, u.shape[1], n):
            uc = u[:, c0:c0 + n]
            nxt = pltpu.roll(uc, n - 1, axis=1)
            prv = pltpu.roll(uc, 1, axis=1)
            out.append(uc * cos + jnp.where(even, nxt, prv) * sin)
        return jnp.concatenate(out, axis=1)

    c_q, c_v = 0, 2 * sbw
    c_rq, c_rk, c_rv, c_rg = (3 * sbw + i * retw for i in range(4))
    c_a = 3 * sbw + 4 * retw

    def project_conv():
        ag = proj(c_a, 2 * cw) + pb_ref[...]
        pad_ref[halo:halo + tm, :] = (ag[:, :cw] * jax.nn.sigmoid(ag[:, cw:])).astype(pad_ref.dtype)

    def project_ret_gate():
        gate = proj(c_rg, retw)
        sg_s[...] = (gate * jax.nn.sigmoid(gate)).astype(BF16)

    def project_ret_q():
        rq_s[...] = rotary(proj(c_rq, retw)).astype(BF16)

    def project_ret_k():
        rk_s[...] = rotary(proj(c_rk, retw)).astype(BF16)

    def project_ret_v():
        rv_s[...] = proj(c_rv, retw).astype(BF16)

    def project_sb_qk():
        qk = proj(c_q, 2 * sbw)
        q_ref[0] = (head_norm(qk[:, :sbw], gq_ref[...]) * (LOG2E * HEAD_DIM ** -0.5)).astype(BF16)
        k_ref[0] = head_norm(qk[:, sbw:], gk_ref[...]).astype(BF16)

    def project_sb_v():
        v_ref[0] = proj(c_v, sbw).astype(BF16)

    n_shift = CONV_SHIFTS
    first = halo - (CONV_KERNEL - 1)

    def conv_block(r0):
        window = pad_ref[r0:r0 + CONV_WINDOW, :]
        acc = jnp.zeros((CONV_BLOCK, cw), F32) + db_ref[...]
        for shift in range(n_shift):
            if shift == 0:
                moved = window[:CONV_BLOCK + halo].astype(F32)
            else:
                moved = jnp.dot(shift_ref[shift - 1], window, preferred_element_type=F32)
            for tap in range(CONV_KERNEL):
                if (first + tap) % n_shift == shift:
                    a = (first + tap) // n_shift * n_shift
                    acc = acc + moved[a:a + CONV_BLOCK, :] * dw_ref[tap:tap + 1, :]
        mu = jnp.mean(acc, axis=-1, keepdims=True)
        xc = acc - mu
        var = jnp.mean(xc * xc, axis=-1, keepdims=True)
        yn = xc * lax.rsqrt(var + EPS) * lg_ref[...] + lb_ref[...]
        yconv_ref[0, r0:r0 + CONV_BLOCK, :] = (yn * jax.nn.sigmoid(yn)).astype(yconv_ref.dtype)

    ch = qdec_ref.shape[0]
    group = bd_ref.shape[0]
    heads_per_group = group // HEAD_DIM
    lane = lax.broadcasted_iota(jnp.int32, (ch, group), 1)

    def per_head_rows(t):
        return jnp.concatenate(
            [jnp.where((lane >= hh * HEAD_DIM) & (lane < (hh + 1) * HEAD_DIM), t, jnp.zeros_like(t))
             for hh in range(heads_per_group)], axis=0)

    def retention_scores(c0, gi):
        cols = slice(gi * group, (gi + 1) * group)
        sc = lax.dot_general(rq_s[c0:c0 + ch, cols], per_head_rows(rk_s[c0:c0 + ch, cols]),
                             (((1,), (1,)), ((), ())), preferred_element_type=F32)
        return (sc * dmat_ref[gi]).astype(BF16)

    def retention_chunk(c0, gi, scores):
        cols = slice(gi * group, (gi + 1) * group)
        q = rq_s[c0:c0 + ch, cols]
        k = rk_s[c0:c0 + ch, cols]
        v = rv_s[c0:c0 + ch, cols]
        state = state_ref[gi]
        qd = (q.astype(F32) * qdec_ref[:, cols]).astype(BF16)
        out = (jnp.dot(qd, state.astype(BF16), preferred_element_type=F32)
               + jnp.dot(scores, per_head_rows(v), preferred_element_type=F32))
        kd = (k.astype(F32) * kdec_ref[:, cols]).astype(BF16)
        kv = lax.dot_general(kd, v, (((0,), (0,)), ((), ())), preferred_element_type=F32)
        state_ref[gi] = state * cdec_ref[:, cols] + kv * bd_ref[...]
        return out

    def retention_norm(c0, gi, out):
        cols = slice(gi * group, (gi + 1) * group)
        ss = jnp.dot((out * out).astype(BF16), ones_ref[...], preferred_element_type=F32)
        yn = out * lax.rsqrt(ss * (1.0 / HEAD_DIM) + EPS) * rg_ref[:, cols]
        yret_ref[0, c0:c0 + ch, cols] = (yn * sg_s[c0:c0 + ch, cols].astype(F32)).astype(yret_ref.dtype)

    ret_chunks = [(c0, gi) for c0 in range(0, tm, ch) for gi in range(retw // group)]
    project_conv()
    conv_starts = list(range(0, tm, CONV_BLOCK))
    projections = [project_ret_q, project_ret_k, project_ret_v, project_ret_gate]
    for i in range(max(len(conv_starts), len(projections))):
        if i < len(conv_starts):
            conv_block(conv_starts[i])
        if i < len(projections):
            projections[i]()
    fillers = [project_sb_qk, project_sb_v]
    scores, outs = {}, {}
    for i in range(len(ret_chunks) + 2):
        if i < len(ret_chunks):
            scores[i] = retention_scores(*ret_chunks[i])
        if i < len(fillers):
            fillers[i]()
        if 0 <= i - 1 < len(ret_chunks):
            outs[i - 1] = retention_chunk(*ret_chunks[i - 1], scores.pop(i - 1))
        if 0 <= i - 2 < len(ret_chunks):
            retention_norm(*ret_chunks[i - 2], outs.pop(i - 2))
    for f in fillers[len(ret_chunks) + 2:]:
        f()
    pad_ref[0:halo, :] = pad_ref[tm:tm + halo, :]


def _mix_proj(layer, x, g, w, ones_bd, cos_t, sin_t, gq, gk, pb, shifts, dw_w, dw_b, ln_g, ln_b,
              qdec, kdec, dmat, cdec, bd, ret_g, widths):
    B, T, D = x.shape
    sbw, retw, cw = widths
    tm = min(TOKEN_TILE, T)
    assert tm % qdec.shape[0] == 0 and tm % CONV_BLOCK == 0
    grid = (B, T // tm)
    tok = lambda width: pl.BlockSpec((1, tm, width), lambda b, t: (b, t, 0))
    tab = pl.BlockSpec((tm, cos_t.shape[1]), lambda b, t: (t, 0))
    per_layer = lambda a: _layer_spec(a.shape, layer)
    shared = lambda a: _const_spec(a.shape)
    out_widths = (sbw, sbw, sbw, retw, cw)
    vmem = (2 * tm * D * 4 + w[0].size * 2 + 4 * tm * retw * 4 + dmat.size * 4 + 4 * tm * retw * 2
            + 2 * sum(out_widths) * tm * 2 + 24 * 1024 * 1024)
    return pl.pallas_call(
        functools.partial(_mix_kernel, widths=widths),
        grid=grid,
        in_specs=[tok(D), per_layer(g), per_layer(w), shared(ones_bd), tab, tab,
                  per_layer(gq), per_layer(gk), per_layer(pb), shared(shifts), per_layer(dw_w), per_layer(dw_b),
                  per_layer(ln_g), per_layer(ln_b), shared(qdec), shared(kdec), shared(dmat), shared(cdec),
                  shared(bd), per_layer(ret_g)],
        out_specs=[tok(wd) for wd in out_widths],
        out_shape=[jax.ShapeDtypeStruct((B, T, wd), BF16) for wd in out_widths],
        scratch_shapes=[pltpu.VMEM((tm + CONV_WINDOW - CONV_BLOCK, cw), BF16),
                        pltpu.VMEM((retw // bd.shape[0], bd.shape[0], bd.shape[0]), F32)]
                       + [pltpu.VMEM((tm, retw), BF16)] * 4,
        compiler_params=pltpu.CompilerParams(
            dimension_semantics=("arbitrary", "arbitrary"), vmem_limit_bytes=_vmem_limit(vmem)),
        name="mix_proj",
    )(x, g, w, ones_bd, cos_t, sin_t, gq, gk, pb, shifts, dw_w, dw_b, ln_g, ln_b,
      qdec, kdec, dmat, cdec, bd, ret_g)


def _sb_kernel(q_ref, k_ref, v_ref, m_ref, o_ref, vm_ref, acc_ref, c_ref, *, n_heads):
    blk = SB_BLOCK
    tile = q_ref.shape[1]
    seq = k_ref.shape[1]
    t0 = pl.program_id(1) * tile
    row = lax.broadcasted_iota(jnp.int32, (blk, blk), 0)
    col = lax.broadcasted_iota(jnp.int32, (blk, blk), 1)
    causal = col < row

    def head_lanes(shape, h):
        lane = lax.broadcasted_iota(jnp.int32, shape, 1)
        return (lane >= h * HEAD_DIM) & (lane < (h + 1) * HEAD_DIM)

    @pl.when(pl.program_id(1) == 0)
    def _():
        rows = min(SB_TILE, seq)

        def fill(i, _):
            r0 = pl.multiple_of(i * rows, rows)
            vb = v_ref[0, pl.ds(r0, rows), :]
            for h in range(n_heads):
                vm_ref[h, pl.ds(r0, rows), :] = jnp.where(head_lanes(vb.shape, h), vb, jnp.zeros_like(vb))
            return 0

        lax.fori_loop(0, seq // rows, fill, 0)

    all_heads = tuple(range(n_heads))

    def masked_q(off):
        q = q_ref[0, pl.ds(off, blk), :]
        return jnp.concatenate(
            [jnp.where(head_lanes(q.shape, h), q, jnp.zeros_like(q)) for h in all_heads], axis=0)


    def scores(off, kstart, nblk, diag_last):
        kwin = k_ref[0, pl.ds(kstart, nblk * blk), :]
        z_all = lax.dot_general(masked_q(off), kwin, (((1,), (1,)), ((), ())), preferred_element_type=F32)
        log_beta, split = {}, []
        for h in all_heads:
            for j in range(nblk):
                z = z_all[h * blk:(h + 1) * blk, j * blk:(j + 1) * blk]
                if diag_last and j == nblk - 1:
                    z = jnp.where(causal, z, SB_MASKED_SCORE)
                sp = jnp.maximum(z, 0.0) + jnp.log(1.0 + jnp.exp2(-jnp.abs(z))) * (1.0 / LN2)
                log_beta[h, j] = z - sp
                hi = sp.astype(BF16)
                lo = (sp - hi.astype(F32)).astype(BF16)
                split.append(jnp.concatenate([hi, lo], axis=1))
        return jnp.concatenate(split, axis=0), log_beta

    def weights(split, log_beta, nblk, carries):
        r = jnp.dot(split, m_ref[...], preferred_element_type=F32)
        ws, out_carries = [], []
        for h in all_heads:
            carry = carries[h]
            w = [None] * nblk
            for j in reversed(range(nblk)):
                rr = r[(h * nblk + j) * blk:(h * nblk + j + 1) * blk, :]
                arg = rr[:, :blk] + log_beta[h, j]
                if carry is not None:
                    arg = arg + carry
                w[j] = jnp.exp2(arg).astype(BF16)
                carry = rr[:, blk:] if carry is None else carry + rr[:, blk:]
            out_carries.append(carry)
            ws.extend(w)
        return jnp.concatenate(ws, axis=1), out_carries

    def values(ws, kstart, nblk):
        vs = [vm_ref[h, pl.ds(kstart, nblk * blk), :] for h in all_heads]
        return jnp.dot(ws, jnp.concatenate(vs, axis=0), preferred_element_type=F32)

    def top_carry(carries):
        top = carries[0]
        for c in carries[1:]:
            top = jnp.maximum(top, c)
        return top

    def straight(lookbacks):
        def run():
            n = len(lookbacks)
            kstart = [pl.multiple_of(t0 + (s - lookbacks[s]) * blk, blk) for s in range(n)]
            stage1, stage2, tops = {}, {}, []
            for step in range(n + 2):
                if step < n:
                    stage1[step] = scores(step * blk, kstart[step], lookbacks[step] + 1, True)
                s = step - 1
                if 0 <= s < n:
                    stage2[s] = weights(*stage1.pop(s), lookbacks[s] + 1, [None] * n_heads)
                s = step - 2
                if 0 <= s < n:
                    ws, carries = stage2.pop(s)
                    pv = values(ws, kstart[s], lookbacks[s] + 1)
                    acc_ref[s] = pv
                    o_ref[0, s * blk:(s + 1) * blk, :] = pv.astype(o_ref.dtype)
                    for h in all_heads:
                        c_ref[s, h] = carries[h]
                    tops.append(top_carry(carries))
            return jnp.max(top_carry(tops)) > SB_DEAD_LOG2
        return run

    n_sub = tile // blk
    alive = lax.cond(pl.program_id(1) == 0,
                     straight([min(s, SB_LOOKBACK) for s in range(n_sub)]),
                     straight([SB_LOOKBACK] * n_sub))

    @pl.when(alive)
    def _():
        def sub_block(s, _):
            off = pl.multiple_of(s * blk, blk)
            q0 = t0 + off

            def cond(state):
                i, live = state
                return jnp.logical_and(i * blk <= q0, live)

            def body(state):
                i, _ = state
                kstart = pl.multiple_of(q0 - i * blk, blk)
                ws, carries = weights(*scores(off, kstart, 1, False), 1, [c_ref[s, h] for h in all_heads])
                acc_ref[s] += values(ws, kstart, 1)
                for h in all_heads:
                    c_ref[s, h] = carries[h]
                return i + 1, jnp.max(top_carry(carries)) > SB_DEAD_LOG2

            swept = jnp.minimum(q0 // blk, SB_LOOKBACK) + 1
            live = jnp.max(top_carry([c_ref[s, h] for h in all_heads])) > SB_DEAD_LOG2
            lax.while_loop(cond, body, (swept, live))
            o_ref[0, pl.ds(off, blk), :] = acc_ref[s].astype(o_ref.dtype)
            return 0

        lax.fori_loop(0, n_sub, sub_block, 0)


def _sb_attention(q, k, v, m):
    B, T, W = q.shape
    blk = SB_BLOCK
    tile = min(SB_TILE, T)
    n_heads = W // HEAD_DIM
    grid = (B, T // tile)
    seq = pl.BlockSpec((1, T, W), lambda b, i: (b, 0, 0))
    qblk = pl.BlockSpec((1, tile, W), lambda b, i: (b, i, 0))
    vmem = (4 + n_heads) * T * W * 2 + 24 * 1024 * 1024
    return pl.pallas_call(
        functools.partial(_sb_kernel, n_heads=n_heads),
        grid=grid,
        in_specs=[qblk, seq, seq, _const_spec(m.shape)],
        out_specs=qblk,
        out_shape=jax.ShapeDtypeStruct((B, T, W), BF16),
        scratch_shapes=[pltpu.VMEM((n_heads, T, W), BF16), pltpu.VMEM((tile // blk, blk, W), F32),
                        pltpu.VMEM((tile // blk, n_heads, blk, blk), F32)],
        compiler_params=pltpu.CompilerParams(
            dimension_semantics=("arbitrary", "arbitrary"), vmem_limit_bytes=_vmem_limit(vmem)),
        name="sb_attn",
    )(q, k, v, m)


def _out_mlp_kernel(x_ref, ysb_ref, yret_ref, yconv_ref, wo_ref, g_ref, w1_ref, w2_ref, o_ref):
    mixed = jnp.concatenate([ysb_ref[0], yret_ref[0], yconv_ref[0]], axis=1)
    x1 = x_ref[0] + jnp.dot(mixed, wo_ref[...], preferred_element_type=F32)
    ms = jnp.mean(x1 * x1, axis=-1, keepdims=True)
    h = (x1 * lax.rsqrt(ms + EPS) * g_ref[...]).astype(BF16)
    acc = x1
    for c0 in range(0, w1_ref.shape[1], FF_CHUNK):
        f = jnp.maximum(jnp.dot(h, w1_ref[:, c0:c0 + FF_CHUNK], preferred_element_type=F32), 0.0)
        acc = acc + jnp.dot((f * f).astype(BF16), w2_ref[c0:c0 + FF_CHUNK, :], preferred_element_type=F32)
    o_ref[0] = acc


def _out_mlp(layer, x, ysb, yret, yconv, wo, g, w1, w2):
    B, T, D = x.shape
    tm = min(MLP_TILE, T)
    grid = (B, T // tm)
    tok = lambda width: pl.BlockSpec((1, tm, width), lambda b, t: (b, t, 0))
    dff = w1.shape[2]
    vmem = (D * D + 2 * D * dff) * 2 + 4 * tm * D * 4 + 4 * tm * D * 2 + tm * FF_CHUNK * 12 + 12 * 1024 * 1024
    return pl.pallas_call(
        _out_mlp_kernel,
        grid=grid,
        in_specs=[tok(D), tok(ysb.shape[2]), tok(yret.shape[2]), tok(yconv.shape[2])]
                 + [_layer_spec(a.shape, layer) for a in (wo, g, w1, w2)],
        out_specs=tok(D),
        out_shape=jax.ShapeDtypeStruct((B, T, D), F32),
        compiler_params=pltpu.CompilerParams(
            dimension_semantics=("arbitrary", "arbitrary"), vmem_limit_bytes=_vmem_limit(vmem)),
        name="out_mlp",
    )(x, ysb, yret, yconv, wo, g, w1, w2)


def _block_diag_ones(n, block):
    i = np.arange(n)
    return (i[:, None] // block == i[None, :] // block).astype(np.float32)


def _shift_matrices():
    r = np.arange(CONV_BLOCK + CONV_HALO)[None, :, None]
    j = np.arange(CONV_WINDOW)[None, None, :]
    b = np.arange(1, CONV_SHIFTS)[:, None, None]
    return (j == r + b).astype(np.float32)


def _suffix_sum_matrix(blk):
    j = np.arange(2 * blk)[:, None] % blk
    s = np.arange(2 * blk)[None, :]
    return -((s >= blk) | (j > s)).astype(np.float32)


def _rotary_tables(T):
    inv = 1.0 / (ROPE_BASE ** jnp.linspace(0.0, 1.0, HEAD_DIM // 2, dtype=F32))
    ang = jnp.arange(T).astype(F32)[:, None] * inv[None, :]
    cos = jnp.repeat(jnp.cos(ang), 2, axis=1)
    sin = jnp.repeat(jnp.sin(ang), 2, axis=1)
    sign = jnp.where(jnp.arange(HEAD_DIM) % 2 == 0, -1.0, 1.0).astype(F32)
    reps = (1, V7X_LANES // HEAD_DIM)
    return jnp.tile(cos, reps), jnp.tile(sin * sign[None, :], reps)


def _retention_tables(n_heads, ch):
    log_g = jnp.log(1.0 - jnp.exp2(-5.0 - jnp.arange(n_heads, dtype=F32)))
    j = jnp.arange(ch, dtype=F32)
    per_lane = lambda t: jnp.repeat(t, HEAD_DIM, axis=-1)
    qdec = per_lane(jnp.exp(log_g[None, :] * (j + 1.0)[:, None]))
    kdec = per_lane(jnp.exp(log_g[None, :] * (ch - 1.0 - j)[:, None]))
    cdec = per_lane(jnp.exp(log_g * ch)[None, :])
    rel = j[:, None] - j[None, :]
    dmat = jnp.where(rel >= 0, jnp.exp(log_g[:, None, None] * jnp.maximum(rel, 0.0)), 0.0)
    per_group = V7X_MXU_DIM // HEAD_DIM
    dmat = dmat.reshape(n_heads // per_group, per_group, ch, ch).transpose(0, 2, 1, 3)
    return qdec, kdec, dmat.reshape(n_heads // per_group, ch, per_group * ch), cdec


def kernel(x, mix_norm_g, w_in, sb_q_norm_g, sb_k_norm_g, ret_norm_g, conv_pw_b, conv_dw_w, conv_dw_b,
           conv_ln_g, conv_ln_b, w_out, mlp_norm_g, w_ff1, w_ff2):
    B, T, D = x.shape
    depth = w_in.shape[0]
    sbw = (D // 256) * HEAD_DIM
    retw = (D // 128) * HEAD_DIM
    cw = D - sbw - retw
    assert all(T % tile == 0 or T < tile for tile in (TOKEN_TILE, MLP_TILE, SB_TILE))
    assert T % RET_CHUNK == 0 and T % SB_BLOCK == 0
    widths = (sbw, retw, cw)
    n_ret_heads = retw // HEAD_DIM

    ones_bd = jnp.asarray(_block_diag_ones(V7X_MXU_DIM, HEAD_DIM), BF16)
    bd_mask = jnp.asarray(_block_diag_ones(V7X_MXU_DIM, HEAD_DIM), F32)
    sfx = jnp.asarray(_suffix_sum_matrix(SB_BLOCK), BF16)
    shifts = jnp.asarray(_shift_matrices(), BF16)
    cos_t, sin_t = _rotary_tables(T)
    qdec, kdec, dmat, cdec = _retention_tables(n_ret_heads, RET_CHUNK)
    rk0 = 3 * sbw + retw
    col_scale = jnp.ones((w_in.shape[2],), F32).at[rk0:rk0 + retw].set(HEAD_DIM ** -0.5)

    rows = lambda a: a.reshape(depth, 1, -1).astype(F32)
    w_in_b = (w_in * col_scale[None, None, :]).astype(BF16)
    w_out_b, w_ff1_b, w_ff2_b = (w.astype(BF16) for w in (w_out, w_ff1, w_ff2))
    mix_g, mlp_g, ret_g, pw_b = rows(mix_norm_g), rows(mlp_norm_g), rows(ret_norm_g), rows(conv_pw_b)
    gq, gk = (rows(jnp.tile(g, (1, sbw // HEAD_DIM))) for g in (sb_q_norm_g, sb_k_norm_g))
    dw_w, dw_b, ln_g, ln_b = conv_dw_w.astype(F32), rows(conv_dw_b), rows(conv_ln_g), rows(conv_ln_b)
    for l in range(depth):
        q, k, v, y_ret, y_conv = _mix_proj(
            l, x, mix_g, w_in_b, ones_bd, cos_t, sin_t, gq, gk, pw_b, shifts, dw_w, dw_b, ln_g, ln_b,
            qdec, kdec, dmat, cdec, bd_mask, ret_g, widths)
        y_sb = _sb_attention(q, k, v, sfx)
        x = _out_mlp(l, x, y_sb, y_ret, y_conv, w_out_b, mlp_g, w_ff1_b, w_ff2_b)
    return x
```

```python
import functools

import jax
import jax.numpy as jnp
import numpy as np
from jax import lax
from jax.experimental import pallas as pl
from jax.experimental.pallas import tpu as pltpu

F32 = jnp.float32
BF16 = jnp.bfloat16

HEAD_DIM = 64
CONV_KERNEL = 31
EPS = 1e-6
ROPE_BASE = 10000.0
LOG2E = 1.4426950408889634
LN2 = 0.6931471805599453

V7X_LANES = 128
V7X_MXU_DIM = 256
V7X_VMEM_BYTES = 64 * 1024 * 1024

SB_DEAD_LOG2 = -150.0
SB_MASKED_SCORE = -1e30

TOKEN_TILE = 512
MLP_TILE = 1024
SB_BLOCK = 128
SB_LOOKBACK = 2
RET_CHUNK = 128
CONV_HALO = 32
CONV_BLOCK = 128
CONV_WINDOW = 256
CONV_SHIFTS = 8
FF_CHUNK = 1024


def _vmem_limit(nbytes):
    return int(min(nbytes, V7X_VMEM_BYTES - 4 * 1024 * 1024))


def _const_spec(shape):
    nd = len(shape)
    return pl.BlockSpec(shape, lambda *_: (0,) * nd, pipeline_mode=pl.Buffered(1))


def _layer_spec(shape, layer):
    nd = len(shape)
    return pl.BlockSpec((None,) + tuple(shape[1:]), lambda *_: (layer,) + (0,) * (nd - 1),
                        pipeline_mode=pl.Buffered(1))


def _head_lanes(shape, h):
    lane = lax.broadcasted_iota(jnp.int32, shape, 1)
    return (lane >= h * HEAD_DIM) & (lane < (h + 1) * HEAD_DIM)


def _sb_stages(q_ref, k_ref, vm_ref, m_ref, n_heads):
    blk = SB_BLOCK
    heads = tuple(range(n_heads))
    causal = (lax.broadcasted_iota(jnp.int32, (blk, blk), 1) < lax.broadcasted_iota(jnp.int32, (blk, blk), 0))

    def masked_q(off):
        q = q_ref[pl.ds(off, blk), :]
        return jnp.concatenate([jnp.where(_head_lanes(q.shape, h), q, jnp.zeros_like(q)) for h in heads], axis=0)

    def scores(off, krow, nblk, diag_last, valid=None):
        kwin = k_ref[pl.ds(krow, nblk * blk), :]
        z_all = lax.dot_general(masked_q(off), kwin, (((1,), (1,)), ((), ())), preferred_element_type=F32)
        log_beta, split = {}, []
        for h in heads:
            for j in range(nblk):
                z = z_all[h * blk:(h + 1) * blk, j * blk:(j + 1) * blk]
                if diag_last and j == nblk - 1:
                    z = jnp.where(causal, z, SB_MASKED_SCORE)
                elif valid is not None and valid[j] is not None:
                    z = jnp.where(valid[j], z, SB_MASKED_SCORE)
                sp = jnp.maximum(z, 0.0) + jnp.log(1.0 + jnp.exp2(-jnp.abs(z))) * (1.0 / LN2)
                log_beta[h, j] = z - sp
                hi = sp.astype(BF16)
                lo = (sp - hi.astype(F32)).astype(BF16)
                split.append(jnp.concatenate([hi, lo], axis=1))
        return jnp.concatenate(split, axis=0), log_beta

    def weights(split, log_beta, nblk, carries):
        r = jnp.dot(split, m_ref[...], preferred_element_type=F32)
        ws, out_carries = [], []
        for h in heads:
            carry = carries[h]
            w = [None] * nblk
            for j in reversed(range(nblk)):
                rr = r[(h * nblk + j) * blk:(h * nblk + j + 1) * blk, :]
                arg = rr[:, :blk] + log_beta[h, j]
                if carry is not None:
                    arg = arg + carry
                w[j] = jnp.exp2(arg).astype(BF16)
                carry = rr[:, blk:] if carry is None else carry + rr[:, blk:]
            out_carries.append(carry)
            ws.extend(w)
        return jnp.concatenate(ws, axis=1), out_carries

    def values(ws, krow, nblk):
        vs = [vm_ref[h, pl.ds(krow, nblk * blk), :] for h in heads]
        return jnp.dot(ws, jnp.concatenate(vs, axis=0), preferred_element_type=F32)

    def top_carry(carries):
        top = carries[0]
        for c in carries[1:]:
            top = jnp.maximum(top, c)
        return top

    return scores, weights, values, top_carry


def _mixers_kernel(x_ref, g_ref, w_ref, ones_ref, cos_ref, sin_ref, gq_ref, gk_ref, pb_ref,
                   shift_ref, dw_ref, db_ref, lg_ref, lb_ref,
                   qdec_ref, kdec_ref, dmat_ref, cdec_ref, bd_ref, rg_ref, sfx_ref,
                   ysb_ref, yret_ref, yconv_ref,
                   pad_ref, state_ref, rq_s, rk_s, rv_s, sg_s, q_s, k_all, vm_all, acc_ref, c_ref, *, widths):
    sbw, retw, cw = widths
    tm = x_ref.shape[1]
    halo = CONV_HALO
    blk = SB_BLOCK
    sb_pad = SB_LOOKBACK * blk
    n_sb_heads = sbw // HEAD_DIM
    t0 = pl.multiple_of(pl.program_id(1) * tm, tm)
    has_earlier_tile = pl.program_id(1) > 0

    @pl.when(pl.program_id(1) == 0)
    def _():
        state_ref[...] = jnp.zeros_like(state_ref)
        pad_ref[0:halo, :] = jnp.zeros((halo, cw), pad_ref.dtype)
        pad_ref[halo + tm:, :] = jnp.zeros((pad_ref.shape[0] - halo - tm, cw), pad_ref.dtype)
        k_all[0:sb_pad, :] = jnp.zeros((sb_pad, sbw), k_all.dtype)
        vm_all[:, 0:sb_pad, :] = jnp.zeros((n_sb_heads, sb_pad, sbw), vm_all.dtype)

    x = x_ref[0]
    ms = jnp.mean(x * x, axis=-1, keepdims=True)
    h = (x * lax.rsqrt(ms + EPS) * g_ref[...]).astype(BF16)

    def proj(c0, width):
        return jnp.dot(h, w_ref[:, c0:c0 + width], preferred_element_type=F32)

    def head_norm(u, gain):
        ss = jnp.dot((u * u).astype(BF16), ones_ref[...], preferred_element_type=F32)
        return u * lax.rsqrt(ss * (1.0 / HEAD_DIM) + EPS) * gain

    def rotary(u):
        n = cos_ref.shape[1]
        cos, sin = cos_ref[...], sin_ref[...]
        even = (lax.broadcasted_iota(jnp.int32, cos.shape, 1) & 1) == 0
        out = []
        for c0 in range(0, u.shape[1], n):
            uc = u[:, c0:c0 + n]
            nxt = pltpu.roll(uc, n - 1, axis=1)
            prv = pltpu.roll(uc, 1, axis=1)
            out.append(uc * cos + jnp.where(even, nxt, prv) * sin)
        return jnp.concatenate(out, axis=1)

    c_q, c_v = 0, 2 * sbw
    c_rq, c_rk, c_rv, c_rg = (3 * sbw + i * retw for i in range(4))
    c_a = 3 * sbw + 4 * retw

    def project_conv():
        ag = proj(c_a, 2 * cw) + pb_ref[...]
        pad_ref[halo:halo + tm, :] = (ag[:, :cw] * jax.nn.sigmoid(ag[:, cw:])).astype(pad_ref.dtype)

    def project_ret_gate():
        gate = proj(c_rg, retw)
        sg_s[...] = (gate * jax.nn.sigmoid(gate)).astype(BF16)

    def project_ret_q():
        rq_s[...] = rotary(proj(c_rq, retw)).astype(BF16)

    def project_ret_k():
        rk_s[...] = rotary(proj(c_rk, retw)).astype(BF16)

    def project_ret_v():
        rv_s[...] = proj(c_rv, retw).astype(BF16)

    def project_sb_qk():
        qk = proj(c_q, 2 * sbw)
        q_s[...] = (head_norm(qk[:, :sbw], gq_ref[...]) * (LOG2E * HEAD_DIM ** -0.5)).astype(BF16)
        k_all[pl.ds(sb_pad + t0, tm), :] = head_norm(qk[:, sbw:], gk_ref[...]).astype(BF16)

    def project_sb_v():
        v = proj(c_v, sbw).astype(BF16)
        for hh in range(n_sb_heads):
            vm_all[hh, pl.ds(sb_pad + t0, tm), :] = jnp.where(_head_lanes(v.shape, hh), v, jnp.zeros_like(v))

    n_shift = CONV_SHIFTS
    first = halo - (CONV_KERNEL - 1)

    def conv_block(r0):
        window = pad_ref[r0:r0 + CONV_WINDOW, :]
        acc = jnp.zeros((CONV_BLOCK, cw), F32) + db_ref[...]
        for shift in range(n_shift):
            if shift == 0:
                moved = window[:CONV_BLOCK + halo].astype(F32)
            else:
                moved = jnp.dot(shift_ref[shift - 1], window, preferred_element_type=F32)
            for tap in range(CONV_KERNEL):
                if (first + tap) % n_shift == shift:
                    a = (first + tap) // n_shift * n_shift
                    acc = acc + moved[a:a + CONV_BLOCK, :] * dw_ref[tap:tap + 1, :]
        mu = jnp.mean(acc, axis=-1, keepdims=True)
        xc = acc - mu
        var = jnp.mean(xc * xc, axis=-1, keepdims=True)
        yn = xc * lax.rsqrt(var + EPS) * lg_ref[...] + lb_ref[...]
        yconv_ref[0, r0:r0 + CONV_BLOCK, :] = (yn * jax.nn.sigmoid(yn)).astype(yconv_ref.dtype)

    ch = qdec_ref.shape[0]
    group = bd_ref.shape[0]
    heads_per_group = group // HEAD_DIM

    def per_head_rows(t):
        return jnp.concatenate(
            [jnp.where(_head_lanes(t.shape, hh), t, jnp.zeros_like(t)) for hh in range(heads_per_group)], axis=0)

    def retention_scores(c0, gi):
        cols = slice(gi * group, (gi + 1) * group)
        sc = lax.dot_general(rq_s[c0:c0 + ch, cols], per_head_rows(rk_s[c0:c0 + ch, cols]),
                             (((1,), (1,)), ((), ())), preferred_element_type=F32)
        return (sc * dmat_ref[gi]).astype(BF16)

    def retention_chunk(c0, gi, scores):
        cols = slice(gi * group, (gi + 1) * group)
        q = rq_s[c0:c0 + ch, cols]
        k = rk_s[c0:c0 + ch, cols]
        v = rv_s[c0:c0 + ch, cols]
        state = state_ref[gi]
        qd = (q.astype(F32) * qdec_ref[:, cols]).astype(BF16)
        out = (jnp.dot(qd, state.astype(BF16), preferred_element_type=F32)
               + jnp.dot(scores, per_head_rows(v), preferred_element_type=F32))
        kd = (k.astype(F32) * kdec_ref[:, cols]).astype(BF16)
        kv = lax.dot_general(kd, v, (((0,), (0,)), ((), ())), preferred_element_type=F32)
        state_ref[gi] = state * cdec_ref[:, cols] + kv * bd_ref[...]
        return out

    def retention_norm(c0, gi, out):
        cols = slice(gi * group, (gi + 1) * group)
        ss = jnp.dot((out * out).astype(BF16), ones_ref[...], preferred_element_type=F32)
        yn = out * lax.rsqrt(ss * (1.0 / HEAD_DIM) + EPS) * rg_ref[:, cols]
        yret_ref[0, c0:c0 + ch, cols] = (yn * sg_s[c0:c0 + ch, cols].astype(F32)).astype(yret_ref.dtype)

    sb_scores, sb_weights, sb_values, top_carry = _sb_stages(q_s, k_all, vm_all, sfx_ref, n_sb_heads)
    n_sub = tm // blk
    n_sweep = SB_LOOKBACK + 1
    sb_stage1, sb_stage2, sb_tops = {}, {}, []

    def sb_krow(s):
        return pl.multiple_of(sb_pad + t0 + (s - SB_LOOKBACK) * blk, blk)

    def sb_step(step):
        if step < n_sub:
            valid = [has_earlier_tile if step - SB_LOOKBACK + j < 0 else None for j in range(n_sweep)]
            sb_stage1[step] = sb_scores(step * blk, sb_krow(step), n_sweep, True, valid)
        s = step - 1
        if 0 <= s < n_sub:
            sb_stage2[s] = sb_weights(*sb_stage1.pop(s), n_sweep, [None] * n_sb_heads)
        s = step - 2
        if 0 <= s < n_sub:
            ws, carries = sb_stage2.pop(s)
            pv = sb_values(ws, sb_krow(s), n_sweep)
            acc_ref[s] = pv
            ysb_ref[0, s * blk:(s + 1) * blk, :] = pv.astype(ysb_ref.dtype)
            for hh in range(n_sb_heads):
                c_ref[s, hh] = carries[hh]
            sb_tops.append(top_carry(carries))

    ret_chunks = [(c0, gi) for c0 in range(0, tm, ch) for gi in range(retw // group)]
    project_conv()
    project_sb_qk()
    project_sb_v()
    conv_starts = list(range(0, tm, CONV_BLOCK))
    projections = [project_ret_q, project_ret_k, project_ret_v, project_ret_gate]
    sb_steps = list(range(n_sub + 2))
    for i in range(max(len(conv_starts), len(projections))):
        if i < len(conv_starts):
            conv_block(conv_starts[i])
        if i < len(projections):
            projections[i]()
        if sb_steps:
            sb_step(sb_steps.pop(0))
    pad_ref[0:halo, :] = pad_ref[tm:tm + halo, :]
    scores, outs = {}, {}
    for i in range(len(ret_chunks) + 2):
        if i < len(ret_chunks):
            scores[i] = retention_scores(*ret_chunks[i])
        if sb_steps:
            sb_step(sb_steps.pop(0))
        if 0 <= i - 1 < len(ret_chunks):
            outs[i - 1] = retention_chunk(*ret_chunks[i - 1], scores.pop(i - 1))
        if 0 <= i - 2 < len(ret_chunks):
            retention_norm(*ret_chunks[i - 2], outs.pop(i - 2))
    for step in sb_steps:
        sb_step(step)

    @pl.when(jnp.max(top_carry(sb_tops)) > SB_DEAD_LOG2)
    def _():
        def sub_block(s, _):
            off = pl.multiple_of(s * blk, blk)
            q0 = t0 + off

            def cond(state):
                i, live = state
                return jnp.logical_and(i * blk <= q0, live)

            def body(state):
                i, _ = state
                krow = pl.multiple_of(sb_pad + q0 - i * blk, blk)
                ws, carries = sb_weights(*sb_scores(off, krow, 1, False), 1,
                                         [c_ref[s, hh] for hh in range(n_sb_heads)])
                acc_ref[s] += sb_values(ws, krow, 1)
                for hh in range(n_sb_heads):
                    c_ref[s, hh] = carries[hh]
                return i + 1, jnp.max(top_carry(carries)) > SB_DEAD_LOG2

            live = jnp.max(top_carry([c_ref[s, hh] for hh in range(n_sb_heads)])) > SB_DEAD_LOG2
            lax.while_loop(cond, body, (jnp.int32(n_sweep), live))
            ysb_ref[0, pl.ds(off, blk), :] = acc_ref[s].astype(ysb_ref.dtype)
            return 0

        lax.fori_loop(0, n_sub, sub_block, 0)


def _mixers(layer, x, g, w, ones_bd, cos_t, sin_t, gq, gk, pb, shifts, dw_w, dw_b, ln_g, ln_b,
            qdec, kdec, dmat, cdec, bd, ret_g, sfx, widths):
    B, T, D = x.shape
    sbw, retw, cw = widths
    tm = min(TOKEN_TILE, T)
    blk = SB_BLOCK
    assert tm % qdec.shape[0] == 0 and tm % CONV_BLOCK == 0 and tm % blk == 0
    n_sb_heads = sbw // HEAD_DIM
    key_rows = SB_LOOKBACK * blk + T
    grid = (B, T // tm)
    tok = lambda width: pl.BlockSpec((1, tm, width), lambda b, t: (b, t, 0))
    tab = pl.BlockSpec((tm, cos_t.shape[1]), lambda b, t: (t, 0))
    per_layer = lambda a: _layer_spec(a.shape, layer)
    shared = lambda a: _const_spec(a.shape)
    out_widths = (sbw, retw, cw)
    vmem = (2 * tm * D * 4 + w[0].size * 2 + dmat.size * 4 + 4 * tm * retw * 2 + (1 + n_sb_heads) * key_rows * sbw * 2
            + 2 * sum(out_widths) * tm * 2 + 28 * 1024 * 1024)
    return pl.pallas_call(
        functools.partial(_mixers_kernel, widths=widths),
        grid=grid,
        in_specs=[tok(D), per_layer(g), per_layer(w), shared(ones_bd), tab, tab,
                  per_layer(gq), per_layer(gk), per_layer(pb), shared(shifts), per_layer(dw_w), per_layer(dw_b),
                  per_layer(ln_g), per_layer(ln_b), shared(qdec), shared(kdec), shared(dmat), shared(cdec),
                  shared(bd), per_layer(ret_g), shared(sfx)],
        out_specs=[tok(wd) for wd in out_widths],
        out_shape=[jax.ShapeDtypeStruct((B, T, wd), BF16) for wd in out_widths],
        scratch_shapes=[pltpu.VMEM((tm + CONV_WINDOW - CONV_BLOCK, cw), BF16),
                        pltpu.VMEM((retw // bd.shape[0], bd.shape[0], bd.shape[0]), F32)]
                       + [pltpu.VMEM((tm, retw), BF16)] * 4
                       + [pltpu.VMEM((tm, sbw), BF16), pltpu.VMEM((key_rows, sbw), BF16),
                          pltpu.VMEM((n_sb_heads, key_rows, sbw), BF16),
                          pltpu.VMEM((tm // blk, blk, sbw), F32),
                          pltpu.VMEM((tm // blk, n_sb_heads, blk, blk), F32)],
        compiler_params=pltpu.CompilerParams(
            dimension_semantics=("arbitrary", "arbitrary"), vmem_limit_bytes=_vmem_limit(vmem)),
        name="mixers",
    )(x, g, w, ones_bd, cos_t, sin_t, gq, gk, pb, shifts, dw_w, dw_b, ln_g, ln_b,
      qdec, kdec, dmat, cdec, bd, ret_g, sfx)


def _out_mlp_kernel(x_ref, ysb_ref, yret_ref, yconv_ref, wo_ref, g_ref, w1_ref, w2_ref, o_ref):
    mixed = jnp.concatenate([ysb_ref[0], yret_ref[0], yconv_ref[0]], axis=1)
    x1 = x_ref[0] + jnp.dot(mixed, wo_ref[...], preferred_element_type=F32)
    ms = jnp.mean(x1 * x1, axis=-1, keepdims=True)
    h = (x1 * lax.rsqrt(ms + EPS) * g_ref[...]).astype(BF16)
    acc = x1
    for c0 in range(0, w1_ref.shape[1], FF_CHUNK):
        f = jnp.maximum(jnp.dot(h, w1_ref[:, c0:c0 + FF_CHUNK], preferred_element_type=F32), 0.0)
        acc = acc + jnp.dot((f * f).astype(BF16), w2_ref[c0:c0 + FF_CHUNK, :], preferred_element_type=F32)
    o_ref[0] = acc


def _out_mlp(layer, x, ysb, yret, yconv, wo, g, w1, w2):
    B, T, D = x.shape
    tm = min(MLP_TILE, T)
    grid = (B, T // tm)
    tok = lambda width: pl.BlockSpec((1, tm, width), lambda b, t: (b, t, 0))
    dff = w1.shape[2]
    vmem = (D * D + 2 * D * dff) * 2 + 4 * tm * D * 4 + 4 * tm * D * 2 + tm * FF_CHUNK * 12 + 12 * 1024 * 1024
    return pl.pallas_call(
        _out_mlp_kernel,
        grid=grid,
        in_specs=[tok(D), tok(ysb.shape[2]), tok(yret.shape[2]), tok(yconv.shape[2])]
                 + [_layer_spec(a.shape, layer) for a in (wo, g, w1, w2)],
        out_specs=tok(D),
        out_shape=jax.ShapeDtypeStruct((B, T, D), F32),
        compiler_params=pltpu.CompilerParams(
            dimension_semantics=("arbitrary", "arbitrary"), vmem_limit_bytes=_vmem_limit(vmem)),
        name="out_mlp",
    )(x, ysb, yret, yconv, wo, g, w1, w2)


def _block_diag_ones(n, block):
    i = np.arange(n)
    return (i[:, None] // block == i[None, :] // block).astype(np.float32)


def _shift_matrices():
    r = np.arange(CONV_BLOCK + CONV_HALO)[None, :, None]
    j = np.arange(CONV_WINDOW)[None, None, :]
    b = np.arange(1, CONV_SHIFTS)[:, None, None]
    return (j == r + b).astype(np.float32)


def _suffix_sum_matrix(blk):
    j = np.arange(2 * blk)[:, None] % blk
    s = np.arange(2 * blk)[None, :]
    return -((s >= blk) | (j > s)).astype(np.float32)


def _rotary_tables(T):
    inv = 1.0 / (ROPE_BASE ** jnp.linspace(0.0, 1.0, HEAD_DIM // 2, dtype=F32))
    ang = jnp.arange(T).astype(F32)[:, None] * inv[None, :]
    cos = jnp.repeat(jnp.cos(ang), 2, axis=1)
    sin = jnp.repeat(jnp.sin(ang), 2, axis=1)
    sign = jnp.where(jnp.arange(HEAD_DIM) % 2 == 0, -1.0, 1.0).astype(F32)
    reps = (1, V7X_LANES // HEAD_DIM)
    return jnp.tile(cos, reps), jnp.tile(sin * sign[None, :], reps)


def _retention_tables(n_heads, ch):
    log_g = jnp.log(1.0 - jnp.exp2(-5.0 - jnp.arange(n_heads, dtype=F32)))
    j = jnp.arange(ch, dtype=F32)
    per_lane = lambda t: jnp.repeat(t, HEAD_DIM, axis=-1)
    qdec = per_lane(jnp.exp(log_g[None, :] * (j + 1.0)[:, None]))
    kdec = per_lane(jnp.exp(log_g[None, :] * (ch - 1.0 - j)[:, None]))
    cdec = per_lane(jnp.exp(log_g * ch)[None, :])
    rel = j[:, None] - j[None, :]
    dmat = jnp.where(rel >= 0, jnp.exp(log_g[:, None, None] * jnp.maximum(rel, 0.0)), 0.0)
    per_group = V7X_MXU_DIM // HEAD_DIM
    dmat = dmat.reshape(n_heads // per_group, per_group, ch, ch).transpose(0, 2, 1, 3)
    return qdec, kdec, dmat.reshape(n_heads // per_group, ch, per_group * ch), cdec


def kernel(x, mix_norm_g, w_in, sb_q_norm_g, sb_k_norm_g, ret_norm_g, conv_pw_b, conv_dw_w, conv_dw_b,
           conv_ln_g, conv_ln_b, w_out, mlp_norm_g, w_ff1, w_ff2):
    B, T, D = x.shape
    depth = w_in.shape[0]
    sbw = (D // 256) * HEAD_DIM
    retw = (D // 128) * HEAD_DIM
    cw = D - sbw - retw
    assert all(T % tile == 0 or T < tile for tile in (TOKEN_TILE, MLP_TILE))
    assert T % RET_CHUNK == 0 and T % SB_BLOCK == 0
    widths = (sbw, retw, cw)
    n_ret_heads = retw // HEAD_DIM

    ones_bd = jnp.asarray(_block_diag_ones(V7X_MXU_DIM, HEAD_DIM), BF16)
    bd_mask = jnp.asarray(_block_diag_ones(V7X_MXU_DIM, HEAD_DIM), F32)
    sfx = jnp.asarray(_suffix_sum_matrix(SB_BLOCK), BF16)
    shifts = jnp.asarray(_shift_matrices(), BF16)
    cos_t, sin_t = _rotary_tables(T)
    qdec, kdec, dmat, cdec = _retention_tables(n_ret_heads, RET_CHUNK)
    rk0 = 3 * sbw + retw
    col_scale = jnp.ones((w_in.shape[2],), F32).at[rk0:rk0 + retw].set(HEAD_DIM ** -0.5)

    rows = lambda a: a.reshape(depth, 1, -1).astype(F32)
    w_in_b = (w_in * col_scale[None, None, :]).astype(BF16)
    w_out_b, w_ff1_b, w_ff2_b = (w.astype(BF16) for w in (w_out, w_ff1, w_ff2))
    mix_g, mlp_g, ret_g, pw_b = rows(mix_norm_g), rows(mlp_norm_g), rows(ret_norm_g), rows(conv_pw_b)
    gq, gk = (rows(jnp.tile(g, (1, sbw // HEAD_DIM))) for g in (sb_q_norm_g, sb_k_norm_g))
    dw_w, dw_b, ln_g, ln_b = conv_dw_w.astype(F32), rows(conv_dw_b), rows(conv_ln_g), rows(conv_ln_b)
    for l in range(depth):
        y_sb, y_ret, y_conv = _mixers(
            l, x, mix_g, w_in_b, ones_bd, cos_t, sin_t, gq, gk, pw_b, shifts, dw_w, dw_b, ln_g, ln_b,
            qdec, kdec, dmat, cdec, bd_mask, ret_g, sfx, widths)
        x = _out_mlp(l, x, y_sb, y_ret, y_conv, w_out_b, mlp_g, w_ff1_b, w_ff2_b)
    return x
```

```python
import functools

import jax
import jax.numpy as jnp
import numpy as np
from jax import lax
from jax.experimental import pallas as pl
from jax.experimental.pallas import tpu as pltpu

F32 = jnp.float32
BF16 = jnp.bfloat16

HEAD_DIM = 64
CONV_KERNEL = 31
EPS = 1e-6
ROPE_BASE = 10000.0
LOG2E = 1.4426950408889634
LN2 = 0.6931471805599453

V7X_LANES = 128
V7X_MXU_DIM = 256
V7X_VMEM_BYTES = 64 * 1024 * 1024

SB_DEAD_LOG2 = -150.0
SB_MASKED_SCORE = -1e30

TOKEN_TILE = 512
MLP_TILE = 1024
SB_BLOCK = 128
SB_LOOKBACK = 2
RET_CHUNK = 128
CONV_HALO = 32
CONV_BLOCK = 128
CONV_WINDOW = 256
CONV_SHIFTS = 8
FF_CHUNK = 1024


def _vmem_limit(nbytes):
    return int(min(nbytes, V7X_VMEM_BYTES - 4 * 1024 * 1024))


def _const_spec(shape):
    nd = len(shape)
    return pl.BlockSpec(shape, lambda *_: (0,) * nd, pipeline_mode=pl.Buffered(1))


def _layer_spec(shape, layer):
    nd = len(shape)
    return pl.BlockSpec((None,) + tuple(shape[1:]), lambda *_: (layer,) + (0,) * (nd - 1),
                        pipeline_mode=pl.Buffered(1))


def _head_lanes(shape, h):
    lane = lax.broadcasted_iota(jnp.int32, shape, 1)
    return (lane >= h * HEAD_DIM) & (lane < (h + 1) * HEAD_DIM)


def _sb_stages(q_ref, k_ref, vm_ref, m_ref, n_heads):
    blk = SB_BLOCK
    heads = tuple(range(n_heads))
    causal = (lax.broadcasted_iota(jnp.int32, (blk, blk), 1) < lax.broadcasted_iota(jnp.int32, (blk, blk), 0))

    def masked_q(off):
        q = q_ref[pl.ds(off, blk), :]
        return jnp.concatenate([jnp.where(_head_lanes(q.shape, h), q, jnp.zeros_like(q)) for h in heads], axis=0)

    def scores(off, krow, nblk, diag_last, valid=None):
        kwin = k_ref[pl.ds(krow, nblk * blk), :]
        z_all = lax.dot_general(masked_q(off), kwin, (((1,), (1,)), ((), ())), preferred_element_type=F32)
        log_beta, totals, operands = {}, {}, []
        for h in heads:
            for j in range(nblk):
                z = z_all[h * blk:(h + 1) * blk, j * blk:(j + 1) * blk]
                if diag_last and j == nblk - 1:
                    z = jnp.where(causal, z, SB_MASKED_SCORE)
                elif valid is not None and valid[j] is not None:
                    z = jnp.where(valid[j], z, SB_MASKED_SCORE)
                sp = jnp.maximum(z, 0.0) + jnp.log(1.0 + jnp.exp2(-jnp.abs(z))) * (1.0 / LN2)
                log_beta[h, j] = z - sp
                totals[h, j] = jnp.sum(sp, axis=1, keepdims=True)
                operands.append(sp.astype(BF16))
        pairs = [jnp.concatenate(operands[u:u + 2], axis=1) for u in range(0, len(operands), 2)]
        return jnp.concatenate(pairs, axis=0), log_beta, totals

    def weights(split, log_beta, totals, nblk, carries):
        r = jnp.dot(split, m_ref[...], preferred_element_type=F32)
        ws, out_carries = [], []
        for h in heads:
            carry = carries[h]
            w = [None] * nblk
            for j in reversed(range(nblk)):
                u = h * nblk + j
                arg = r[u // 2 * blk:(u // 2 + 1) * blk, u % 2 * blk:(u % 2 + 1) * blk] + log_beta[h, j]
                if carry is not None:
                    arg = arg + carry
                w[j] = jnp.exp2(arg).astype(BF16)
                carry = -totals[h, j] if carry is None else carry - totals[h, j]
            out_carries.append(carry)
            ws.extend(w)
        return jnp.concatenate(ws, axis=1), out_carries

    def values(ws, krow, nblk):
        vs = [vm_ref[h, pl.ds(krow, nblk * blk), :] for h in heads]
        return jnp.dot(ws, jnp.concatenate(vs, axis=0), preferred_element_type=F32)

    def top_carry(carries):
        top = carries[0]
        for c in carries[1:]:
            top = jnp.maximum(top, c)
        return top

    return scores, weights, values, top_carry


def _mixers_kernel(x_ref, g_ref, w_ref, ones_ref, cos_ref, sin_ref, gq_ref, gk_ref, pb_ref,
                   shift_ref, dw_ref, db_ref, lg_ref, lb_ref,
                   qdec_ref, kdec_ref, dmat_ref, cdec_ref, bd_ref, rg_ref, sfx_ref,
                   ysb_ref, yret_ref, yconv_ref,
                   pad_ref, state_ref, rq_s, rk_s, rv_s, sg_s, q_s, k_all, vm_all, acc_ref, c_ref, *, widths):
    sbw, retw, cw = widths
    tm = x_ref.shape[1]
    halo = CONV_HALO
    blk = SB_BLOCK
    sb_pad = SB_LOOKBACK * blk
    n_sb_heads = sbw // HEAD_DIM
    t0 = pl.multiple_of(pl.program_id(1) * tm, tm)
    has_earlier_tile = pl.program_id(1) > 0

    @pl.when(pl.program_id(1) == 0)
    def _():
        state_ref[...] = jnp.zeros_like(state_ref)
        pad_ref[0:halo, :] = jnp.zeros((halo, cw), pad_ref.dtype)
        pad_ref[halo + tm:, :] = jnp.zeros((pad_ref.shape[0] - halo - tm, cw), pad_ref.dtype)
        k_all[0:sb_pad, :] = jnp.zeros((sb_pad, sbw), k_all.dtype)
        vm_all[:, 0:sb_pad, :] = jnp.zeros((n_sb_heads, sb_pad, sbw), vm_all.dtype)

    x = x_ref[0]
    ms = jnp.mean(x * x, axis=-1, keepdims=True)
    h = (x * lax.rsqrt(ms + EPS) * g_ref[...]).astype(BF16)

    def proj(c0, width):
        return jnp.dot(h, w_ref[:, c0:c0 + width], preferred_element_type=F32)

    def head_norm(u, gain):
        ss = jnp.dot((u * u).astype(BF16), ones_ref[...], preferred_element_type=F32)
        return u * lax.rsqrt(ss * (1.0 / HEAD_DIM) + EPS) * gain

    def rotary(u):
        n = cos_ref.shape[1]
        cos, sin = cos_ref[...], sin_ref[...]
        even = (lax.broadcasted_iota(jnp.int32, cos.shape, 1) & 1) == 0
        out = []
        for c0 in range(0, u.shape[1], n):
            uc = u[:, c0:c0 + n]
            nxt = pltpu.roll(uc, n - 1, axis=1)
            prv = pltpu.roll(uc, 1, axis=1)
            out.append(uc * cos + jnp.where(even, nxt, prv) * sin)
        return jnp.concatenate(out, axis=1)

    c_q, c_v = 0, 2 * sbw
    c_rq, c_rk, c_rv, c_rg = (3 * sbw + i * retw for i in range(4))
    c_a = 3 * sbw + 4 * retw

    def project_conv():
        ag = proj(c_a, 2 * cw) + pb_ref[...]
        pad_ref[halo:halo + tm, :] = (ag[:, :cw] * jax.nn.sigmoid(ag[:, cw:])).astype(pad_ref.dtype)

    def project_ret_gate():
        gate = proj(c_rg, retw)
        sg_s[...] = (gate * jax.nn.sigmoid(gate)).astype(BF16)

    def project_ret_q():
        rq_s[...] = rotary(proj(c_rq, retw)).astype(BF16)

    def project_ret_k():
        rk_s[...] = rotary(proj(c_rk, retw)).astype(BF16)

    def project_ret_v():
        rv_s[...] = proj(c_rv, retw).astype(BF16)

    def project_sb_qk():
        qk = proj(c_q, 2 * sbw)
        q_s[...] = (head_norm(qk[:, :sbw], gq_ref[...]) * (LOG2E * HEAD_DIM ** -0.5)).astype(BF16)
        k_all[pl.ds(sb_pad + t0, tm), :] = head_norm(qk[:, sbw:], gk_ref[...]).astype(BF16)

    def project_sb_v():
        v = proj(c_v, sbw).astype(BF16)
        for hh in range(n_sb_heads):
            vm_all[hh, pl.ds(sb_pad + t0, tm), :] = jnp.where(_head_lanes(v.shape, hh), v, jnp.zeros_like(v))

    n_shift = CONV_SHIFTS
    first = halo - (CONV_KERNEL - 1)

    def conv_block(r0):
        window = pad_ref[r0:r0 + CONV_WINDOW, :]
        acc = jnp.zeros((CONV_BLOCK, cw), F32) + db_ref[...]
        for shift in range(n_shift):
            if shift == 0:
                moved = window[:CONV_BLOCK + halo].astype(F32)
            else:
                moved = jnp.dot(shift_ref[shift - 1], window, preferred_element_type=F32)
            for tap in range(CONV_KERNEL):
                if (first + tap) % n_shift == shift:
                    a = (first + tap) // n_shift * n_shift
                    acc = acc + moved[a:a + CONV_BLOCK, :] * dw_ref[tap:tap + 1, :]
        mu = jnp.mean(acc, axis=-1, keepdims=True)
        xc = acc - mu
        var = jnp.mean(xc * xc, axis=-1, keepdims=True)
        yn = xc * lax.rsqrt(var + EPS) * lg_ref[...] + lb_ref[...]
        yconv_ref[0, r0:r0 + CONV_BLOCK, :] = (yn * jax.nn.sigmoid(yn)).astype(yconv_ref.dtype)

    ch = qdec_ref.shape[0]
    group = bd_ref.shape[0]
    heads_per_group = group // HEAD_DIM

    def per_head_rows(t):
        return jnp.concatenate(
            [jnp.where(_head_lanes(t.shape, hh), t, jnp.zeros_like(t)) for hh in range(heads_per_group)], axis=0)

    def retention_scores(c0, gi):
        cols = slice(gi * group, (gi + 1) * group)
        sc = lax.dot_general(rq_s[c0:c0 + ch, cols], per_head_rows(rk_s[c0:c0 + ch, cols]),
                             (((1,), (1,)), ((), ())), preferred_element_type=F32)
        return (sc * dmat_ref[gi]).astype(BF16)

    def retention_chunk(c0, gi, scores):
        cols = slice(gi * group, (gi + 1) * group)
        q = rq_s[c0:c0 + ch, cols]
        k = rk_s[c0:c0 + ch, cols]
        v = rv_s[c0:c0 + ch, cols]
        state = state_ref[gi]
        qd = (q.astype(F32) * qdec_ref[:, cols]).astype(BF16)
        out = (jnp.dot(qd, state.astype(BF16), preferred_element_type=F32)
               + jnp.dot(scores, per_head_rows(v), preferred_element_type=F32))
        kd = (k.astype(F32) * kdec_ref[:, cols]).astype(BF16)
        kv = lax.dot_general(kd, v, (((0,), (0,)), ((), ())), preferred_element_type=F32)
        state_ref[gi] = state * cdec_ref[:, cols] + kv * bd_ref[...]
        return out

    def retention_norm(c0, gi, out):
        cols = slice(gi * group, (gi + 1) * group)
        ss = jnp.dot((out * out).astype(BF16), ones_ref[...], preferred_element_type=F32)
        yn = out * lax.rsqrt(ss * (1.0 / HEAD_DIM) + EPS) * rg_ref[:, cols]
        yret_ref[0, c0:c0 + ch, cols] = (yn * sg_s[c0:c0 + ch, cols].astype(F32)).astype(yret_ref.dtype)

    sb_scores, sb_weights, sb_values, top_carry = _sb_stages(q_s, k_all, vm_all, sfx_ref, n_sb_heads)
    n_sub = tm // blk
    n_sweep = SB_LOOKBACK + 1
    sb_stage1, sb_stage2, sb_tops = {}, {}, []

    def sb_krow(s):
        return pl.multiple_of(sb_pad + t0 + (s - SB_LOOKBACK) * blk, blk)

    def sb_step(step):
        if step < n_sub:
            valid = [has_earlier_tile if step - SB_LOOKBACK + j < 0 else None for j in range(n_sweep)]
            sb_stage1[step] = sb_scores(step * blk, sb_krow(step), n_sweep, True, valid)
        s = step - 1
        if 0 <= s < n_sub:
            sb_stage2[s] = sb_weights(*sb_stage1.pop(s), n_sweep, [None] * n_sb_heads)
        s = step - 2
        if 0 <= s < n_sub:
            ws, carries = sb_stage2.pop(s)
            pv = sb_values(ws, sb_krow(s), n_sweep)
            acc_ref[s] = pv
            ysb_ref[0, s * blk:(s + 1) * blk, :] = pv.astype(ysb_ref.dtype)
            for hh in range(n_sb_heads):
                c_ref[s, hh] = jnp.broadcast_to(carries[hh], (blk, blk))
            sb_tops.append(top_carry(carries))

    ret_chunks = [(c0, gi) for c0 in range(0, tm, ch) for gi in range(retw // group)]
    project_conv()
    project_sb_qk()
    project_sb_v()
    conv_starts = list(range(0, tm, CONV_BLOCK))
    projections = [project_ret_q, project_ret_k, project_ret_v, project_ret_gate]
    sb_steps = list(range(n_sub + 2))
    for i in range(max(len(conv_starts), len(projections))):
        if i < len(conv_starts):
            conv_block(conv_starts[i])
        if i < len(projections):
            projections[i]()
        if sb_steps:
            sb_step(sb_steps.pop(0))
    pad_ref[0:halo, :] = pad_ref[tm:tm + halo, :]
    scores, outs = {}, {}
    for i in range(len(ret_chunks) + 2):
        if i < len(ret_chunks):
            scores[i] = retention_scores(*ret_chunks[i])
        if sb_steps:
            sb_step(sb_steps.pop(0))
        if 0 <= i - 1 < len(ret_chunks):
            outs[i - 1] = retention_chunk(*ret_chunks[i - 1], scores.pop(i - 1))
        if 0 <= i - 2 < len(ret_chunks):
            retention_norm(*ret_chunks[i - 2], outs.pop(i - 2))
    for step in sb_steps:
        sb_step(step)

    @pl.when(jnp.max(top_carry(sb_tops)) > SB_DEAD_LOG2)
    def _():
        def sub_block(s, _):
            off = pl.multiple_of(s * blk, blk)
            q0 = t0 + off

            def cond(state):
                i, live = state
                return jnp.logical_and(i * blk <= q0, live)

            def body(state):
                i, _ = state
                krow = pl.multiple_of(sb_pad + q0 - i * blk, blk)
                ws, carries = sb_weights(*sb_scores(off, krow, 1, False), 1,
                                         [c_ref[s, hh] for hh in range(n_sb_heads)])
                acc_ref[s] += sb_values(ws, krow, 1)
                for hh in range(n_sb_heads):
                    c_ref[s, hh] = carries[hh]
                return i + 1, jnp.max(top_carry(carries)) > SB_DEAD_LOG2

            live = jnp.max(top_carry([c_ref[s, hh] for hh in range(n_sb_heads)])) > SB_DEAD_LOG2
            lax.while_loop(cond, body, (jnp.int32(n_sweep), live))
            ysb_ref[0, pl.ds(off, blk), :] = acc_ref[s].astype(ysb_ref.dtype)
            return 0

        lax.fori_loop(0, n_sub, sub_block, 0)


def _mixers(layer, x, g, w, ones_bd, cos_t, sin_t, gq, gk, pb, shifts, dw_w, dw_b, ln_g, ln_b,
            qdec, kdec, dmat, cdec, bd, ret_g, sfx, widths):
    B, T, D = x.shape
    sbw, retw, cw = widths
    tm = min(TOKEN_TILE, T)
    blk = SB_BLOCK
    assert tm % qdec.shape[0] == 0 and tm % CONV_BLOCK == 0 and tm % blk == 0
    n_sb_heads = sbw // HEAD_DIM
    key_rows = SB_LOOKBACK * blk + T
    grid = (B, T // tm)
    tok = lambda width: pl.BlockSpec((1, tm, width), lambda b, t: (b, t, 0))
    tab = pl.BlockSpec((tm, cos_t.shape[1]), lambda b, t: (t, 0))
    per_layer = lambda a: _layer_spec(a.shape, layer)
    shared = lambda a: _const_spec(a.shape)
    out_widths = (sbw, retw, cw)
    vmem = (2 * tm * D * 4 + w[0].size * 2 + dmat.size * 4 + 4 * tm * retw * 2 + (1 + n_sb_heads) * key_rows * sbw * 2
            + 2 * sum(out_widths) * tm * 2 + 28 * 1024 * 1024)
    return pl.pallas_call(
        functools.partial(_mixers_kernel, widths=widths),
        grid=grid,
        in_specs=[tok(D), per_layer(g), per_layer(w), shared(ones_bd), tab, tab,
                  per_layer(gq), per_layer(gk), per_layer(pb), shared(shifts), per_layer(dw_w), per_layer(dw_b),
                  per_layer(ln_g), per_layer(ln_b), shared(qdec), shared(kdec), shared(dmat), shared(cdec),
                  shared(bd), per_layer(ret_g), shared(sfx)],
        out_specs=[tok(wd) for wd in out_widths],
        out_shape=[jax.ShapeDtypeStruct((B, T, wd), BF16) for wd in out_widths],
        scratch_shapes=[pltpu.VMEM((tm + CONV_WINDOW - CONV_BLOCK, cw), BF16),
                        pltpu.VMEM((retw // bd.shape[0], bd.shape[0], bd.shape[0]), F32)]
                       + [pltpu.VMEM((tm, retw), BF16)] * 4
                       + [pltpu.VMEM((tm, sbw), BF16), pltpu.VMEM((key_rows, sbw), BF16),
                          pltpu.VMEM((n_sb_heads, key_rows, sbw), BF16),
                          pltpu.VMEM((tm // blk, blk, sbw), F32),
                          pltpu.VMEM((tm // blk, n_sb_heads, blk, blk), F32)],
        compiler_params=pltpu.CompilerParams(
            dimension_semantics=("arbitrary", "arbitrary"), vmem_limit_bytes=_vmem_limit(vmem)),
        name="mixers",
    )(x, g, w, ones_bd, cos_t, sin_t, gq, gk, pb, shifts, dw_w, dw_b, ln_g, ln_b,
      qdec, kdec, dmat, cdec, bd, ret_g, sfx)


def _out_mlp_kernel(x_ref, ysb_ref, yret_ref, yconv_ref, wo_ref, g_ref, w1_ref, w2_ref, o_ref):
    mixed = jnp.concatenate([ysb_ref[0], yret_ref[0], yconv_ref[0]], axis=1)
    x1 = x_ref[0] + jnp.dot(mixed, wo_ref[...], preferred_element_type=F32)
    ms = jnp.mean(x1 * x1, axis=-1, keepdims=True)
    h = (x1 * lax.rsqrt(ms + EPS) * g_ref[...]).astype(BF16)
    acc = x1
    for c0 in range(0, w1_ref.shape[1], FF_CHUNK):
        f = jnp.maximum(jnp.dot(h, w1_ref[:, c0:c0 + FF_CHUNK], preferred_element_type=F32), 0.0)
        acc = acc + jnp.dot((f * f).astype(BF16), w2_ref[c0:c0 + FF_CHUNK, :], preferred_element_type=F32)
    o_ref[0] = acc


def _out_mlp(layer, x, ysb, yret, yconv, wo, g, w1, w2):
    B, T, D = x.shape
    tm = min(MLP_TILE, T)
    grid = (B, T // tm)
    tok = lambda width: pl.BlockSpec((1, tm, width), lambda b, t: (b, t, 0))
    dff = w1.shape[2]
    vmem = (D * D + 2 * D * dff) * 2 + 4 * tm * D * 4 + 4 * tm * D * 2 + tm * FF_CHUNK * 12 + 12 * 1024 * 1024
    return pl.pallas_call(
        _out_mlp_kernel,
        grid=grid,
        in_specs=[tok(D), tok(ysb.shape[2]), tok(yret.shape[2]), tok(yconv.shape[2])]
                 + [_layer_spec(a.shape, layer) for a in (wo, g, w1, w2)],
        out_specs=tok(D),
        out_shape=jax.ShapeDtypeStruct((B, T, D), F32),
        compiler_params=pltpu.CompilerParams(
            dimension_semantics=("arbitrary", "arbitrary"), vmem_limit_bytes=_vmem_limit(vmem)),
        name="out_mlp",
    )(x, ysb, yret, yconv, wo, g, w1, w2)


def _block_diag_ones(n, block):
    i = np.arange(n)
    return (i[:, None] // block == i[None, :] // block).astype(np.float32)


def _shift_matrices():
    r = np.arange(CONV_BLOCK + CONV_HALO)[None, :, None]
    j = np.arange(CONV_WINDOW)[None, None, :]
    b = np.arange(1, CONV_SHIFTS)[:, None, None]
    return (j == r + b).astype(np.float32)


def _suffix_sum_matrix(blk):
    j = np.arange(2 * blk)[:, None]
    s = np.arange(2 * blk)[None, :]
    return -((j // blk == s // blk) & (j % blk > s % blk)).astype(np.float32)


def _rotary_tables(T):
    inv = 1.0 / (ROPE_BASE ** jnp.linspace(0.0, 1.0, HEAD_DIM // 2, dtype=F32))
    ang = jnp.arange(T).astype(F32)[:, None] * inv[None, :]
    cos = jnp.repeat(jnp.cos(ang), 2, axis=1)
    sin = jnp.repeat(jnp.sin(ang), 2, axis=1)
    sign = jnp.where(jnp.arange(HEAD_DIM) % 2 == 0, -1.0, 1.0).astype(F32)
    reps = (1, V7X_LANES // HEAD_DIM)
    return jnp.tile(cos, reps), jnp.tile(sin * sign[None, :], reps)


def _retention_tables(n_heads, ch):
    log_g = jnp.log(1.0 - jnp.exp2(-5.0 - jnp.arange(n_heads, dtype=F32)))
    j = jnp.arange(ch, dtype=F32)
    per_lane = lambda t: jnp.repeat(t, HEAD_DIM, axis=-1)
    qdec = per_lane(jnp.exp(log_g[None, :] * (j + 1.0)[:, None]))
    kdec = per_lane(jnp.exp(log_g[None, :] * (ch - 1.0 - j)[:, None]))
    cdec = per_lane(jnp.exp(log_g * ch)[None, :])
    rel = j[:, None] - j[None, :]
    dmat = jnp.where(rel >= 0, jnp.exp(log_g[:, None, None] * jnp.maximum(rel, 0.0)), 0.0)
    per_group = V7X_MXU_DIM // HEAD_DIM
    dmat = dmat.reshape(n_heads // per_group, per_group, ch, ch).transpose(0, 2, 1, 3)
    return qdec, kdec, dmat.reshape(n_heads // per_group, ch, per_group * ch), cdec


def kernel(x, mix_norm_g, w_in, sb_q_norm_g, sb_k_norm_g, ret_norm_g, conv_pw_b, conv_dw_w, conv_dw_b,
           conv_ln_g, conv_ln_b, w_out, mlp_norm_g, w_ff1, w_ff2):
    B, T, D = x.shape
    depth = w_in.shape[0]
    sbw = (D // 256) * HEAD_DIM
    retw = (D // 128) * HEAD_DIM
    cw = D - sbw - retw
    assert all(T % tile == 0 or T < tile for tile in (TOKEN_TILE, MLP_TILE))
    assert T % RET_CHUNK == 0 and T % SB_BLOCK == 0
    widths = (sbw, retw, cw)
    n_ret_heads = retw // HEAD_DIM

    ones_bd = jnp.asarray(_block_diag_ones(V7X_MXU_DIM, HEAD_DIM), BF16)
    bd_mask = jnp.asarray(_block_diag_ones(V7X_MXU_DIM, HEAD_DIM), F32)
    sfx = jnp.asarray(_suffix_sum_matrix(SB_BLOCK), BF16)
    shifts = jnp.asarray(_shift_matrices(), BF16)
    cos_t, sin_t = _rotary_tables(T)
    qdec, kdec, dmat, cdec = _retention_tables(n_ret_heads, RET_CHUNK)
    rk0 = 3 * sbw + retw
    col_scale = jnp.ones((w_in.shape[2],), F32).at[rk0:rk0 + retw].set(HEAD_DIM ** -0.5)

    rows = lambda a: a.reshape(depth, 1, -1).astype(F32)
    w_in_b = (w_in * col_scale[None, None, :]).astype(BF16)
    w_out_b, w_ff1_b, w_ff2_b = (w.astype(BF16) for w in (w_out, w_ff1, w_ff2))
    mix_g, mlp_g, ret_g, pw_b = rows(mix_norm_g), rows(mlp_norm_g), rows(ret_norm_g), rows(conv_pw_b)
    gq, gk = (rows(jnp.tile(g, (1, sbw // HEAD_DIM))) for g in (sb_q_norm_g, sb_k_norm_g))
    dw_w, dw_b, ln_g, ln_b = conv_dw_w.astype(F32), rows(conv_dw_b), rows(conv_ln_g), rows(conv_ln_b)
    for l in range(depth):
        y_sb, y_ret, y_conv = _mixers(
            l, x, mix_g, w_in_b, ones_bd, cos_t, sin_t, gq, gk, pw_b, shifts, dw_w, dw_b, ln_g, ln_b,
            qdec, kdec, dmat, cdec, bd_mask, ret_g, sfx, widths)
        x = _out_mlp(l, x, y_sb, y_ret, y_conv, w_out_b, mlp_g, w_ff1_b, w_ff2_b)
    return x
```

```python
import functools

import jax
import jax.numpy as jnp
import numpy as np
from jax import lax
from jax.experimental import pallas as pl
from jax.experimental.pallas import tpu as pltpu

F32 = jnp.float32
BF16 = jnp.bfloat16

HEAD_DIM = 64
CONV_KERNEL = 31
EPS = 1e-6
ROPE_BASE = 10000.0
LOG2E = 1.4426950408889634
LN2 = 0.6931471805599453

V7X_LANES = 128
V7X_MXU_DIM = 256
V7X_VMEM_BYTES = 64 * 1024 * 1024

SB_DEAD_LOG2 = -150.0
SB_MASKED_SCORE = -1e30

TOKEN_TILE = 512
MLP_TILE = 1024
SB_BLOCK = 128
SB_LOOKBACK = 2
RET_CHUNK = 128
CONV_HALO = 32
CONV_BLOCK = 128
CONV_WINDOW = 256
CONV_SHIFTS = 8
FF_CHUNK = 1024


def _vmem_limit(nbytes):
    return int(min(nbytes, V7X_VMEM_BYTES - 4 * 1024 * 1024))


def _const_spec(shape):
    nd = len(shape)
    return pl.BlockSpec(shape, lambda *_: (0,) * nd, pipeline_mode=pl.Buffered(1))


def _layer_spec(shape, layer):
    nd = len(shape)
    return pl.BlockSpec((None,) + tuple(shape[1:]), lambda *_: (layer,) + (0,) * (nd - 1),
                        pipeline_mode=pl.Buffered(1))


def _head_lanes(shape, h):
    lane = lax.broadcasted_iota(jnp.int32, shape, 1)
    return (lane >= h * HEAD_DIM) & (lane < (h + 1) * HEAD_DIM)


def _sb_stages(q_ref, k_ref, vm_ref, m_ref, n_heads):
    blk = SB_BLOCK
    heads = tuple(range(n_heads))
    causal = (lax.broadcasted_iota(jnp.int32, (blk, blk), 1) < lax.broadcasted_iota(jnp.int32, (blk, blk), 0))

    def masked_q(off):
        q = q_ref[pl.ds(off, blk), :]
        return jnp.concatenate([jnp.where(_head_lanes(q.shape, h), q, jnp.zeros_like(q)) for h in heads], axis=0)

    def scores(off, krow, nblk, diag_last, valid=None):
        kwin = k_ref[pl.ds(krow, nblk * blk), :]
        z_all = lax.dot_general(masked_q(off), kwin, (((1,), (1,)), ((), ())), preferred_element_type=F32)
        log_beta, split = {}, []
        for h in heads:
            for j in range(nblk):
                z = z_all[h * blk:(h + 1) * blk, j * blk:(j + 1) * blk]
                if diag_last and j == nblk - 1:
                    z = jnp.where(causal, z, SB_MASKED_SCORE)
                elif valid is not None and valid[j] is not None:
                    z = jnp.where(valid[j], z, SB_MASKED_SCORE)
                sp = jnp.maximum(z, 0.0) + jnp.log(1.0 + jnp.exp2(-jnp.abs(z))) * (1.0 / LN2)
                log_beta[h, j] = z - sp
                hi = sp.astype(BF16)
                lo = (sp - hi.astype(F32)).astype(BF16)
                split.append(jnp.concatenate([hi, lo], axis=1))
        return jnp.concatenate(split, axis=0), log_beta

    def weights(split, log_beta, nblk, carries):
        r = jnp.dot(split, m_ref[...], preferred_element_type=F32)
        ws, out_carries = [], []
        for h in heads:
            carry = carries[h]
            w = [None] * nblk
            for j in reversed(range(nblk)):
                rr = r[(h * nblk + j) * blk:(h * nblk + j + 1) * blk, :]
                arg = rr[:, :blk] + log_beta[h, j]
                if carry is not None:
                    arg = arg + carry
                w[j] = jnp.exp2(arg).astype(BF16)
                carry = rr[:, blk:] if carry is None else carry + rr[:, blk:]
            out_carries.append(carry)
            ws.extend(w)
        return jnp.concatenate(ws, axis=1), out_carries

    def values(ws, krow, nblk):
        vs = [vm_ref[h, pl.ds(krow, nblk * blk), :] for h in heads]
        return jnp.dot(ws, jnp.concatenate(vs, axis=0), preferred_element_type=F32)

    def top_carry(carries):
        top = carries[0]
        for c in carries[1:]:
            top = jnp.maximum(top, c)
        return top

    return scores, weights, values, top_carry


def _mixers_kernel(x_ref, g_ref, w_ref, ones_ref, cos_ref, sin_ref, gq_ref, gk_ref, pb_ref,
                   shift_ref, dw_ref, db_ref, lg_ref, lb_ref,
                   qdec_ref, kdec_ref, dmat_ref, cdec_ref, bd_ref, rg_ref, sfx_ref,
                   ysb_ref, yret_ref, yconv_ref,
                   pad_ref, state_ref, rq_s, rk_s, rv_s, sg_s, q_s, k_all, vm_all, acc_ref, c_ref, *, widths):
    sbw, retw, cw = widths
    tm = x_ref.shape[1]
    halo = CONV_HALO
    blk = SB_BLOCK
    sb_pad = SB_LOOKBACK * blk
    n_sb_heads = sbw // HEAD_DIM
    t0 = pl.multiple_of(pl.program_id(1) * tm, tm)
    has_earlier_tile = pl.program_id(1) > 0

    @pl.when(pl.program_id(1) == 0)
    def _():
        state_ref[...] = jnp.zeros_like(state_ref)
        pad_ref[0:halo, :] = jnp.zeros((halo, cw), pad_ref.dtype)
        pad_ref[halo + tm:, :] = jnp.zeros((pad_ref.shape[0] - halo - tm, cw), pad_ref.dtype)
        k_all[0:sb_pad, :] = jnp.zeros((sb_pad, sbw), k_all.dtype)
        vm_all[:, 0:sb_pad, :] = jnp.zeros((n_sb_heads, sb_pad, sbw), vm_all.dtype)

    x = x_ref[0]
    ms = jnp.mean(x * x, axis=-1, keepdims=True)
    h = (x * lax.rsqrt(ms + EPS) * g_ref[...]).astype(BF16)

    def proj(c0, width):
        return jnp.dot(h, w_ref[:, c0:c0 + width], preferred_element_type=F32)

    def head_norm(u, gain):
        ss = jnp.dot((u * u).astype(BF16), ones_ref[...], preferred_element_type=F32)
        return u * lax.rsqrt(ss * (1.0 / HEAD_DIM) + EPS) * gain

    def rotary(u):
        n = cos_ref.shape[1]
        cos, sin = cos_ref[...], sin_ref[...]
        even = (lax.broadcasted_iota(jnp.int32, cos.shape, 1) & 1) == 0
        out = []
        for c0 in range(0, u.shape[1], n):
            uc = u[:, c0:c0 + n]
            nxt = pltpu.roll(uc, n - 1, axis=1)
            prv = pltpu.roll(uc, 1, axis=1)
            out.append(uc * cos + jnp.where(even, nxt, prv) * sin)
        return jnp.concatenate(out, axis=1)

    c_q, c_v = 0, 2 * sbw
    c_rq, c_rk, c_rv, c_rg = (3 * sbw + i * retw for i in range(4))
    c_a = 3 * sbw + 4 * retw

    def project_conv():
        ag = proj(c_a, 2 * cw) + pb_ref[...]
        pad_ref[halo:halo + tm, :] = (ag[:, :cw] * jax.nn.sigmoid(ag[:, cw:])).astype(pad_ref.dtype)

    def project_ret_gate():
        gate = proj(c_rg, retw)
        sg_s[...] = (gate * jax.nn.sigmoid(gate)).astype(BF16)

    def project_ret_q():
        rq_s[...] = rotary(proj(c_rq, retw)).astype(BF16)

    def project_ret_k():
        rk_s[...] = rotary(proj(c_rk, retw)).astype(BF16)

    def project_ret_v():
        rv_s[...] = proj(c_rv, retw).astype(BF16)

    def project_sb_qk():
        qk = proj(c_q, 2 * sbw)
        q_s[...] = (head_norm(qk[:, :sbw], gq_ref[...]) * (LOG2E * HEAD_DIM ** -0.5)).astype(BF16)
        k_all[pl.ds(sb_pad + t0, tm), :] = head_norm(qk[:, sbw:], gk_ref[...]).astype(BF16)

    def project_sb_v():
        v = proj(c_v, sbw).astype(BF16)
        for hh in range(n_sb_heads):
            vm_all[hh, pl.ds(sb_pad + t0, tm), :] = jnp.where(_head_lanes(v.shape, hh), v, jnp.zeros_like(v))

    n_shift = CONV_SHIFTS
    first = halo - (CONV_KERNEL - 1)

    def conv_block(r0):
        window = pad_ref[r0:r0 + CONV_WINDOW, :]
        acc = jnp.zeros((CONV_BLOCK, cw), F32) + db_ref[...]
        for shift in range(n_shift):
            if shift == 0:
                moved = window[:CONV_BLOCK + halo].astype(F32)
            else:
                moved = jnp.dot(shift_ref[shift - 1], window, preferred_element_type=F32)
            for tap in range(CONV_KERNEL):
                if (first + tap) % n_shift == shift:
                    a = (first + tap) // n_shift * n_shift
                    acc = acc + moved[a:a + CONV_BLOCK, :] * dw_ref[tap:tap + 1, :]
        mu = jnp.mean(acc, axis=-1, keepdims=True)
        xc = acc - mu
        var = jnp.mean(xc * xc, axis=-1, keepdims=True)
        yn = xc * lax.rsqrt(var + EPS) * lg_ref[...] + lb_ref[...]
        yconv_ref[0, r0:r0 + CONV_BLOCK, :] = (yn * jax.nn.sigmoid(yn)).astype(yconv_ref.dtype)

    ch = qdec_ref.shape[0]
    group = bd_ref.shape[0]
    heads_per_group = group // HEAD_DIM

    def per_head_rows(t):
        return jnp.concatenate(
            [jnp.where(_head_lanes(t.shape, hh), t, jnp.zeros_like(t)) for hh in range(heads_per_group)], axis=0)

    def retention_scores(c0, gi):
        cols = slice(gi * group, (gi + 1) * group)
        sc = lax.dot_general(rq_s[c0:c0 + ch, cols], per_head_rows(rk_s[c0:c0 + ch, cols]),
                             (((1,), (1,)), ((), ())), preferred_element_type=F32)
        return (sc * dmat_ref[gi]).astype(BF16)

    def retention_chunk(c0, gi, scores):
        cols = slice(gi * group, (gi + 1) * group)
        q = rq_s[c0:c0 + ch, cols]
        k = rk_s[c0:c0 + ch, cols]
        v = rv_s[c0:c0 + ch, cols]
        state = state_ref[gi]
        qd = (q.astype(F32) * qdec_ref[:, cols]).astype(BF16)
        out = (jnp.dot(qd, state.astype(BF16), preferred_element_type=F32)
               + jnp.dot(scores, per_head_rows(v), preferred_element_type=F32))
        kd = (k.astype(F32) * kdec_ref[:, cols]).astype(BF16)
        kv = lax.dot_general(kd, v, (((0,), (0,)), ((), ())), preferred_element_type=F32)
        state_ref[gi] = state * cdec_ref[:, cols] + kv * bd_ref[...]
        return out

    def retention_norm(c0, gi, out):
        cols = slice(gi * group, (gi + 1) * group)
        ss = jnp.dot((out * out).astype(BF16), ones_ref[...], preferred_element_type=F32)
        yn = out * lax.rsqrt(ss * (1.0 / HEAD_DIM) + EPS) * rg_ref[:, cols]
        yret_ref[0, c0:c0 + ch, cols] = (yn * sg_s[c0:c0 + ch, cols].astype(F32)).astype(yret_ref.dtype)

    sb_scores, sb_weights, sb_values, top_carry = _sb_stages(q_s, k_all, vm_all, sfx_ref, n_sb_heads)
    n_sub = tm // blk
    n_sweep = SB_LOOKBACK + 1
    sb_stage1, sb_stage2, sb_tops = {}, {}, []

    def sb_krow(s):
        return pl.multiple_of(sb_pad + t0 + (s - SB_LOOKBACK) * blk, blk)

    def sb_step(step):
        if step < n_sub:
            valid = [has_earlier_tile if step - SB_LOOKBACK + j < 0 else None for j in range(n_sweep)]
            sb_stage1[step] = sb_scores(step * blk, sb_krow(step), n_sweep, True, valid)
        s = step - 1
        if 0 <= s < n_sub:
            sb_stage2[s] = sb_weights(*sb_stage1.pop(s), n_sweep, [None] * n_sb_heads)
        s = step - 2
        if 0 <= s < n_sub:
            ws, carries = sb_stage2.pop(s)
            pv = sb_values(ws, sb_krow(s), n_sweep)
            acc_ref[s] = pv
            ysb_ref[0, s * blk:(s + 1) * blk, :] = pv.astype(ysb_ref.dtype)
            for hh in range(n_sb_heads):
                c_ref[s, hh] = carries[hh]
            sb_tops.append(top_carry(carries))

    ret_chunks = [(c0, gi) for c0 in range(0, tm, ch) for gi in range(retw // group)]
    project_conv()
    project_sb_qk()
    project_sb_v()
    conv_starts = list(range(0, tm, CONV_BLOCK))
    projections = [project_ret_q, project_ret_k, project_ret_v, project_ret_gate]
    sb_steps = list(range(n_sub + 2))
    for i in range(max(len(conv_starts), len(projections))):
        if i < len(conv_starts):
            conv_block(conv_starts[i])
        if i < len(projections):
            projections[i]()
    pad_ref[0:halo, :] = pad_ref[tm:tm + halo, :]
    scores, outs = {}, {}
    for i in range(len(ret_chunks) + 2):
        if i < len(ret_chunks):
            scores[i] = retention_scores(*ret_chunks[i])
        if sb_steps:
            sb_step(sb_steps.pop(0))
        if 0 <= i - 1 < len(ret_chunks):
            outs[i - 1] = retention_chunk(*ret_chunks[i - 1], scores.pop(i - 1))
        if 0 <= i - 2 < len(ret_chunks):
            retention_norm(*ret_chunks[i - 2], outs.pop(i - 2))
    for step in sb_steps:
        sb_step(step)

    @pl.when(jnp.max(top_carry(sb_tops)) > SB_DEAD_LOG2)
    def _():
        def sub_block(s, _):
            off = pl.multiple_of(s * blk, blk)
            q0 = t0 + off

            def cond(state):
                i, live = state
                return jnp.logical_and(i * blk <= q0, live)

            def body(state):
                i, _ = state
                krow = pl.multiple_of(sb_pad + q0 - i * blk, blk)
                ws, carries = sb_weights(*sb_scores(off, krow, 1, False), 1,
                                         [c_ref[s, hh] for hh in range(n_sb_heads)])
                acc_ref[s] += sb_values(ws, krow, 1)
                for hh in range(n_sb_heads):
                    c_ref[s, hh] = carries[hh]
                return i + 1, jnp.max(top_carry(carries)) > SB_DEAD_LOG2

            live = jnp.max(top_carry([c_ref[s, hh] for hh in range(n_sb_heads)])) > SB_DEAD_LOG2
            lax.while_loop(cond, body, (jnp.int32(n_sweep), live))
            ysb_ref[0, pl.ds(off, blk), :] = acc_ref[s].astype(ysb_ref.dtype)
            return 0

        lax.fori_loop(0, n_sub, sub_block, 0)


def _mixers(layer, x, g, w, ones_bd, cos_t, sin_t, gq, gk, pb, shifts, dw_w, dw_b, ln_g, ln_b,
            qdec, kdec, dmat, cdec, bd, ret_g, sfx, widths):
    B, T, D = x.shape
    sbw, retw, cw = widths
    tm = min(TOKEN_TILE, T)
    blk = SB_BLOCK
    assert tm % qdec.shape[0] == 0 and tm % CONV_BLOCK == 0 and tm % blk == 0
    n_sb_heads = sbw // HEAD_DIM
    key_rows = SB_LOOKBACK * blk + T
    grid = (B, T // tm)
    tok = lambda width: pl.BlockSpec((1, tm, width), lambda b, t: (b, t, 0))
    tab = pl.BlockSpec((tm, cos_t.shape[1]), lambda b, t: (t, 0))
    per_layer = lambda a: _layer_spec(a.shape, layer)
    shared = lambda a: _const_spec(a.shape)
    out_widths = (sbw, retw, cw)
    vmem = (2 * tm * D * 4 + w[0].size * 2 + dmat.size * 4 + 4 * tm * retw * 2 + (1 + n_sb_heads) * key_rows * sbw * 2
            + 2 * sum(out_widths) * tm * 2 + 28 * 1024 * 1024)
    return pl.pallas_call(
        functools.partial(_mixers_kernel, widths=widths),
        grid=grid,
        in_specs=[tok(D), per_layer(g), per_layer(w), shared(ones_bd), tab, tab,
                  per_layer(gq), per_layer(gk), per_layer(pb), shared(shifts), per_layer(dw_w), per_layer(dw_b),
                  per_layer(ln_g), per_layer(ln_b), shared(qdec), shared(kdec), shared(dmat), shared(cdec),
                  shared(bd), per_layer(ret_g), shared(sfx)],
        out_specs=[tok(wd) for wd in out_widths],
        out_shape=[jax.ShapeDtypeStruct((B, T, wd), BF16) for wd in out_widths],
        scratch_shapes=[pltpu.VMEM((tm + CONV_WINDOW - CONV_BLOCK, cw), BF16),
                        pltpu.VMEM((retw // bd.shape[0], bd.shape[0], bd.shape[0]), F32)]
                       + [pltpu.VMEM((tm, retw), BF16)] * 4
                       + [pltpu.VMEM((tm, sbw), BF16), pltpu.VMEM((key_rows, sbw), BF16),
                          pltpu.VMEM((n_sb_heads, key_rows, sbw), BF16),
                          pltpu.VMEM((tm // blk, blk, sbw), F32),
                          pltpu.VMEM((tm // blk, n_sb_heads, blk, blk), F32)],
        compiler_params=pltpu.CompilerParams(
            dimension_semantics=("arbitrary", "arbitrary"), vmem_limit_bytes=_vmem_limit(vmem)),
        name="mixers",
    )(x, g, w, ones_bd, cos_t, sin_t, gq, gk, pb, shifts, dw_w, dw_b, ln_g, ln_b,
      qdec, kdec, dmat, cdec, bd, ret_g, sfx)


def _out_mlp_kernel(x_ref, ysb_ref, yret_ref, yconv_ref, wo_ref, g_ref, w1_ref, w2_ref, o_ref):
    mixed = jnp.concatenate([ysb_ref[0], yret_ref[0], yconv_ref[0]], axis=1)
    x1 = x_ref[0] + jnp.dot(mixed, wo_ref[...], preferred_element_type=F32)
    ms = jnp.mean(x1 * x1, axis=-1, keepdims=True)
    h = (x1 * lax.rsqrt(ms + EPS) * g_ref[...]).astype(BF16)
    acc = x1
    for c0 in range(0, w1_ref.shape[1], FF_CHUNK):
        f = jnp.maximum(jnp.dot(h, w1_ref[:, c0:c0 + FF_CHUNK], preferred_element_type=F32), 0.0)
        acc = acc + jnp.dot((f * f).astype(BF16), w2_ref[c0:c0 + FF_CHUNK, :], preferred_element_type=F32)
    o_ref[0] = acc


def _out_mlp(layer, x, ysb, yret, yconv, wo, g, w1, w2):
    B, T, D = x.shape
    tm = min(MLP_TILE, T)
    grid = (B, T // tm)
    tok = lambda width: pl.BlockSpec((1, tm, width), lambda b, t: (b, t, 0))
    dff = w1.shape[2]
    vmem = (D * D + 2 * D * dff) * 2 + 4 * tm * D * 4 + 4 * tm * D * 2 + tm * FF_CHUNK * 12 + 12 * 1024 * 1024
    return pl.pallas_call(
        _out_mlp_kernel,
        grid=grid,
        in_specs=[tok(D), tok(ysb.shape[2]), tok(yret.shape[2]), tok(yconv.shape[2])]
                 + [_layer_spec(a.shape, layer) for a in (wo, g, w1, w2)],
        out_specs=tok(D),
        out_shape=jax.ShapeDtypeStruct((B, T, D), F32),
        compiler_params=pltpu.CompilerParams(
            dimension_semantics=("arbitrary", "arbitrary"), vmem_limit_bytes=_vmem_limit(vmem)),
        name="out_mlp",
    )(x, ysb, yret, yconv, wo, g, w1, w2)


def _block_diag_ones(n, block):
    i = np.arange(n)
    return (i[:, None] // block == i[None, :] // block).astype(np.float32)


def _shift_matrices():
    r = np.arange(CONV_BLOCK + CONV_HALO)[None, :, None]
    j = np.arange(CONV_WINDOW)[None, None, :]
    b = np.arange(1, CONV_SHIFTS)[:, None, None]
    return (j == r + b).astype(np.float32)


def _suffix_sum_matrix(blk):
    j = np.arange(2 * blk)[:, None] % blk
    s = np.arange(2 * blk)[None, :]
    return -((s >= blk) | (j > s)).astype(np.float32)


def _rotary_tables(T):
    inv = 1.0 / (ROPE_BASE ** jnp.linspace(0.0, 1.0, HEAD_DIM // 2, dtype=F32))
    ang = jnp.arange(T).astype(F32)[:, None] * inv[None, :]
    cos = jnp.repeat(jnp.cos(ang), 2, axis=1)
    sin = jnp.repeat(jnp.sin(ang), 2, axis=1)
    sign = jnp.where(jnp.arange(HEAD_DIM) % 2 == 0, -1.0, 1.0).astype(F32)
    reps = (1, V7X_LANES // HEAD_DIM)
    return jnp.tile(cos, reps), jnp.tile(sin * sign[None, :], reps)


def _retention_tables(n_heads, ch):
    log_g = jnp.log(1.0 - jnp.exp2(-5.0 - jnp.arange(n_heads, dtype=F32)))
    j = jnp.arange(ch, dtype=F32)
    per_lane = lambda t: jnp.repeat(t, HEAD_DIM, axis=-1)
    qdec = per_lane(jnp.exp(log_g[None, :] * (j + 1.0)[:, None]))
    kdec = per_lane(jnp.exp(log_g[None, :] * (ch - 1.0 - j)[:, None]))
    cdec = per_lane(jnp.exp(log_g * ch)[None, :])
    rel = j[:, None] - j[None, :]
    dmat = jnp.where(rel >= 0, jnp.exp(log_g[:, None, None] * jnp.maximum(rel, 0.0)), 0.0)
    per_group = V7X_MXU_DIM // HEAD_DIM
    dmat = dmat.reshape(n_heads // per_group, per_group, ch, ch).transpose(0, 2, 1, 3)
    return qdec, kdec, dmat.reshape(n_heads // per_group, ch, per_group * ch), cdec


def kernel(x, mix_norm_g, w_in, sb_q_norm_g, sb_k_norm_g, ret_norm_g, conv_pw_b, conv_dw_w, conv_dw_b,
           conv_ln_g, conv_ln_b, w_out, mlp_norm_g, w_ff1, w_ff2):
    B, T, D = x.shape
    depth = w_in.shape[0]
    sbw = (D // 256) * HEAD_DIM
    retw = (D // 128) * HEAD_DIM
    cw = D - sbw - retw
    assert all(T % tile == 0 or T < tile for tile in (TOKEN_TILE, MLP_TILE))
    assert T % RET_CHUNK == 0 and T % SB_BLOCK == 0
    widths = (sbw, retw, cw)
    n_ret_heads = retw // HEAD_DIM

    ones_bd = jnp.asarray(_block_diag_ones(V7X_MXU_DIM, HEAD_DIM), BF16)
    bd_mask = jnp.asarray(_block_diag_ones(V7X_MXU_DIM, HEAD_DIM), F32)
    sfx = jnp.asarray(_suffix_sum_matrix(SB_BLOCK), BF16)
    shifts = jnp.asarray(_shift_matrices(), BF16)
    cos_t, sin_t = _rotary_tables(T)
    qdec, kdec, dmat, cdec = _retention_tables(n_ret_heads, RET_CHUNK)
    rk0 = 3 * sbw + retw
    col_scale = jnp.ones((w_in.shape[2],), F32).at[rk0:rk0 + retw].set(HEAD_DIM ** -0.5)

    rows = lambda a: a.reshape(depth, 1, -1).astype(F32)
    w_in_b = (w_in * col_scale[None, None, :]).astype(BF16)
    w_out_b, w_ff1_b, w_ff2_b = (w.astype(BF16) for w in (w_out, w_ff1, w_ff2))
    mix_g, mlp_g, ret_g, pw_b = rows(mix_norm_g), rows(mlp_norm_g), rows(ret_norm_g), rows(conv_pw_b)
    gq, gk = (rows(jnp.tile(g, (1, sbw // HEAD_DIM))) for g in (sb_q_norm_g, sb_k_norm_g))
    dw_w, dw_b, ln_g, ln_b = conv_dw_w.astype(F32), rows(conv_dw_b), rows(conv_ln_g), rows(conv_ln_b)
    for l in range(depth):
        y_sb, y_ret, y_conv = _mixers(
            l, x, mix_g, w_in_b, ones_bd, cos_t, sin_t, gq, gk, pw_b, shifts, dw_w, dw_b, ln_g, ln_b,
            qdec, kdec, dmat, cdec, bd_mask, ret_g, sfx, widths)
        x = _out_mlp(l, x, y_sb, y_ret, y_conv, w_out_b, mlp_g, w_ff1_b, w_ff2_b)
    return x
```

```python
import functools

import jax
import jax.numpy as jnp
import numpy as np
from jax import lax
from jax.experimental import pallas as pl
from jax.experimental.pallas import tpu as pltpu

F32 = jnp.float32
BF16 = jnp.bfloat16

HEAD_DIM = 64
CONV_KERNEL = 31
EPS = 1e-6
ROPE_BASE = 10000.0
LOG2E = 1.4426950408889634
LN2 = 0.6931471805599453

V7X_LANES = 128
V7X_MXU_DIM = 256
V7X_VMEM_BYTES = 64 * 1024 * 1024

SB_DEAD_LOG2 = -150.0
SB_MASKED_SCORE = -1e30

TOKEN_TILE = 1024
MLP_TILE = 1024
SB_BLOCK = 128
SB_LOOKBACK = 2
RET_CHUNK = 128
CONV_HALO = 32
CONV_BLOCK = 128
CONV_WINDOW = 256
CONV_SHIFTS = 8
FF_CHUNK = 1024


def _vmem_limit(nbytes):
    return int(min(nbytes, V7X_VMEM_BYTES - 4 * 1024 * 1024))


def _const_spec(shape):
    nd = len(shape)
    return pl.BlockSpec(shape, lambda *_: (0,) * nd, pipeline_mode=pl.Buffered(1))


def _layer_spec(shape, layer):
    nd = len(shape)
    return pl.BlockSpec((None,) + tuple(shape[1:]), lambda *_: (layer,) + (0,) * (nd - 1),
                        pipeline_mode=pl.Buffered(1))


def _head_lanes(shape, h):
    lane = lax.broadcasted_iota(jnp.int32, shape, 1)
    return (lane >= h * HEAD_DIM) & (lane < (h + 1) * HEAD_DIM)


def _sb_stages(q_ref, k_ref, vm_ref, m_ref, n_heads):
    blk = SB_BLOCK
    heads = tuple(range(n_heads))
    causal = (lax.broadcasted_iota(jnp.int32, (blk, blk), 1) < lax.broadcasted_iota(jnp.int32, (blk, blk), 0))

    def masked_q(off):
        q = q_ref[pl.ds(off, blk), :]
        return jnp.concatenate([jnp.where(_head_lanes(q.shape, h), q, jnp.zeros_like(q)) for h in heads], axis=0)

    def scores(off, krow, nblk, diag_last, valid=None):
        kwin = k_ref[pl.ds(krow, nblk * blk), :]
        z_all = lax.dot_general(masked_q(off), kwin, (((1,), (1,)), ((), ())), preferred_element_type=F32)
        log_beta, split = {}, []
        for h in heads:
            for j in range(nblk):
                z = z_all[h * blk:(h + 1) * blk, j * blk:(j + 1) * blk]
                if diag_last and j == nblk - 1:
                    z = jnp.where(causal, z, SB_MASKED_SCORE)
                elif valid is not None and valid[j] is not None:
                    z = jnp.where(valid[j], z, SB_MASKED_SCORE)
                sp = jnp.maximum(z, 0.0) + jnp.log(1.0 + jnp.exp2(-jnp.abs(z))) * (1.0 / LN2)
                log_beta[h, j] = z - sp
                hi = sp.astype(BF16)
                lo = (sp - hi.astype(F32)).astype(BF16)
                split.append(jnp.concatenate([hi, lo], axis=1))
        return jnp.concatenate(split, axis=0), log_beta

    def weights(split, log_beta, nblk, carries):
        r = jnp.dot(split, m_ref[...], preferred_element_type=F32)
        ws, out_carries = [], []
        for h in heads:
            carry = carries[h]
            w = [None] * nblk
            for j in reversed(range(nblk)):
                rr = r[(h * nblk + j) * blk:(h * nblk + j + 1) * blk, :]
                arg = rr[:, :blk] + log_beta[h, j]
                if carry is not None:
                    arg = arg + carry
                w[j] = jnp.exp2(arg).astype(BF16)
                carry = rr[:, blk:] if carry is None else carry + rr[:, blk:]
            out_carries.append(carry)
            ws.extend(w)
        return jnp.concatenate(ws, axis=1), out_carries

    def values(ws, krow, nblk):
        vs = [vm_ref[h, pl.ds(krow, nblk * blk), :] for h in heads]
        return jnp.dot(ws, jnp.concatenate(vs, axis=0), preferred_element_type=F32)

    def top_carry(carries):
        top = carries[0]
        for c in carries[1:]:
            top = jnp.maximum(top, c)
        return top

    return scores, weights, values, top_carry


def _mixers_kernel(x_ref, g_ref, w_ref, ones_ref, cos_ref, sin_ref, gq_ref, gk_ref, pb_ref,
                   shift_ref, dw_ref, db_ref, lg_ref, lb_ref,
                   qdec_ref, kdec_ref, dmat_ref, cdec_ref, bd_ref, rg_ref, sfx_ref, wo_f_ref, w1_f_ref, w2_f_ref,
                   ysb_ref, yret_ref, yconv_ref, wo_b_ref, w1_b_ref, w2_b_ref,
                   pad_ref, state_ref, rq_s, rk_s, rv_s, sg_s, q_s, k_all, vm_all, acc_ref, c_ref, *, widths):
    for f_ref, b_ref in ((wo_f_ref, wo_b_ref), (w1_f_ref, w1_b_ref), (w2_f_ref, w2_b_ref)):
        b_ref[...] = f_ref[...].astype(b_ref.dtype)
    sbw, retw, cw = widths
    tm = x_ref.shape[1]
    halo = CONV_HALO
    blk = SB_BLOCK
    sb_pad = SB_LOOKBACK * blk
    n_sb_heads = sbw // HEAD_DIM
    t0 = pl.multiple_of(pl.program_id(1) * tm, tm)
    has_earlier_tile = pl.program_id(1) > 0

    @pl.when(pl.program_id(1) == 0)
    def _():
        state_ref[...] = jnp.zeros_like(state_ref)
        pad_ref[0:halo, :] = jnp.zeros((halo, cw), pad_ref.dtype)
        pad_ref[halo + tm:, :] = jnp.zeros((pad_ref.shape[0] - halo - tm, cw), pad_ref.dtype)
        k_all[0:sb_pad, :] = jnp.zeros((sb_pad, sbw), k_all.dtype)
        vm_all[:, 0:sb_pad, :] = jnp.zeros((n_sb_heads, sb_pad, sbw), vm_all.dtype)

    x = x_ref[0]
    ms = jnp.mean(x * x, axis=-1, keepdims=True)
    h = (x * lax.rsqrt(ms + EPS) * g_ref[...]).astype(BF16)

    def proj(c0, width):
        return jnp.dot(h, w_ref[:, c0:c0 + width], preferred_element_type=F32)

    def head_norm(u, gain):
        ss = jnp.dot((u * u).astype(BF16), ones_ref[...], preferred_element_type=F32)
        return u * lax.rsqrt(ss * (1.0 / HEAD_DIM) + EPS) * gain

    def rotary(u):
        n = cos_ref.shape[1]
        cos, sin = cos_ref[...], sin_ref[...]
        even = (lax.broadcasted_iota(jnp.int32, cos.shape, 1) & 1) == 0
        out = []
        for c0 in range(0, u.shape[1], n):
            uc = u[:, c0:c0 + n]
            nxt = pltpu.roll(uc, n - 1, axis=1)
            prv = pltpu.roll(uc, 1, axis=1)
            out.append(uc * cos + jnp.where(even, nxt, prv) * sin)
        return jnp.concatenate(out, axis=1)

    c_q, c_v = 0, 2 * sbw
    c_rq, c_rk, c_rv, c_rg = (3 * sbw + i * retw for i in range(4))
    c_a = 3 * sbw + 4 * retw

    def project_conv():
        ag = proj(c_a, 2 * cw) + pb_ref[...]
        pad_ref[halo:halo + tm, :] = (ag[:, :cw] * jax.nn.sigmoid(ag[:, cw:])).astype(pad_ref.dtype)

    def project_ret_gate():
        gate = proj(c_rg, retw)
        sg_s[...] = (gate * jax.nn.sigmoid(gate)).astype(BF16)

    def project_ret_q():
        rq_s[...] = rotary(proj(c_rq, retw)).astype(BF16)

    def project_ret_k():
        rk_s[...] = rotary(proj(c_rk, retw)).astype(BF16)

    def project_ret_v():
        rv_s[...] = proj(c_rv, retw).astype(BF16)

    def project_sb_qk():
        qk = proj(c_q, 2 * sbw)
        q_s[...] = (head_norm(qk[:, :sbw], gq_ref[...]) * (LOG2E * HEAD_DIM ** -0.5)).astype(BF16)
        k_all[pl.ds(sb_pad + t0, tm), :] = head_norm(qk[:, sbw:], gk_ref[...]).astype(BF16)

    def project_sb_v():
        v = proj(c_v, sbw).astype(BF16)
        for hh in range(n_sb_heads):
            vm_all[hh, pl.ds(sb_pad + t0, tm), :] = jnp.where(_head_lanes(v.shape, hh), v, jnp.zeros_like(v))

    n_shift = CONV_SHIFTS
    first = halo - (CONV_KERNEL - 1)

    def conv_block(r0):
        window = pad_ref[r0:r0 + CONV_WINDOW, :]
        acc = jnp.zeros((CONV_BLOCK, cw), F32) + db_ref[...]
        for shift in range(n_shift):
            if shift == 0:
                moved = window[:CONV_BLOCK + halo].astype(F32)
            else:
                moved = jnp.dot(shift_ref[shift - 1], window, preferred_element_type=F32)
            for tap in range(CONV_KERNEL):
                if (first + tap) % n_shift == shift:
                    a = (first + tap) // n_shift * n_shift
                    acc = acc + moved[a:a + CONV_BLOCK, :] * dw_ref[tap:tap + 1, :]
        mu = jnp.mean(acc, axis=-1, keepdims=True)
        xc = acc - mu
        var = jnp.mean(xc * xc, axis=-1, keepdims=True)
        yn = xc * lax.rsqrt(var + EPS) * lg_ref[...] + lb_ref[...]
        yconv_ref[0, r0:r0 + CONV_BLOCK, :] = (yn * jax.nn.sigmoid(yn)).astype(yconv_ref.dtype)

    ch = qdec_ref.shape[0]
    group = bd_ref.shape[0]
    heads_per_group = group // HEAD_DIM

    def per_head_rows(t):
        return jnp.concatenate(
            [jnp.where(_head_lanes(t.shape, hh), t, jnp.zeros_like(t)) for hh in range(heads_per_group)], axis=0)

    def retention_scores(c0, gi):
        cols = slice(gi * group, (gi + 1) * group)
        sc = lax.dot_general(rq_s[c0:c0 + ch, cols], per_head_rows(rk_s[c0:c0 + ch, cols]),
                             (((1,), (1,)), ((), ())), preferred_element_type=F32)
        return (sc * dmat_ref[gi]).astype(BF16)

    def retention_chunk(c0, gi, scores):
        cols = slice(gi * group, (gi + 1) * group)
        q = rq_s[c0:c0 + ch, cols]
        k = rk_s[c0:c0 + ch, cols]
        v = rv_s[c0:c0 + ch, cols]
        state = state_ref[gi]
        qd = (q.astype(F32) * qdec_ref[:, cols]).astype(BF16)
        out = (jnp.dot(qd, state.astype(BF16), preferred_element_type=F32)
               + jnp.dot(scores, per_head_rows(v), preferred_element_type=F32))
        kd = (k.astype(F32) * kdec_ref[:, cols]).astype(BF16)
        kv = lax.dot_general(kd, v, (((0,), (0,)), ((), ())), preferred_element_type=F32)
        state_ref[gi] = state * cdec_ref[:, cols] + kv * bd_ref[...]
        return out

    def retention_norm(c0, gi, out):
        cols = slice(gi * group, (gi + 1) * group)
        ss = jnp.dot((out * out).astype(BF16), ones_ref[...], preferred_element_type=F32)
        yn = out * lax.rsqrt(ss * (1.0 / HEAD_DIM) + EPS) * rg_ref[:, cols]
        yret_ref[0, c0:c0 + ch, cols] = (yn * sg_s[c0:c0 + ch, cols].astype(F32)).astype(yret_ref.dtype)

    sb_scores, sb_weights, sb_values, top_carry = _sb_stages(q_s, k_all, vm_all, sfx_ref, n_sb_heads)
    n_sub = tm // blk
    n_sweep = SB_LOOKBACK + 1
    sb_stage1, sb_stage2, sb_tops = {}, {}, []

    def sb_krow(s):
        return pl.multiple_of(sb_pad + t0 + (s - SB_LOOKBACK) * blk, blk)

    def sb_step(step):
        if step < n_sub:
            valid = [has_earlier_tile if step - SB_LOOKBACK + j < 0 else None for j in range(n_sweep)]
            sb_stage1[step] = sb_scores(step * blk, sb_krow(step), n_sweep, True, valid)
        s = step - 1
        if 0 <= s < n_sub:
            sb_stage2[s] = sb_weights(*sb_stage1.pop(s), n_sweep, [None] * n_sb_heads)
        s = step - 2
        if 0 <= s < n_sub:
            ws, carries = sb_stage2.pop(s)
            pv = sb_values(ws, sb_krow(s), n_sweep)
            acc_ref[s] = pv
            ysb_ref[0, s * blk:(s + 1) * blk, :] = pv.astype(ysb_ref.dtype)
            for hh in range(n_sb_heads):
                c_ref[s, hh] = carries[hh]
            sb_tops.append(top_carry(carries))

    ret_chunks = [(c0, gi) for c0 in range(0, tm, ch) for gi in range(retw // group)]
    project_conv()
    project_sb_qk()
    project_sb_v()
    conv_starts = list(range(0, tm, CONV_BLOCK))
    projections = [project_ret_q, project_ret_k, project_ret_v, project_ret_gate]
    sb_steps = list(range(n_sub + 2))
    for i in range(max(len(conv_starts), len(projections))):
        if i < len(conv_starts):
            conv_block(conv_starts[i])
        if i < len(projections):
            projections[i]()
    pad_ref[0:halo, :] = pad_ref[tm:tm + halo, :]
    scores, outs = {}, {}
    for i in range(len(ret_chunks) + 2):
        if i < len(ret_chunks):
            scores[i] = retention_scores(*ret_chunks[i])
        if sb_steps:
            sb_step(sb_steps.pop(0))
        if 0 <= i - 1 < len(ret_chunks):
            outs[i - 1] = retention_chunk(*ret_chunks[i - 1], scores.pop(i - 1))
        if 0 <= i - 2 < len(ret_chunks):
            retention_norm(*ret_chunks[i - 2], outs.pop(i - 2))
    for step in sb_steps:
        sb_step(step)

    @pl.when(jnp.max(top_carry(sb_tops)) > SB_DEAD_LOG2)
    def _():
        def sub_block(s, _):
            off = pl.multiple_of(s * blk, blk)
            q0 = t0 + off

            def cond(state):
                i, live = state
                return jnp.logical_and(i * blk <= q0, live)

            def body(state):
                i, _ = state
                krow = pl.multiple_of(sb_pad + q0 - i * blk, blk)
                ws, carries = sb_weights(*sb_scores(off, krow, 1, False), 1,
                                         [c_ref[s, hh] for hh in range(n_sb_heads)])
                acc_ref[s] += sb_values(ws, krow, 1)
                for hh in range(n_sb_heads):
                    c_ref[s, hh] = carries[hh]
                return i + 1, jnp.max(top_carry(carries)) > SB_DEAD_LOG2

            live = jnp.max(top_carry([c_ref[s, hh] for hh in range(n_sb_heads)])) > SB_DEAD_LOG2
            lax.while_loop(cond, body, (jnp.int32(n_sweep), live))
            ysb_ref[0, pl.ds(off, blk), :] = acc_ref[s].astype(ysb_ref.dtype)
            return 0

        lax.fori_loop(0, n_sub, sub_block, 0)


def _slab_specs(w, layer, n_steps, steps_per_row):
    rows, cols = w.shape[1:]
    assert rows % n_steps == 0 and (rows // n_steps) % 16 == 0
    slab = rows // n_steps
    return (pl.BlockSpec((None, slab, cols), lambda i, j: (layer, i * steps_per_row + j, 0)),
            pl.BlockSpec((slab, cols), lambda i, j: (i * steps_per_row + j, 0)))


def _mixers(layer, x, g, w, ones_bd, cos_t, sin_t, gq, gk, pb, shifts, dw_w, dw_b, ln_g, ln_b,
            qdec, kdec, dmat, cdec, bd, ret_g, sfx, next_weights, widths):
    B, T, D = x.shape
    sbw, retw, cw = widths
    tm = min(TOKEN_TILE, T)
    blk = SB_BLOCK
    assert tm % qdec.shape[0] == 0 and tm % CONV_BLOCK == 0 and tm % blk == 0
    n_sb_heads = sbw // HEAD_DIM
    key_rows = SB_LOOKBACK * blk + T
    grid = (B, T // tm)
    tok = lambda width: pl.BlockSpec((1, tm, width), lambda b, t: (b, t, 0))
    tab = pl.BlockSpec((tm, cos_t.shape[1]), lambda b, t: (t, 0))
    per_layer = lambda a: _layer_spec(a.shape, layer)
    shared = lambda a: _const_spec(a.shape)
    slabs = [_slab_specs(a, layer, grid[0] * grid[1], grid[1]) for a in next_weights]
    out_widths = (sbw, retw, cw)
    vmem = (2 * tm * D * 4 + w.size * 2 + dmat.size * 4 + 4 * tm * retw * 2 + (1 + n_sb_heads) * key_rows * sbw * 2
            + 2 * sum(out_widths) * tm * 2 + 28 * 1024 * 1024)
    return pl.pallas_call(
        functools.partial(_mixers_kernel, widths=widths),
        grid=grid,
        in_specs=[tok(D), per_layer(g), shared(w), shared(ones_bd), tab, tab,
                  per_layer(gq), per_layer(gk), per_layer(pb), shared(shifts), per_layer(dw_w), per_layer(dw_b),
                  per_layer(ln_g), per_layer(ln_b), shared(qdec), shared(kdec), shared(dmat), shared(cdec),
                  shared(bd), per_layer(ret_g), shared(sfx)] + [s[0] for s in slabs],
        out_specs=[tok(wd) for wd in out_widths] + [s[1] for s in slabs],
        out_shape=[jax.ShapeDtypeStruct((B, T, wd), BF16) for wd in out_widths]
                  + [jax.ShapeDtypeStruct(a.shape[1:], BF16) for a in next_weights],
        scratch_shapes=[pltpu.VMEM((tm + CONV_WINDOW - CONV_BLOCK, cw), BF16),
                        pltpu.VMEM((retw // bd.shape[0], bd.shape[0], bd.shape[0]), F32)]
                       + [pltpu.VMEM((tm, retw), BF16)] * 4
                       + [pltpu.VMEM((tm, sbw), BF16), pltpu.VMEM((key_rows, sbw), BF16),
                          pltpu.VMEM((n_sb_heads, key_rows, sbw), BF16),
                          pltpu.VMEM((tm // blk, blk, sbw), F32),
                          pltpu.VMEM((tm // blk, n_sb_heads, blk, blk), F32)],
        compiler_params=pltpu.CompilerParams(
            dimension_semantics=("arbitrary", "arbitrary"), vmem_limit_bytes=_vmem_limit(vmem)),
        name="mixers",
    )(x, g, w, ones_bd, cos_t, sin_t, gq, gk, pb, shifts, dw_w, dw_b, ln_g, ln_b,
      qdec, kdec, dmat, cdec, bd, ret_g, sfx, *next_weights)


def _out_mlp_kernel(x_ref, ysb_ref, yret_ref, yconv_ref, wo_ref, g_ref, w1_ref, w2_ref, *rest):
    if len(rest) == 1:
        (o_ref,) = rest
    else:
        w_next_f_ref, col_scale_ref, o_ref, w_next_b_ref = rest
        w_next_b_ref[...] = (w_next_f_ref[...] * col_scale_ref[...]).astype(w_next_b_ref.dtype)
    mixed = jnp.concatenate([ysb_ref[0], yret_ref[0], yconv_ref[0]], axis=1)
    x1 = x_ref[0] + jnp.dot(mixed, wo_ref[...], preferred_element_type=F32)
    ms = jnp.mean(x1 * x1, axis=-1, keepdims=True)
    h = (x1 * lax.rsqrt(ms + EPS) * g_ref[...]).astype(BF16)
    acc = x1
    for c0 in range(0, w1_ref.shape[1], FF_CHUNK):
        f = jnp.maximum(jnp.dot(h, w1_ref[:, c0:c0 + FF_CHUNK], preferred_element_type=F32), 0.0)
        acc = acc + jnp.dot((f * f).astype(BF16), w2_ref[c0:c0 + FF_CHUNK, :], preferred_element_type=F32)
    o_ref[0] = acc


def _out_mlp(layer, x, ysb, yret, yconv, wo, g, w1, w2, w_in, col_scale):
    B, T, D = x.shape
    tm = min(MLP_TILE, T)
    grid = (B, T // tm)
    tok = lambda width: pl.BlockSpec((1, tm, width), lambda b, t: (b, t, 0))
    dff = w1.shape[1]
    vmem = (D * D + 2 * D * dff) * 2 + 4 * tm * D * 4 + 4 * tm * D * 2 + tm * FF_CHUNK * 12 + 12 * 1024 * 1024
    in_specs = [tok(D), tok(ysb.shape[2]), tok(yret.shape[2]), tok(yconv.shape[2]),
                _const_spec(wo.shape), _layer_spec(g.shape, layer), _const_spec(w1.shape), _const_spec(w2.shape)]
    out_specs, out_shape, extra = [tok(D)], [jax.ShapeDtypeStruct((B, T, D), F32)], ()
    if layer + 1 < w_in.shape[0]:
        f_spec, b_spec = _slab_specs(w_in, layer + 1, grid[0] * grid[1], grid[1])
        in_specs += [f_spec, _const_spec(col_scale.shape)]
        out_specs.append(b_spec)
        out_shape.append(jax.ShapeDtypeStruct(w_in.shape[1:], BF16))
        extra = (w_in, col_scale)
    return pl.pallas_call(
        _out_mlp_kernel,
        grid=grid,
        in_specs=in_specs,
        out_specs=out_specs,
        out_shape=out_shape,
        compiler_params=pltpu.CompilerParams(
            dimension_semantics=("arbitrary", "arbitrary"), vmem_limit_bytes=_vmem_limit(vmem)),
        name="out_mlp",
    )(x, ysb, yret, yconv, wo, g, w1, w2, *extra)


def _block_diag_ones(n, block):
    i = np.arange(n)
    return (i[:, None] // block == i[None, :] // block).astype(np.float32)


def _shift_matrices():
    r = np.arange(CONV_BLOCK + CONV_HALO)[None, :, None]
    j = np.arange(CONV_WINDOW)[None, None, :]
    b = np.arange(1, CONV_SHIFTS)[:, None, None]
    return (j == r + b).astype(np.float32)


def _suffix_sum_matrix(blk):
    j = np.arange(2 * blk)[:, None] % blk
    s = np.arange(2 * blk)[None, :]
    return -((s >= blk) | (j > s)).astype(np.float32)


def _rotary_tables(T):
    inv = 1.0 / (ROPE_BASE ** jnp.linspace(0.0, 1.0, HEAD_DIM // 2, dtype=F32))
    ang = jnp.arange(T).astype(F32)[:, None] * inv[None, :]
    cos = jnp.repeat(jnp.cos(ang), 2, axis=1)
    sin = jnp.repeat(jnp.sin(ang), 2, axis=1)
    sign = jnp.where(jnp.arange(HEAD_DIM) % 2 == 0, -1.0, 1.0).astype(F32)
    reps = (1, V7X_LANES // HEAD_DIM)
    return jnp.tile(cos, reps), jnp.tile(sin * sign[None, :], reps)


def _retention_tables(n_heads, ch):
    log_g = jnp.log(1.0 - jnp.exp2(-5.0 - jnp.arange(n_heads, dtype=F32)))
    j = jnp.arange(ch, dtype=F32)
    per_lane = lambda t: jnp.repeat(t, HEAD_DIM, axis=-1)
    qdec = per_lane(jnp.exp(log_g[None, :] * (j + 1.0)[:, None]))
    kdec = per_lane(jnp.exp(log_g[None, :] * (ch - 1.0 - j)[:, None]))
    cdec = per_lane(jnp.exp(log_g * ch)[None, :])
    rel = j[:, None] - j[None, :]
    dmat = jnp.where(rel >= 0, jnp.exp(log_g[:, None, None] * jnp.maximum(rel, 0.0)), 0.0)
    per_group = V7X_MXU_DIM // HEAD_DIM
    dmat = dmat.reshape(n_heads // per_group, per_group, ch, ch).transpose(0, 2, 1, 3)
    return qdec, kdec, dmat.reshape(n_heads // per_group, ch, per_group * ch), cdec


def kernel(x, mix_norm_g, w_in, sb_q_norm_g, sb_k_norm_g, ret_norm_g, conv_pw_b, conv_dw_w, conv_dw_b,
           conv_ln_g, conv_ln_b, w_out, mlp_norm_g, w_ff1, w_ff2):
    B, T, D = x.shape
    depth = w_in.shape[0]
    sbw = (D // 256) * HEAD_DIM
    retw = (D // 128) * HEAD_DIM
    cw = D - sbw - retw
    assert all(T % tile == 0 or T < tile for tile in (TOKEN_TILE, MLP_TILE))
    assert T % RET_CHUNK == 0 and T % SB_BLOCK == 0
    widths = (sbw, retw, cw)
    n_ret_heads = retw // HEAD_DIM

    ones_bd = jnp.asarray(_block_diag_ones(V7X_MXU_DIM, HEAD_DIM), BF16)
    bd_mask = jnp.asarray(_block_diag_ones(V7X_MXU_DIM, HEAD_DIM), F32)
    sfx = jnp.asarray(_suffix_sum_matrix(SB_BLOCK), BF16)
    shifts = jnp.asarray(_shift_matrices(), BF16)
    cos_t, sin_t = _rotary_tables(T)
    qdec, kdec, dmat, cdec = _retention_tables(n_ret_heads, RET_CHUNK)
    rk0 = 3 * sbw + retw
    col_scale = jnp.ones((1, w_in.shape[2]), F32).at[:, rk0:rk0 + retw].set(HEAD_DIM ** -0.5)

    rows = lambda a: a.reshape(depth, 1, -1).astype(F32)
    mix_g, mlp_g, ret_g, pw_b = rows(mix_norm_g), rows(mlp_norm_g), rows(ret_norm_g), rows(conv_pw_b)
    gq, gk = (rows(jnp.tile(g, (1, sbw // HEAD_DIM))) for g in (sb_q_norm_g, sb_k_norm_g))
    dw_w, dw_b, ln_g, ln_b = conv_dw_w.astype(F32), rows(conv_dw_b), rows(conv_ln_g), rows(conv_ln_b)
    w_in_b = (w_in[0] * col_scale).astype(BF16)
    for l in range(depth):
        y_sb, y_ret, y_conv, w_out_b, w_ff1_b, w_ff2_b = _mixers(
            l, x, mix_g, w_in_b, ones_bd, cos_t, sin_t, gq, gk, pw_b, shifts, dw_w, dw_b, ln_g, ln_b,
            qdec, kdec, dmat, cdec, bd_mask, ret_g, sfx, (w_out, w_ff1, w_ff2), widths)
        x, *w_next = _out_mlp(l, x, y_sb, y_ret, y_conv, w_out_b, mlp_g, w_ff1_b, w_ff2_b, w_in, col_scale)
        if w_next:
            (w_in_b,) = w_next
    return x
```

```python
import functools

import jax
import jax.numpy as jnp
import numpy as np
from jax import lax
from jax.experimental import pallas as pl
from jax.experimental.pallas import tpu as pltpu

F32 = jnp.float32
BF16 = jnp.bfloat16

HEAD_DIM = 64
CONV_KERNEL = 31
EPS = 1e-6
ROPE_BASE = 10000.0
LOG2E = 1.4426950408889634
LN2 = 0.6931471805599453

V7X_LANES = 128
V7X_MXU_DIM = 256
V7X_VMEM_BYTES = 64 * 1024 * 1024

SB_DEAD_LOG2 = -150.0
SB_MASKED_SCORE = -1e30

TOKEN_TILE = 1024
MLP_TILE = 1024
SB_BLOCK = 128
SB_LOOKBACK = 2
RET_CHUNK = 128
CONV_HALO = 32
CONV_BLOCK = 128
CONV_WINDOW = 256
CONV_SHIFTS = 8
FF_CHUNK = 1024


def _vmem_limit(nbytes):
    return int(min(nbytes, V7X_VMEM_BYTES - 4 * 1024 * 1024))


def _const_spec(shape):
    nd = len(shape)
    return pl.BlockSpec(shape, lambda *_: (0,) * nd, pipeline_mode=pl.Buffered(1))


def _layer_spec(shape, layer):
    nd = len(shape)
    return pl.BlockSpec((None,) + tuple(shape[1:]), lambda *_: (layer,) + (0,) * (nd - 1),
                        pipeline_mode=pl.Buffered(1))


def _head_lanes(shape, h):
    lane = lax.broadcasted_iota(jnp.int32, shape, 1)
    return (lane >= h * HEAD_DIM) & (lane < (h + 1) * HEAD_DIM)


def _sb_stages(q_ref, k_ref, vm_ref, m_ref, n_heads):
    blk = SB_BLOCK
    heads = tuple(range(n_heads))
    causal = (lax.broadcasted_iota(jnp.int32, (blk, blk), 1) < lax.broadcasted_iota(jnp.int32, (blk, blk), 0))

    def masked_q(off):
        q = q_ref[pl.ds(off, blk), :]
        return jnp.concatenate([jnp.where(_head_lanes(q.shape, h), q, jnp.zeros_like(q)) for h in heads], axis=0)

    def scores(off, krow, nblk, diag_last, valid=None):
        kwin = k_ref[pl.ds(krow, nblk * blk), :]
        z_all = lax.dot_general(masked_q(off), kwin, (((1,), (1,)), ((), ())), preferred_element_type=F32)
        log_beta, split = {}, []
        for h in heads:
            for j in range(nblk):
                z = z_all[h * blk:(h + 1) * blk, j * blk:(j + 1) * blk]
                if diag_last and j == nblk - 1:
                    z = jnp.where(causal, z, SB_MASKED_SCORE)
                elif valid is not None and valid[j] is not None:
                    z = jnp.where(valid[j], z, SB_MASKED_SCORE)
                sp = jnp.maximum(z, 0.0) + jnp.log(1.0 + jnp.exp2(-jnp.abs(z))) * (1.0 / LN2)
                log_beta[h, j] = z - sp
                hi = sp.astype(BF16)
                lo = (sp - hi.astype(F32)).astype(BF16)
                split.append(jnp.concatenate([hi, lo], axis=1))
        return jnp.concatenate(split, axis=0), log_beta

    def weights(split, log_beta, nblk, carries):
        r = jnp.dot(split, m_ref[...], preferred_element_type=F32)
        ws, out_carries = [], []
        for h in heads:
            carry = carries[h]
            w = [None] * nblk
            for j in reversed(range(nblk)):
                rr = r[(h * nblk + j) * blk:(h * nblk + j + 1) * blk, :]
                arg = rr[:, :blk] + log_beta[h, j]
                if carry is not None:
                    arg = arg + carry
                w[j] = jnp.exp2(arg).astype(BF16)
                carry = rr[:, blk:] if carry is None else carry + rr[:, blk:]
            out_carries.append(carry)
            ws.extend(w)
        return jnp.concatenate(ws, axis=1), out_carries

    def values(ws, krow, nblk):
        vs = [vm_ref[h, pl.ds(krow, nblk * blk), :] for h in heads]
        return jnp.dot(ws, jnp.concatenate(vs, axis=0), preferred_element_type=F32)

    def top_carry(carries):
        top = carries[0]
        for c in carries[1:]:
            top = jnp.maximum(top, c)
        return top

    return scores, weights, values, top_carry


def _mixers_kernel(x_ref, g_ref, w_ref, ones_ref, cos_ref, sin_ref, gq_ref, gk_ref, pb_ref,
                   shift_ref, dw_ref, db_ref, lg_ref, lb_ref,
                   qdec_ref, kdec_ref, dmat_ref, cdec_ref, bd_ref, rg_ref, sfx_ref, wo_f_ref, w1_f_ref, w2_f_ref,
                   ysb_ref, yret_ref, yconv_ref, wo_b_ref, w1_b_ref, w2_b_ref,
                   pad_ref, state_ref, rq_s, rk_s, rv_s, sg_s, q_s, k_all, vm_all, acc_ref, c_ref, *, widths):
    for f_ref, b_ref in ((wo_f_ref, wo_b_ref), (w1_f_ref, w1_b_ref), (w2_f_ref, w2_b_ref)):
        b_ref[...] = f_ref[...].astype(b_ref.dtype)
    sbw, retw, cw = widths
    tm = x_ref.shape[1]
    halo = CONV_HALO
    blk = SB_BLOCK
    sb_pad = SB_LOOKBACK * blk
    n_sb_heads = sbw // HEAD_DIM
    t0 = pl.multiple_of(pl.program_id(1) * tm, tm)
    has_earlier_tile = pl.program_id(1) > 0

    @pl.when(pl.program_id(1) == 0)
    def _():
        state_ref[...] = jnp.zeros_like(state_ref)
        pad_ref[0:halo, :] = jnp.zeros((halo, cw), pad_ref.dtype)
        pad_ref[halo + tm:, :] = jnp.zeros((pad_ref.shape[0] - halo - tm, cw), pad_ref.dtype)
        k_all[0:sb_pad, :] = jnp.zeros((sb_pad, sbw), k_all.dtype)
        vm_all[:, 0:sb_pad, :] = jnp.zeros((n_sb_heads, sb_pad, sbw), vm_all.dtype)

    x = x_ref[0]
    ms = jnp.mean(x * x, axis=-1, keepdims=True)
    h = (x * lax.rsqrt(ms + EPS) * g_ref[...]).astype(BF16)

    def proj(c0, width):
        return jnp.dot(h, w_ref[:, c0:c0 + width], preferred_element_type=F32)

    def head_norm(u, gain):
        ss = jnp.dot((u * u).astype(BF16), ones_ref[...], preferred_element_type=F32)
        return u * lax.rsqrt(ss * (1.0 / HEAD_DIM) + EPS) * gain

    def rotary(u):
        n = cos_ref.shape[1]
        cos, sin = cos_ref[...], sin_ref[...]
        even = (lax.broadcasted_iota(jnp.int32, cos.shape, 1) & 1) == 0
        out = []
        for c0 in range(0, u.shape[1], n):
            uc = u[:, c0:c0 + n]
            nxt = pltpu.roll(uc, n - 1, axis=1)
            prv = pltpu.roll(uc, 1, axis=1)
            out.append(uc * cos + jnp.where(even, nxt, prv) * sin)
        return jnp.concatenate(out, axis=1)

    c_q, c_v = 0, 2 * sbw
    c_rq, c_rk, c_rv, c_rg = (3 * sbw + i * retw for i in range(4))
    c_a = 3 * sbw + 4 * retw

    def project_conv():
        ag = proj(c_a, 2 * cw) + pb_ref[...]
        pad_ref[halo:halo + tm, :] = (ag[:, :cw] * jax.nn.sigmoid(ag[:, cw:])).astype(pad_ref.dtype)

    def project_ret_gate():
        gate = proj(c_rg, retw)
        sg_s[...] = (gate * jax.nn.sigmoid(gate)).astype(BF16)

    def project_ret_q():
        rq_s[...] = rotary(proj(c_rq, retw)).astype(BF16)

    def project_ret_k():
        rk_s[...] = rotary(proj(c_rk, retw)).astype(BF16)

    def project_ret_v():
        rv_s[...] = proj(c_rv, retw).astype(BF16)

    def project_sb_qk():
        qk = proj(c_q, 2 * sbw)
        q_s[...] = (head_norm(qk[:, :sbw], gq_ref[...]) * (LOG2E * HEAD_DIM ** -0.5)).astype(BF16)
        k_all[pl.ds(sb_pad + t0, tm), :] = head_norm(qk[:, sbw:], gk_ref[...]).astype(BF16)

    def project_sb_v():
        v = proj(c_v, sbw).astype(BF16)
        for hh in range(n_sb_heads):
            vm_all[hh, pl.ds(sb_pad + t0, tm), :] = jnp.where(_head_lanes(v.shape, hh), v, jnp.zeros_like(v))

    n_shift = CONV_SHIFTS
    first = halo - (CONV_KERNEL - 1)

    def conv_block(r0):
        window = pad_ref[r0:r0 + CONV_WINDOW, :]
        acc = jnp.zeros((CONV_BLOCK, cw), F32) + db_ref[...]
        for shift in range(n_shift):
            if shift == 0:
                moved = window[:CONV_BLOCK + halo].astype(F32)
            else:
                moved = jnp.dot(shift_ref[shift - 1], window, preferred_element_type=F32)
            for tap in range(CONV_KERNEL):
                if (first + tap) % n_shift == shift:
                    a = (first + tap) // n_shift * n_shift
                    acc = acc + moved[a:a + CONV_BLOCK, :] * dw_ref[tap:tap + 1, :]
        mu = jnp.mean(acc, axis=-1, keepdims=True)
        xc = acc - mu
        var = jnp.mean(xc * xc, axis=-1, keepdims=True)
        yn = xc * lax.rsqrt(var + EPS) * lg_ref[...] + lb_ref[...]
        yconv_ref[0, r0:r0 + CONV_BLOCK, :] = (yn * jax.nn.sigmoid(yn)).astype(yconv_ref.dtype)

    ch = qdec_ref.shape[0]
    group = bd_ref.shape[0]
    heads_per_group = group // HEAD_DIM

    def per_head_rows(t):
        return jnp.concatenate(
            [jnp.where(_head_lanes(t.shape, hh), t, jnp.zeros_like(t)) for hh in range(heads_per_group)], axis=0)

    def retention_scores(c0, gi):
        cols = slice(gi * group, (gi + 1) * group)
        sc = lax.dot_general(rq_s[c0:c0 + ch, cols], per_head_rows(rk_s[c0:c0 + ch, cols]),
                             (((1,), (1,)), ((), ())), preferred_element_type=F32)
        return (sc * dmat_ref[gi]).astype(BF16)

    def retention_chunk(c0, gi, scores):
        cols = slice(gi * group, (gi + 1) * group)
        q = rq_s[c0:c0 + ch, cols]
        k = rk_s[c0:c0 + ch, cols]
        v = rv_s[c0:c0 + ch, cols]
        state = state_ref[gi]
        qd = (q.astype(F32) * qdec_ref[:, cols]).astype(BF16)
        out = (jnp.dot(qd, state.astype(BF16), preferred_element_type=F32)
               + jnp.dot(scores, per_head_rows(v), preferred_element_type=F32))
        kd = (k.astype(F32) * kdec_ref[:, cols]).astype(BF16)
        kv = lax.dot_general(kd, v, (((0,), (0,)), ((), ())), preferred_element_type=F32)
        state_ref[gi] = state * cdec_ref[:, cols] + kv * bd_ref[...]
        return out

    def retention_norm(c0, gi, out):
        cols = slice(gi * group, (gi + 1) * group)
        ss = jnp.dot((out * out).astype(BF16), ones_ref[...], preferred_element_type=F32)
        yn = out * lax.rsqrt(ss * (1.0 / HEAD_DIM) + EPS) * rg_ref[:, cols]
        yret_ref[0, c0:c0 + ch, cols] = (yn * sg_s[c0:c0 + ch, cols].astype(F32)).astype(yret_ref.dtype)

    sb_scores, sb_weights, sb_values, top_carry = _sb_stages(q_s, k_all, vm_all, sfx_ref, n_sb_heads)
    n_sub = tm // blk
    n_sweep = SB_LOOKBACK + 1
    sb_stage1, sb_stage2, sb_tops = {}, {}, []

    def sb_krow(s):
        return pl.multiple_of(sb_pad + t0 + (s - SB_LOOKBACK) * blk, blk)

    def sb_step(step):
        if step < n_sub:
            valid = [has_earlier_tile if step - SB_LOOKBACK + j < 0 else None for j in range(n_sweep)]
            sb_stage1[step] = sb_scores(step * blk, sb_krow(step), n_sweep, True, valid)
        s = step - 1
        if 0 <= s < n_sub:
            sb_stage2[s] = sb_weights(*sb_stage1.pop(s), n_sweep, [None] * n_sb_heads)
        s = step - 2
        if 0 <= s < n_sub:
            ws, carries = sb_stage2.pop(s)
            pv = sb_values(ws, sb_krow(s), n_sweep)
            acc_ref[s] = pv
            ysb_ref[0, s * blk:(s + 1) * blk, :] = pv.astype(ysb_ref.dtype)
            for hh in range(n_sb_heads):
                c_ref[s, hh] = carries[hh]
            sb_tops.append(top_carry(carries))

    ret_chunks = [(c0, gi) for c0 in range(0, tm, ch) for gi in range(retw // group)]
    project_conv()
    project_sb_qk()
    project_sb_v()
    conv_starts = list(range(0, tm, CONV_BLOCK))
    projections = [project_ret_q, project_ret_k, project_ret_v, project_ret_gate]
    sb_steps = list(range(n_sub + 2))
    for i in range(max(len(conv_starts), len(projections))):
        if i < len(conv_starts):
            conv_block(conv_starts[i])
        if i < len(projections):
            projections[i]()
    pad_ref[0:halo, :] = pad_ref[tm:tm + halo, :]
    scores, outs = {}, {}
    for i in range(len(ret_chunks) + 2):
        if i < len(ret_chunks):
            scores[i] = retention_scores(*ret_chunks[i])
        if sb_steps:
            sb_step(sb_steps.pop(0))
        if 0 <= i - 1 < len(ret_chunks):
            outs[i - 1] = retention_chunk(*ret_chunks[i - 1], scores.pop(i - 1))
        if 0 <= i - 2 < len(ret_chunks):
            retention_norm(*ret_chunks[i - 2], outs.pop(i - 2))
    for step in sb_steps:
        sb_step(step)

    @pl.when(jnp.max(top_carry(sb_tops)) > SB_DEAD_LOG2)
    def _():
        def sub_block(s, _):
            off = pl.multiple_of(s * blk, blk)
            q0 = t0 + off

            def cond(state):
                i, live = state
                return jnp.logical_and(i * blk <= q0, live)

            def body(state):
                i, _ = state
                krow = pl.multiple_of(sb_pad + q0 - i * blk, blk)
                ws, carries = sb_weights(*sb_scores(off, krow, 1, False), 1,
                                         [c_ref[s, hh] for hh in range(n_sb_heads)])
                acc_ref[s] += sb_values(ws, krow, 1)
                for hh in range(n_sb_heads):
                    c_ref[s, hh] = carries[hh]
                return i + 1, jnp.max(top_carry(carries)) > SB_DEAD_LOG2

            live = jnp.max(top_carry([c_ref[s, hh] for hh in range(n_sb_heads)])) > SB_DEAD_LOG2
            lax.while_loop(cond, body, (jnp.int32(n_sweep), live))
            ysb_ref[0, pl.ds(off, blk), :] = acc_ref[s].astype(ysb_ref.dtype)
            return 0

        lax.fori_loop(0, n_sub, sub_block, 0)


def _slab_specs(w, layer, n_steps, steps_per_row):
    rows, cols = w.shape[1:]
    assert rows % n_steps == 0 and (rows // n_steps) % 16 == 0
    slab = rows // n_steps
    return (pl.BlockSpec((None, slab, cols), lambda i, j: (layer, i * steps_per_row + j, 0)),
            pl.BlockSpec((slab, cols), lambda i, j: (i * steps_per_row + j, 0)))


def _mixers(layer, x, g, w, ones_bd, cos_t, sin_t, gq, gk, pb, shifts, dw_w, dw_b, ln_g, ln_b,
            qdec, kdec, dmat, cdec, bd, ret_g, sfx, next_weights, widths):
    B, T, D = x.shape
    sbw, retw, cw = widths
    tm = min(TOKEN_TILE, T)
    blk = SB_BLOCK
    assert tm % qdec.shape[0] == 0 and tm % CONV_BLOCK == 0 and tm % blk == 0
    n_sb_heads = sbw // HEAD_DIM
    key_rows = SB_LOOKBACK * blk + T
    grid = (B, T // tm)
    tok = lambda width: pl.BlockSpec((1, tm, width), lambda b, t: (b, t, 0))
    tab = pl.BlockSpec((tm, cos_t.shape[1]), lambda b, t: (t, 0))
    per_layer = lambda a: _layer_spec(a.shape, layer)
    shared = lambda a: _const_spec(a.shape)
    slabs = [_slab_specs(a, layer, grid[0] * grid[1], grid[1]) for a in next_weights]
    out_widths = (sbw, retw, cw)
    vmem = (2 * tm * D * 4 + w.size * 2 + dmat.size * 4 + 4 * tm * retw * 2 + (1 + n_sb_heads) * key_rows * sbw * 2
            + 2 * sum(out_widths) * tm * 2 + 28 * 1024 * 1024)
    return pl.pallas_call(
        functools.partial(_mixers_kernel, widths=widths),
        grid=grid,
        in_specs=[tok(D), per_layer(g), shared(w), shared(ones_bd), tab, tab,
                  per_layer(gq), per_layer(gk), per_layer(pb), shared(shifts), per_layer(dw_w), per_layer(dw_b),
                  per_layer(ln_g), per_layer(ln_b), shared(qdec), shared(kdec), shared(dmat), shared(cdec),
                  shared(bd), per_layer(ret_g), shared(sfx)] + [s[0] for s in slabs],
        out_specs=[tok(wd) for wd in out_widths] + [s[1] for s in slabs],
        out_shape=[jax.ShapeDtypeStruct((B, T, wd), BF16) for wd in out_widths]
                  + [jax.ShapeDtypeStruct(a.shape[1:], BF16) for a in next_weights],
        scratch_shapes=[pltpu.VMEM((tm + CONV_WINDOW - CONV_BLOCK, cw), BF16),
                        pltpu.VMEM((retw // bd.shape[0], bd.shape[0], bd.shape[0]), F32)]
                       + [pltpu.VMEM((tm, retw), BF16)] * 4
                       + [pltpu.VMEM((tm, sbw), BF16), pltpu.VMEM((key_rows, sbw), BF16),
                          pltpu.VMEM((n_sb_heads, key_rows, sbw), BF16),
                          pltpu.VMEM((tm // blk, blk, sbw), F32),
                          pltpu.VMEM((tm // blk, n_sb_heads, blk, blk), F32)],
        compiler_params=pltpu.CompilerParams(
            dimension_semantics=("arbitrary", "arbitrary"), vmem_limit_bytes=_vmem_limit(vmem)),
        name="mixers",
    )(x, g, w, ones_bd, cos_t, sin_t, gq, gk, pb, shifts, dw_w, dw_b, ln_g, ln_b,
      qdec, kdec, dmat, cdec, bd, ret_g, sfx, *next_weights)


def _out_mlp_kernel(x_ref, ysb_ref, yret_ref, yconv_ref, wo_ref, g_ref, w1_ref, w2_ref, *rest):
    if len(rest) == 1:
        (o_ref,) = rest
    else:
        w_next_f_ref, col_scale_ref, o_ref, w_next_b_ref = rest
        w_next_b_ref[...] = (w_next_f_ref[...] * col_scale_ref[...]).astype(w_next_b_ref.dtype)
    mixed = jnp.concatenate([ysb_ref[0], yret_ref[0], yconv_ref[0]], axis=1)
    x1 = x_ref[0] + jnp.dot(mixed, wo_ref[...], preferred_element_type=F32)
    ms = jnp.mean(x1 * x1, axis=-1, keepdims=True)
    h = (x1 * lax.rsqrt(ms + EPS) * g_ref[...]).astype(BF16)
    acc = x1
    for c0 in range(0, w1_ref.shape[1], FF_CHUNK):
        f = jnp.maximum(jnp.dot(h, w1_ref[:, c0:c0 + FF_CHUNK], preferred_element_type=F32), 0.0)
        acc = acc + jnp.dot((f * f).astype(BF16), w2_ref[c0:c0 + FF_CHUNK, :], preferred_element_type=F32)
    o_ref[0] = acc


def _out_mlp(layer, x, ysb, yret, yconv, wo, g, w1, w2, w_in, col_scale):
    B, T, D = x.shape
    tm = min(MLP_TILE, T)
    grid = (B, T // tm)
    tok = lambda width: pl.BlockSpec((1, tm, width), lambda b, t: (b, t, 0))
    dff = w1.shape[1]
    vmem = (D * D + 2 * D * dff) * 2 + 4 * tm * D * 4 + 4 * tm * D * 2 + tm * FF_CHUNK * 12 + 12 * 1024 * 1024
    in_specs = [tok(D), tok(ysb.shape[2]), tok(yret.shape[2]), tok(yconv.shape[2]),
                _const_spec(wo.shape), _layer_spec(g.shape, layer), _const_spec(w1.shape), _const_spec(w2.shape)]
    out_specs, out_shape, extra = [tok(D)], [jax.ShapeDtypeStruct((B, T, D), F32)], ()
    if layer + 1 < w_in.shape[0]:
        f_spec, b_spec = _slab_specs(w_in, layer + 1, grid[0] * grid[1], grid[1])
        in_specs += [f_spec, _const_spec(col_scale.shape)]
        out_specs.append(b_spec)
        out_shape.append(jax.ShapeDtypeStruct(w_in.shape[1:], BF16))
        extra = (w_in, col_scale)
    return pl.pallas_call(
        _out_mlp_kernel,
        grid=grid,
        in_specs=in_specs,
        out_specs=out_specs,
        out_shape=out_shape,
        compiler_params=pltpu.CompilerParams(
            dimension_semantics=("arbitrary", "arbitrary"), vmem_limit_bytes=_vmem_limit(vmem)),
        name="out_mlp",
    )(x, ysb, yret, yconv, wo, g, w1, w2, *extra)


def _block_diag_ones(n, block):
    i = np.arange(n)
    return (i[:, None] // block == i[None, :] // block).astype(np.float32)


def _shift_matrices():
    r = np.arange(CONV_BLOCK + CONV_HALO)[None, :, None]
    j = np.arange(CONV_WINDOW)[None, None, :]
    b = np.arange(1, CONV_SHIFTS)[:, None, None]
    return (j == r + b).astype(np.float32)


def _suffix_sum_matrix(blk):
    j = np.arange(2 * blk)[:, None] % blk
    s = np.arange(2 * blk)[None, :]
    return -((s >= blk) | (j > s)).astype(np.float32)


def _rotary_tables(T):
    inv = 1.0 / (ROPE_BASE ** np.linspace(0.0, 1.0, HEAD_DIM // 2))
    ang = np.arange(T)[:, None] * inv[None, :]
    cos = np.repeat(np.cos(ang), 2, axis=1)
    sin = np.repeat(np.sin(ang), 2, axis=1)
    sign = np.where(np.arange(HEAD_DIM) % 2 == 0, -1.0, 1.0)
    reps = (1, V7X_LANES // HEAD_DIM)
    return jnp.asarray(np.tile(cos, reps), F32), jnp.asarray(np.tile(sin * sign[None, :], reps), F32)


def _retention_tables(n_heads, ch):
    log_g = np.log(1.0 - np.exp2(-5.0 - np.arange(n_heads)))
    j = np.arange(ch, dtype=np.float64)
    per_lane = lambda t: np.repeat(t, HEAD_DIM, axis=-1)
    qdec = per_lane(np.exp(log_g[None, :] * (j + 1.0)[:, None]))
    kdec = per_lane(np.exp(log_g[None, :] * (ch - 1.0 - j)[:, None]))
    cdec = per_lane(np.exp(log_g * ch)[None, :])
    rel = j[:, None] - j[None, :]
    dmat = np.where(rel >= 0, np.exp(log_g[:, None, None] * np.maximum(rel, 0.0)), 0.0)
    per_group = V7X_MXU_DIM // HEAD_DIM
    dmat = dmat.reshape(n_heads // per_group, per_group, ch, ch).transpose(0, 2, 1, 3)
    dmat = dmat.reshape(n_heads // per_group, ch, per_group * ch)
    return tuple(jnp.asarray(t, F32) for t in (qdec, kdec, dmat, cdec))


def kernel(x, mix_norm_g, w_in, sb_q_norm_g, sb_k_norm_g, ret_norm_g, conv_pw_b, conv_dw_w, conv_dw_b,
           conv_ln_g, conv_ln_b, w_out, mlp_norm_g, w_ff1, w_ff2):
    B, T, D = x.shape
    depth = w_in.shape[0]
    sbw = (D // 256) * HEAD_DIM
    retw = (D // 128) * HEAD_DIM
    cw = D - sbw - retw
    assert all(T % tile == 0 or T < tile for tile in (TOKEN_TILE, MLP_TILE))
    assert T % RET_CHUNK == 0 and T % SB_BLOCK == 0
    widths = (sbw, retw, cw)
    n_ret_heads = retw // HEAD_DIM

    ones_bd = jnp.asarray(_block_diag_ones(V7X_MXU_DIM, HEAD_DIM), BF16)
    bd_mask = jnp.asarray(_block_diag_ones(V7X_MXU_DIM, HEAD_DIM), F32)
    sfx = jnp.asarray(_suffix_sum_matrix(SB_BLOCK), BF16)
    shifts = jnp.asarray(_shift_matrices(), BF16)
    cos_t, sin_t = _rotary_tables(T)
    qdec, kdec, dmat, cdec = _retention_tables(n_ret_heads, RET_CHUNK)
    rk0 = 3 * sbw + retw
    col_scale = np.ones((1, w_in.shape[2]), np.float32)
    col_scale[:, rk0:rk0 + retw] = HEAD_DIM ** -0.5
    col_scale = jnp.asarray(col_scale)

    rows = lambda a: a.reshape(depth, 1, -1).astype(F32)
    mix_g, mlp_g, ret_g, pw_b = rows(mix_norm_g), rows(mlp_norm_g), rows(ret_norm_g), rows(conv_pw_b)
    gq, gk = (rows(jnp.tile(g, (1, sbw // HEAD_DIM))) for g in (sb_q_norm_g, sb_k_norm_g))
    dw_w, dw_b, ln_g, ln_b = conv_dw_w.astype(F32), rows(conv_dw_b), rows(conv_ln_g), rows(conv_ln_b)
    w_in_b = (w_in[0] * col_scale).astype(BF16)
    for l in range(depth):
        y_sb, y_ret, y_conv, w_out_b, w_ff1_b, w_ff2_b = _mixers(
            l, x, mix_g, w_in_b, ones_bd, cos_t, sin_t, gq, gk, pw_b, shifts, dw_w, dw_b, ln_g, ln_b,
            qdec, kdec, dmat, cdec, bd_mask, ret_g, sfx, (w_out, w_ff1, w_ff2), widths)
        x, *w_next = _out_mlp(l, x, y_sb, y_ret, y_conv, w_out_b, mlp_g, w_ff1_b, w_ff2_b, w_in, col_scale)
        if w_next:
            (w_in_b,) = w_next
    return x
```

```python
import functools

import jax
import jax.numpy as jnp
import numpy as np
from jax import lax
from jax.experimental import pallas as pl
from jax.experimental.pallas import tpu as pltpu

F32 = jnp.float32
BF16 = jnp.bfloat16

HEAD_DIM = 64
CONV_KERNEL = 31
EPS = 1e-6
ROPE_BASE = 10000.0
LOG2E = 1.4426950408889634
LN2 = 0.6931471805599453

V7X_LANES = 128
V7X_MXU_DIM = 256
V7X_VMEM_BYTES = 64 * 1024 * 1024

SB_DEAD_LOG2 = -150.0
SB_MASKED_SCORE = -1e30

TOKEN_TILE = 1024
MLP_TILE = 1024
SB_BLOCK = 128
SB_LOOKBACK = 2
RET_CHUNK = 128
CONV_HALO = 32
CONV_BLOCK = 128
CONV_WINDOW = 256
CONV_SHIFTS = 8
FF_CHUNK = 1024


def _vmem_limit(nbytes):
    return int(min(nbytes, V7X_VMEM_BYTES - 4 * 1024 * 1024))


def _const_spec(shape):
    nd = len(shape)
    return pl.BlockSpec(shape, lambda *_: (0,) * nd, pipeline_mode=pl.Buffered(1))


def _layer_spec(shape, layer):
    nd = len(shape)
    return pl.BlockSpec((None,) + tuple(shape[1:]), lambda *_: (layer,) + (0,) * (nd - 1),
                        pipeline_mode=pl.Buffered(1))


def _head_lanes(shape, h):
    lane = lax.broadcasted_iota(jnp.int32, shape, 1)
    return (lane >= h * HEAD_DIM) & (lane < (h + 1) * HEAD_DIM)


def _sb_stages(q_ref, k_ref, vm_ref, m_ref, n_heads):
    blk = SB_BLOCK
    heads = tuple(range(n_heads))
    causal = (lax.broadcasted_iota(jnp.int32, (blk, blk), 1) < lax.broadcasted_iota(jnp.int32, (blk, blk), 0))

    def masked_q(off):
        q = q_ref[pl.ds(off, blk), :]
        return jnp.concatenate([jnp.where(_head_lanes(q.shape, h), q, jnp.zeros_like(q)) for h in heads], axis=0)

    def scores(off, krow, nblk, diag_last, valid=None):
        kwin = k_ref[pl.ds(krow, nblk * blk), :]
        z_all = lax.dot_general(masked_q(off), kwin, (((1,), (1,)), ((), ())), preferred_element_type=F32)
        log_beta, split = {}, []
        for h in heads:
            for j in range(nblk):
                z = z_all[h * blk:(h + 1) * blk, j * blk:(j + 1) * blk]
                if diag_last and j == nblk - 1:
                    z = jnp.where(causal, z, SB_MASKED_SCORE)
                elif valid is not None and valid[j] is not None:
                    z = jnp.where(valid[j], z, SB_MASKED_SCORE)
                sp = jnp.maximum(z, 0.0) + jnp.log(1.0 + jnp.exp2(-jnp.abs(z))) * (1.0 / LN2)
                log_beta[h, j] = z - sp
                hi = sp.astype(BF16)
                lo = (sp - hi.astype(F32)).astype(BF16)
                split.append(jnp.concatenate([hi, lo], axis=1))
        return jnp.concatenate(split, axis=0), log_beta

    def weights(split, log_beta, nblk, carries):
        r = jnp.dot(split, m_ref[...], preferred_element_type=F32)
        ws, out_carries = [], []
        for h in heads:
            carry = carries[h]
            w = [None] * nblk
            for j in reversed(range(nblk)):
                rr = r[(h * nblk + j) * blk:(h * nblk + j + 1) * blk, :]
                arg = rr[:, :blk] + log_beta[h, j]
                if carry is not None:
                    arg = arg + carry
                w[j] = jnp.exp2(arg).astype(BF16)
                carry = rr[:, blk:] if carry is None else carry + rr[:, blk:]
            out_carries.append(carry)
            ws.extend(w)
        return jnp.concatenate(ws, axis=1), out_carries

    def values(ws, krow, nblk):
        vs = [vm_ref[h, pl.ds(krow, nblk * blk), :] for h in heads]
        return jnp.dot(ws, jnp.concatenate(vs, axis=0), preferred_element_type=F32)

    def top_carry(carries):
        top = carries[0]
        for c in carries[1:]:
            top = jnp.maximum(top, c)
        return top

    return scores, weights, values, top_carry


def _mixers_kernel(x_ref, g_ref, w_ref, ones_ref, cos_ref, sin_ref, gq_ref, gk_ref, pb_ref,
                   shift_ref, dw_ref, db_ref, lg_ref, lb_ref,
                   qdec_ref, kdec_ref, dmat_ref, cdec_ref, bd_ref, rg_ref, sfx_ref, wo_f_ref, w1_f_ref, w2_f_ref,
                   ysb_ref, yret_ref, yconv_ref, wo_b_ref, w1_b_ref, w2_b_ref,
                   pad_ref, state_ref, rq_s, rk_s, rv_s, sg_s, q_s, k_all, vm_all, acc_ref, c_ref,
                   *, layer, widths):
    for f_ref, b_ref in ((wo_f_ref, wo_b_ref), (w1_f_ref, w1_b_ref), (w2_f_ref, w2_b_ref)):
        b_ref[...] = f_ref[...].astype(b_ref.dtype)
    row = slice(layer, layer + 1)
    sbw, retw, cw = widths
    tm = x_ref.shape[1]
    halo = CONV_HALO
    blk = SB_BLOCK
    sb_pad = SB_LOOKBACK * blk
    n_sb_heads = sbw // HEAD_DIM
    t0 = pl.multiple_of(pl.program_id(1) * tm, tm)
    has_earlier_tile = pl.program_id(1) > 0

    @pl.when(pl.program_id(1) == 0)
    def _():
        state_ref[...] = jnp.zeros_like(state_ref)
        pad_ref[0:halo, :] = jnp.zeros((halo, cw), pad_ref.dtype)
        pad_ref[halo + tm:, :] = jnp.zeros((pad_ref.shape[0] - halo - tm, cw), pad_ref.dtype)
        k_all[0:sb_pad, :] = jnp.zeros((sb_pad, sbw), k_all.dtype)
        vm_all[:, 0:sb_pad, :] = jnp.zeros((n_sb_heads, sb_pad, sbw), vm_all.dtype)

    x = x_ref[0]
    ms = jnp.mean(x * x, axis=-1, keepdims=True)
    h = (x * lax.rsqrt(ms + EPS) * g_ref[row, :]).astype(BF16)

    def proj(c0, width):
        return jnp.dot(h, w_ref[:, c0:c0 + width], preferred_element_type=F32)

    def head_norm(u, gain):
        ss = jnp.dot((u * u).astype(BF16), ones_ref[...], preferred_element_type=F32)
        return u * lax.rsqrt(ss * (1.0 / HEAD_DIM) + EPS) * gain

    def rotary(u):
        n = cos_ref.shape[1]
        cos, sin = cos_ref[...], sin_ref[...]
        even = (lax.broadcasted_iota(jnp.int32, cos.shape, 1) & 1) == 0
        out = []
        for c0 in range(0, u.shape[1], n):
            uc = u[:, c0:c0 + n]
            nxt = pltpu.roll(uc, n - 1, axis=1)
            prv = pltpu.roll(uc, 1, axis=1)
            out.append(uc * cos + jnp.where(even, nxt, prv) * sin)
        return jnp.concatenate(out, axis=1)

    c_q, c_v = 0, 2 * sbw
    c_rq, c_rk, c_rv, c_rg = (3 * sbw + i * retw for i in range(4))
    c_a = 3 * sbw + 4 * retw

    def project_conv():
        ag = proj(c_a, 2 * cw) + pb_ref[row, :]
        pad_ref[halo:halo + tm, :] = (ag[:, :cw] * jax.nn.sigmoid(ag[:, cw:])).astype(pad_ref.dtype)

    def project_ret_gate():
        gate = proj(c_rg, retw)
        sg_s[...] = (gate * jax.nn.sigmoid(gate)).astype(BF16)

    def project_ret_q():
        rq_s[...] = rotary(proj(c_rq, retw)).astype(BF16)

    def project_ret_k():
        rk_s[...] = rotary(proj(c_rk, retw)).astype(BF16)

    def project_ret_v():
        rv_s[...] = proj(c_rv, retw).astype(BF16)

    def project_sb_qk():
        qk = proj(c_q, 2 * sbw)
        q_s[...] = (head_norm(qk[:, :sbw], gq_ref[row, :]) * (LOG2E * HEAD_DIM ** -0.5)).astype(BF16)
        k_all[pl.ds(sb_pad + t0, tm), :] = head_norm(qk[:, sbw:], gk_ref[row, :]).astype(BF16)

    def project_sb_v():
        v = proj(c_v, sbw).astype(BF16)
        for hh in range(n_sb_heads):
            vm_all[hh, pl.ds(sb_pad + t0, tm), :] = jnp.where(_head_lanes(v.shape, hh), v, jnp.zeros_like(v))

    n_shift = CONV_SHIFTS
    first = halo - (CONV_KERNEL - 1)

    def conv_block(r0):
        window = pad_ref[r0:r0 + CONV_WINDOW, :]
        acc = jnp.zeros((CONV_BLOCK, cw), F32) + db_ref[row, :]
        for shift in range(n_shift):
            if shift == 0:
                moved = window[:CONV_BLOCK + halo].astype(F32)
            else:
                moved = jnp.dot(shift_ref[shift - 1], window, preferred_element_type=F32)
            for tap in range(CONV_KERNEL):
                if (first + tap) % n_shift == shift:
                    a = (first + tap) // n_shift * n_shift
                    acc = acc + moved[a:a + CONV_BLOCK, :] * dw_ref[tap:tap + 1, :]
        mu = jnp.mean(acc, axis=-1, keepdims=True)
        xc = acc - mu
        var = jnp.mean(xc * xc, axis=-1, keepdims=True)
        yn = xc * lax.rsqrt(var + EPS) * lg_ref[row, :] + lb_ref[row, :]
        yconv_ref[0, r0:r0 + CONV_BLOCK, :] = (yn * jax.nn.sigmoid(yn)).astype(yconv_ref.dtype)

    ch = qdec_ref.shape[0]
    group = bd_ref.shape[0]
    heads_per_group = group // HEAD_DIM

    def per_head_rows(t):
        return jnp.concatenate(
            [jnp.where(_head_lanes(t.shape, hh), t, jnp.zeros_like(t)) for hh in range(heads_per_group)], axis=0)

    def retention_scores(c0, gi):
        cols = slice(gi * group, (gi + 1) * group)
        sc = lax.dot_general(rq_s[c0:c0 + ch, cols], per_head_rows(rk_s[c0:c0 + ch, cols]),
                             (((1,), (1,)), ((), ())), preferred_element_type=F32)
        return (sc * dmat_ref[gi]).astype(BF16)

    def retention_chunk(c0, gi, scores):
        cols = slice(gi * group, (gi + 1) * group)
        q = rq_s[c0:c0 + ch, cols]
        k = rk_s[c0:c0 + ch, cols]
        v = rv_s[c0:c0 + ch, cols]
        state = state_ref[gi]
        qd = (q.astype(F32) * qdec_ref[:, cols]).astype(BF16)
        out = (jnp.dot(qd, state.astype(BF16), preferred_element_type=F32)
               + jnp.dot(scores, per_head_rows(v), preferred_element_type=F32))
        kd = (k.astype(F32) * kdec_ref[:, cols]).astype(BF16)
        kv = lax.dot_general(kd, v, (((0,), (0,)), ((), ())), preferred_element_type=F32)
        state_ref[gi] = state * cdec_ref[:, cols] + kv * bd_ref[...]
        return out

    def retention_norm(c0, gi, out):
        cols = slice(gi * group, (gi + 1) * group)
        ss = jnp.dot((out * out).astype(BF16), ones_ref[...], preferred_element_type=F32)
        yn = out * lax.rsqrt(ss * (1.0 / HEAD_DIM) + EPS) * rg_ref[row, cols]
        yret_ref[0, c0:c0 + ch, cols] = (yn * sg_s[c0:c0 + ch, cols].astype(F32)).astype(yret_ref.dtype)

    sb_scores, sb_weights, sb_values, top_carry = _sb_stages(q_s, k_all, vm_all, sfx_ref, n_sb_heads)
    n_sub = tm // blk
    n_sweep = SB_LOOKBACK + 1
    sb_stage1, sb_stage2, sb_tops = {}, {}, []

    def sb_krow(s):
        return pl.multiple_of(sb_pad + t0 + (s - SB_LOOKBACK) * blk, blk)

    def sb_step(step):
        if step < n_sub:
            valid = [has_earlier_tile if step - SB_LOOKBACK + j < 0 else None for j in range(n_sweep)]
            sb_stage1[step] = sb_scores(step * blk, sb_krow(step), n_sweep, True, valid)
        s = step - 1
        if 0 <= s < n_sub:
            sb_stage2[s] = sb_weights(*sb_stage1.pop(s), n_sweep, [None] * n_sb_heads)
        s = step - 2
        if 0 <= s < n_sub:
            ws, carries = sb_stage2.pop(s)
            pv = sb_values(ws, sb_krow(s), n_sweep)
            acc_ref[s] = pv
            ysb_ref[0, s * blk:(s + 1) * blk, :] = pv.astype(ysb_ref.dtype)
            for hh in range(n_sb_heads):
                c_ref[s, hh] = carries[hh]
            sb_tops.append(top_carry(carries))

    ret_chunks = [(c0, gi) for c0 in range(0, tm, ch) for gi in range(retw // group)]
    project_conv()
    project_sb_qk()
    project_sb_v()
    conv_starts = list(range(0, tm, CONV_BLOCK))
    projections = [project_ret_q, project_ret_k, project_ret_v, project_ret_gate]
    sb_steps = list(range(n_sub + 2))
    for i in range(max(len(conv_starts), len(projections))):
        if i < len(conv_starts):
            conv_block(conv_starts[i])
        if i < len(projections):
            projections[i]()
    pad_ref[0:halo, :] = pad_ref[tm:tm + halo, :]
    scores, outs = {}, {}
    for i in range(len(ret_chunks) + 2):
        if i < len(ret_chunks):
            scores[i] = retention_scores(*ret_chunks[i])
        if sb_steps:
            sb_step(sb_steps.pop(0))
        if 0 <= i - 1 < len(ret_chunks):
            outs[i - 1] = retention_chunk(*ret_chunks[i - 1], scores.pop(i - 1))
        if 0 <= i - 2 < len(ret_chunks):
            retention_norm(*ret_chunks[i - 2], outs.pop(i - 2))
    for step in sb_steps:
        sb_step(step)

    @pl.when(jnp.max(top_carry(sb_tops)) > SB_DEAD_LOG2)
    def _():
        def sub_block(s, _):
            off = pl.multiple_of(s * blk, blk)
            q0 = t0 + off

            def cond(state):
                i, live = state
                return jnp.logical_and(i * blk <= q0, live)

            def body(state):
                i, _ = state
                krow = pl.multiple_of(sb_pad + q0 - i * blk, blk)
                ws, carries = sb_weights(*sb_scores(off, krow, 1, False), 1,
                                         [c_ref[s, hh] for hh in range(n_sb_heads)])
                acc_ref[s] += sb_values(ws, krow, 1)
                for hh in range(n_sb_heads):
                    c_ref[s, hh] = carries[hh]
                return i + 1, jnp.max(top_carry(carries)) > SB_DEAD_LOG2

            live = jnp.max(top_carry([c_ref[s, hh] for hh in range(n_sb_heads)])) > SB_DEAD_LOG2
            lax.while_loop(cond, body, (jnp.int32(n_sweep), live))
            ysb_ref[0, pl.ds(off, blk), :] = acc_ref[s].astype(ysb_ref.dtype)
            return 0

        lax.fori_loop(0, n_sub, sub_block, 0)


def _slab_specs(w, layer, n_steps, steps_per_row):
    rows, cols = w.shape[1:]
    assert rows % n_steps == 0 and (rows // n_steps) % 16 == 0
    slab = rows // n_steps
    return (pl.BlockSpec((None, slab, cols), lambda i, j: (layer, i * steps_per_row + j, 0)),
            pl.BlockSpec((slab, cols), lambda i, j: (i * steps_per_row + j, 0)))


def _mixers(layer, x, g, w, ones_bd, cos_t, sin_t, gq, gk, pb, shifts, dw_w, dw_b, ln_g, ln_b,
            qdec, kdec, dmat, cdec, bd, ret_g, sfx, next_weights, widths):
    B, T, D = x.shape
    sbw, retw, cw = widths
    tm = min(TOKEN_TILE, T)
    blk = SB_BLOCK
    assert tm % qdec.shape[0] == 0 and tm % CONV_BLOCK == 0 and tm % blk == 0
    n_sb_heads = sbw // HEAD_DIM
    key_rows = SB_LOOKBACK * blk + T
    grid = (B, T // tm)
    tok = lambda width: pl.BlockSpec((1, tm, width), lambda b, t: (b, t, 0))
    tab = pl.BlockSpec((tm, cos_t.shape[1]), lambda b, t: (t, 0))
    per_layer = lambda a: _layer_spec(a.shape, layer)
    shared = lambda a: _const_spec(a.shape)
    slabs = [_slab_specs(a, layer, grid[0] * grid[1], grid[1]) for a in next_weights]
    out_widths = (sbw, retw, cw)
    vmem = (2 * tm * D * 4 + w.size * 2 + dmat.size * 4 + 4 * tm * retw * 2 + (1 + n_sb_heads) * key_rows * sbw * 2
            + 2 * sum(out_widths) * tm * 2 + 28 * 1024 * 1024)
    return pl.pallas_call(
        functools.partial(_mixers_kernel, layer=layer, widths=widths),
        grid=grid,
        in_specs=[tok(D), shared(g), shared(w), shared(ones_bd), tab, tab,
                  shared(gq), shared(gk), shared(pb), shared(shifts), per_layer(dw_w), shared(dw_b),
                  shared(ln_g), shared(ln_b), shared(qdec), shared(kdec), shared(dmat), shared(cdec),
                  shared(bd), shared(ret_g), shared(sfx)] + [s[0] for s in slabs],
        out_specs=[tok(wd) for wd in out_widths] + [s[1] for s in slabs],
        out_shape=[jax.ShapeDtypeStruct((B, T, wd), BF16) for wd in out_widths]
                  + [jax.ShapeDtypeStruct(a.shape[1:], BF16) for a in next_weights],
        scratch_shapes=[pltpu.VMEM((tm + CONV_WINDOW - CONV_BLOCK, cw), BF16),
                        pltpu.VMEM((retw // bd.shape[0], bd.shape[0], bd.shape[0]), F32)]
                       + [pltpu.VMEM((tm, retw), BF16)] * 4
                       + [pltpu.VMEM((tm, sbw), BF16), pltpu.VMEM((key_rows, sbw), BF16),
                          pltpu.VMEM((n_sb_heads, key_rows, sbw), BF16),
                          pltpu.VMEM((tm // blk, blk, sbw), F32),
                          pltpu.VMEM((tm // blk, n_sb_heads, blk, blk), F32)],
        compiler_params=pltpu.CompilerParams(
            dimension_semantics=("arbitrary", "arbitrary"), vmem_limit_bytes=_vmem_limit(vmem)),
        name="mixers",
    )(x, g, w, ones_bd, cos_t, sin_t, gq, gk, pb, shifts, dw_w, dw_b, ln_g, ln_b,
      qdec, kdec, dmat, cdec, bd, ret_g, sfx, *next_weights)


def _out_mlp_kernel(x_ref, ysb_ref, yret_ref, yconv_ref, wo_ref, g_ref, w1_ref, w2_ref, *rest, layer):
    if len(rest) == 1:
        (o_ref,) = rest
    else:
        w_next_f_ref, col_scale_ref, o_ref, w_next_b_ref = rest
        w_next_b_ref[...] = (w_next_f_ref[...] * col_scale_ref[...]).astype(w_next_b_ref.dtype)
    mixed = jnp.concatenate([ysb_ref[0], yret_ref[0], yconv_ref[0]], axis=1)
    x1 = x_ref[0] + jnp.dot(mixed, wo_ref[...], preferred_element_type=F32)
    ms = jnp.mean(x1 * x1, axis=-1, keepdims=True)
    h = (x1 * lax.rsqrt(ms + EPS) * g_ref[layer:layer + 1, :]).astype(BF16)
    acc = x1
    for c0 in range(0, w1_ref.shape[1], FF_CHUNK):
        f = jnp.maximum(jnp.dot(h, w1_ref[:, c0:c0 + FF_CHUNK], preferred_element_type=F32), 0.0)
        acc = acc + jnp.dot((f * f).astype(BF16), w2_ref[c0:c0 + FF_CHUNK, :], preferred_element_type=F32)
    o_ref[0] = acc


def _out_mlp(layer, x, ysb, yret, yconv, wo, g, w1, w2, w_in, col_scale):
    B, T, D = x.shape
    tm = min(MLP_TILE, T)
    grid = (B, T // tm)
    tok = lambda width: pl.BlockSpec((1, tm, width), lambda b, t: (b, t, 0))
    dff = w1.shape[1]
    vmem = (D * D + 2 * D * dff) * 2 + 4 * tm * D * 4 + 4 * tm * D * 2 + tm * FF_CHUNK * 12 + 12 * 1024 * 1024
    in_specs = [tok(D), tok(ysb.shape[2]), tok(yret.shape[2]), tok(yconv.shape[2]),
                _const_spec(wo.shape), _const_spec(g.shape), _const_spec(w1.shape), _const_spec(w2.shape)]
    out_specs, out_shape, extra = [tok(D)], [jax.ShapeDtypeStruct((B, T, D), F32)], ()
    if layer + 1 < w_in.shape[0]:
        f_spec, b_spec = _slab_specs(w_in, layer + 1, grid[0] * grid[1], grid[1])
        in_specs += [f_spec, _const_spec(col_scale.shape)]
        out_specs.append(b_spec)
        out_shape.append(jax.ShapeDtypeStruct(w_in.shape[1:], BF16))
        extra = (w_in, col_scale)
    return pl.pallas_call(
        functools.partial(_out_mlp_kernel, layer=layer),
        grid=grid,
        in_specs=in_specs,
        out_specs=out_specs,
        out_shape=out_shape,
        compiler_params=pltpu.CompilerParams(
            dimension_semantics=("arbitrary", "arbitrary"), vmem_limit_bytes=_vmem_limit(vmem)),
        name="out_mlp",
    )(x, ysb, yret, yconv, wo, g, w1, w2, *extra)


def _block_diag_ones(n, block):
    i = np.arange(n)
    return (i[:, None] // block == i[None, :] // block).astype(np.float32)


def _shift_matrices():
    r = np.arange(CONV_BLOCK + CONV_HALO)[None, :, None]
    j = np.arange(CONV_WINDOW)[None, None, :]
    b = np.arange(1, CONV_SHIFTS)[:, None, None]
    return (j == r + b).astype(np.float32)


def _suffix_sum_matrix(blk):
    j = np.arange(2 * blk)[:, None] % blk
    s = np.arange(2 * blk)[None, :]
    return -((s >= blk) | (j > s)).astype(np.float32)


def _rotary_tables(T):
    inv = 1.0 / (ROPE_BASE ** np.linspace(0.0, 1.0, HEAD_DIM // 2))
    ang = np.arange(T)[:, None] * inv[None, :]
    cos = np.repeat(np.cos(ang), 2, axis=1)
    sin = np.repeat(np.sin(ang), 2, axis=1)
    sign = np.where(np.arange(HEAD_DIM) % 2 == 0, -1.0, 1.0)
    reps = (1, V7X_LANES // HEAD_DIM)
    return jnp.asarray(np.tile(cos, reps), F32), jnp.asarray(np.tile(sin * sign[None, :], reps), F32)


def _retention_tables(n_heads, ch):
    log_g = np.log(1.0 - np.exp2(-5.0 - np.arange(n_heads)))
    j = np.arange(ch, dtype=np.float64)
    per_lane = lambda t: np.repeat(t, HEAD_DIM, axis=-1)
    qdec = per_lane(np.exp(log_g[None, :] * (j + 1.0)[:, None]))
    kdec = per_lane(np.exp(log_g[None, :] * (ch - 1.0 - j)[:, None]))
    cdec = per_lane(np.exp(log_g * ch)[None, :])
    rel = j[:, None] - j[None, :]
    dmat = np.where(rel >= 0, np.exp(log_g[:, None, None] * np.maximum(rel, 0.0)), 0.0)
    per_group = V7X_MXU_DIM // HEAD_DIM
    dmat = dmat.reshape(n_heads // per_group, per_group, ch, ch).transpose(0, 2, 1, 3)
    dmat = dmat.reshape(n_heads // per_group, ch, per_group * ch)
    return tuple(jnp.asarray(t, F32) for t in (qdec, kdec, dmat, cdec))


def kernel(x, mix_norm_g, w_in, sb_q_norm_g, sb_k_norm_g, ret_norm_g, conv_pw_b, conv_dw_w, conv_dw_b,
           conv_ln_g, conv_ln_b, w_out, mlp_norm_g, w_ff1, w_ff2):
    B, T, D = x.shape
    depth = w_in.shape[0]
    sbw = (D // 256) * HEAD_DIM
    retw = (D // 128) * HEAD_DIM
    cw = D - sbw - retw
    assert all(T % tile == 0 or T < tile for tile in (TOKEN_TILE, MLP_TILE))
    assert T % RET_CHUNK == 0 and T % SB_BLOCK == 0
    widths = (sbw, retw, cw)
    n_ret_heads = retw // HEAD_DIM

    ones_bd = jnp.asarray(_block_diag_ones(V7X_MXU_DIM, HEAD_DIM), BF16)
    bd_mask = jnp.asarray(_block_diag_ones(V7X_MXU_DIM, HEAD_DIM), F32)
    sfx = jnp.asarray(_suffix_sum_matrix(SB_BLOCK), BF16)
    shifts = jnp.asarray(_shift_matrices(), BF16)
    cos_t, sin_t = _rotary_tables(T)
    qdec, kdec, dmat, cdec = _retention_tables(n_ret_heads, RET_CHUNK)
    rk0 = 3 * sbw + retw
    col_scale = np.ones((1, w_in.shape[2]), np.float32)
    col_scale[:, rk0:rk0 + retw] = HEAD_DIM ** -0.5
    col_scale = jnp.asarray(col_scale)

    f32 = lambda a: a.astype(F32)
    mix_g, mlp_g, ret_g, pw_b = f32(mix_norm_g), f32(mlp_norm_g), f32(ret_norm_g), f32(conv_pw_b)
    gq, gk = (f32(jnp.tile(g, (1, sbw // HEAD_DIM))) for g in (sb_q_norm_g, sb_k_norm_g))
    dw_w, dw_b, ln_g, ln_b = f32(conv_dw_w), f32(conv_dw_b), f32(conv_ln_g), f32(conv_ln_b)
    w_in_b = (w_in[0] * col_scale).astype(BF16)
    for l in range(depth):
        y_sb, y_ret, y_conv, w_out_b, w_ff1_b, w_ff2_b = _mixers(
            l, x, mix_g, w_in_b, ones_bd, cos_t, sin_t, gq, gk, pw_b, shifts, dw_w, dw_b, ln_g, ln_b,
            qdec, kdec, dmat, cdec, bd_mask, ret_g, sfx, (w_out, w_ff1, w_ff2), widths)
        x, *w_next = _out_mlp(l, x, y_sb, y_ret, y_conv, w_out_b, mlp_g, w_ff1_b, w_ff2_b, w_in, col_scale)
        if w_next:
            (w_in_b,) = w_next
    return x
```

```python
import functools

import jax
import jax.numpy as jnp
import numpy as np
from jax import lax
from jax.experimental import pallas as pl
from jax.experimental.pallas import tpu as pltpu

F32 = jnp.float32
BF16 = jnp.bfloat16

HEAD_DIM = 64
CONV_KERNEL = 31
EPS = 1e-6
ROPE_BASE = 10000.0
LOG2E = 1.4426950408889634
LN2 = 0.6931471805599453

V7X_LANES = 128
V7X_MXU_DIM = 256
V7X_VMEM_BYTES = 64 * 1024 * 1024
MIB = 1024 * 1024
VMEM_RESERVE_BYTES = 4 * MIB
MIXERS_TEMP_BYTES = 28 * MIB
MLP_TEMP_BYTES = 12 * MIB

SB_DEAD_LOG2 = -150.0
SB_MASKED_SCORE = -1e30

TOKEN_TILE = 1024
MLP_TILE = 1024
SB_BLOCK = 128
SB_LOOKBACK = 2
RET_CHUNK = 128
CONV_HALO = 32
CONV_BLOCK = 128
CONV_WINDOW = 256
CONV_SHIFTS = 8
FF_CHUNK = 1024


def _vmem_limit(nbytes):
    return int(min(nbytes, V7X_VMEM_BYTES - VMEM_RESERVE_BYTES))


def _const_spec(shape):
    nd = len(shape)
    return pl.BlockSpec(shape, lambda *_: (0,) * nd, pipeline_mode=pl.Buffered(1))


def _layer_spec(shape, layer):
    nd = len(shape)
    return pl.BlockSpec((None,) + tuple(shape[1:]), lambda *_: (layer,) + (0,) * (nd - 1),
                        pipeline_mode=pl.Buffered(1))


def _head_lanes(shape, h):
    lane = lax.broadcasted_iota(jnp.int32, shape, 1)
    return (lane >= h * HEAD_DIM) & (lane < (h + 1) * HEAD_DIM)


def _sb_stages(q_ref, k_ref, vm_ref, m_ref, n_heads):
    blk = SB_BLOCK
    heads = tuple(range(n_heads))
    causal = (lax.broadcasted_iota(jnp.int32, (blk, blk), 1) < lax.broadcasted_iota(jnp.int32, (blk, blk), 0))

    def masked_q(off):
        q = q_ref[pl.ds(off, blk), :]
        return jnp.concatenate([jnp.where(_head_lanes(q.shape, h), q, jnp.zeros_like(q)) for h in heads], axis=0)

    def scores(off, krow, nblk, diag_last, valid=None):
        kwin = k_ref[pl.ds(krow, nblk * blk), :]
        z_all = lax.dot_general(masked_q(off), kwin, (((1,), (1,)), ((), ())), preferred_element_type=F32)
        log_beta, split = {}, []
        for h in heads:
            for j in range(nblk):
                z = z_all[h * blk:(h + 1) * blk, j * blk:(j + 1) * blk]
                if diag_last and j == nblk - 1:
                    z = jnp.where(causal, z, SB_MASKED_SCORE)
                elif valid is not None and valid[j] is not None:
                    z = jnp.where(valid[j], z, SB_MASKED_SCORE)
                sp = jnp.maximum(z, 0.0) + jnp.log(1.0 + jnp.exp2(-jnp.abs(z))) * (1.0 / LN2)
                log_beta[h, j] = z - sp
                hi = sp.astype(BF16)
                lo = (sp - hi.astype(F32)).astype(BF16)
                split.append(jnp.concatenate([hi, lo], axis=1))
        return jnp.concatenate(split, axis=0), log_beta

    def weights(split, log_beta, nblk, carries):
        r = jnp.dot(split, m_ref[...], preferred_element_type=F32)
        ws, out_carries = [], []
        for h in heads:
            carry = carries[h]
            w = [None] * nblk
            for j in reversed(range(nblk)):
                rr = r[(h * nblk + j) * blk:(h * nblk + j + 1) * blk, :]
                arg = rr[:, :blk] + log_beta[h, j]
                if carry is not None:
                    arg = arg + carry
                w[j] = jnp.exp2(arg).astype(BF16)
                carry = rr[:, blk:] if carry is None else carry + rr[:, blk:]
            out_carries.append(carry)
            ws.extend(w)
        return jnp.concatenate(ws, axis=1), out_carries

    def values(ws, krow, nblk):
        vs = [vm_ref[h, pl.ds(krow, nblk * blk), :] for h in heads]
        return jnp.dot(ws, jnp.concatenate(vs, axis=0), preferred_element_type=F32)

    def top_carry(carries):
        top = carries[0]
        for c in carries[1:]:
            top = jnp.maximum(top, c)
        return top

    return scores, weights, values, top_carry


def _mixers_kernel(x_ref, g_ref, w_ref, ones_ref, cos_ref, sin_ref, gq_ref, gk_ref, pb_ref,
                   shift_ref, dw_ref, db_ref, lg_ref, lb_ref,
                   qdec_ref, kdec_ref, dmat_ref, cdec_ref, bd_ref, rg_ref, sfx_ref, wo_f_ref, w1_f_ref, w2_f_ref,
                   ysb_ref, yret_ref, yconv_ref, wo_b_ref, w1_b_ref, w2_b_ref,
                   pad_ref, state_ref, rq_s, rk_s, rv_s, sg_s, q_s, k_all, vm_all, acc_ref, c_ref,
                   *, layer, widths):
    for f_ref, b_ref in ((wo_f_ref, wo_b_ref), (w1_f_ref, w1_b_ref), (w2_f_ref, w2_b_ref)):
        b_ref[...] = f_ref[...].astype(b_ref.dtype)
    row = slice(layer, layer + 1)
    sbw, retw, cw = widths
    tm = x_ref.shape[1]
    halo = CONV_HALO
    blk = SB_BLOCK
    sb_pad = SB_LOOKBACK * blk
    n_sb_heads = sbw // HEAD_DIM
    t0 = pl.multiple_of(pl.program_id(1) * tm, tm)
    has_earlier_tile = pl.program_id(1) > 0

    @pl.when(pl.program_id(1) == 0)
    def _():
        state_ref[...] = jnp.zeros_like(state_ref)
        pad_ref[0:halo, :] = jnp.zeros((halo, cw), pad_ref.dtype)
        pad_ref[halo + tm:, :] = jnp.zeros((pad_ref.shape[0] - halo - tm, cw), pad_ref.dtype)
        k_all[0:sb_pad, :] = jnp.zeros((sb_pad, sbw), k_all.dtype)
        vm_all[:, 0:sb_pad, :] = jnp.zeros((n_sb_heads, sb_pad, sbw), vm_all.dtype)

    x = x_ref[0]
    ms = jnp.mean(x * x, axis=-1, keepdims=True)
    h = (x * lax.rsqrt(ms + EPS) * g_ref[row, :]).astype(BF16)

    def proj(c0, width):
        return jnp.dot(h, w_ref[:, c0:c0 + width], preferred_element_type=F32)

    def head_norm(u, gain):
        ss = jnp.dot((u * u).astype(BF16), ones_ref[...], preferred_element_type=F32)
        return u * lax.rsqrt(ss * (1.0 / HEAD_DIM) + EPS) * gain

    def rotary(u):
        n = cos_ref.shape[1]
        cos, sin = cos_ref[...], sin_ref[...]
        even = (lax.broadcasted_iota(jnp.int32, cos.shape, 1) & 1) == 0
        out = []
        for c0 in range(0, u.shape[1], n):
            uc = u[:, c0:c0 + n]
            nxt = pltpu.roll(uc, n - 1, axis=1)
            prv = pltpu.roll(uc, 1, axis=1)
            out.append(uc * cos + jnp.where(even, nxt, prv) * sin)
        return jnp.concatenate(out, axis=1)

    c_q, c_v = 0, 2 * sbw
    c_rq, c_rk, c_rv, c_rg = (3 * sbw + i * retw for i in range(4))
    c_a = 3 * sbw + 4 * retw

    def project_conv():
        ag = proj(c_a, 2 * cw) + pb_ref[row, :]
        pad_ref[halo:halo + tm, :] = (ag[:, :cw] * jax.nn.sigmoid(ag[:, cw:])).astype(pad_ref.dtype)

    def project_ret_gate():
        gate = proj(c_rg, retw)
        sg_s[...] = (gate * jax.nn.sigmoid(gate)).astype(BF16)

    def project_ret_q():
        rq_s[...] = rotary(proj(c_rq, retw)).astype(BF16)

    def project_ret_k():
        rk_s[...] = rotary(proj(c_rk, retw)).astype(BF16)

    def project_ret_v():
        rv_s[...] = proj(c_rv, retw).astype(BF16)

    def project_sb_qk():
        qk = proj(c_q, 2 * sbw)
        q_s[...] = (head_norm(qk[:, :sbw], gq_ref[row, :]) * (LOG2E * HEAD_DIM ** -0.5)).astype(BF16)
        k_all[pl.ds(sb_pad + t0, tm), :] = head_norm(qk[:, sbw:], gk_ref[row, :]).astype(BF16)

    def project_sb_v():
        v = proj(c_v, sbw).astype(BF16)
        for hh in range(n_sb_heads):
            vm_all[hh, pl.ds(sb_pad + t0, tm), :] = jnp.where(_head_lanes(v.shape, hh), v, jnp.zeros_like(v))

    n_shift = CONV_SHIFTS
    first = halo - (CONV_KERNEL - 1)

    def conv_block(r0):
        window = pad_ref[r0:r0 + CONV_WINDOW, :]
        acc = jnp.zeros((CONV_BLOCK, cw), F32) + db_ref[row, :]
        for shift in range(n_shift):
            if shift == 0:
                moved = window[:CONV_BLOCK + halo].astype(F32)
            else:
                moved = jnp.dot(shift_ref[shift - 1], window, preferred_element_type=F32)
            for tap in range(CONV_KERNEL):
                if (first + tap) % n_shift == shift:
                    a = (first + tap) // n_shift * n_shift
                    acc = acc + moved[a:a + CONV_BLOCK, :] * dw_ref[tap:tap + 1, :]
        mu = jnp.mean(acc, axis=-1, keepdims=True)
        xc = acc - mu
        var = jnp.mean(xc * xc, axis=-1, keepdims=True)
        yn = xc * lax.rsqrt(var + EPS) * lg_ref[row, :] + lb_ref[row, :]
        yconv_ref[0, r0:r0 + CONV_BLOCK, :] = (yn * jax.nn.sigmoid(yn)).astype(yconv_ref.dtype)

    ch = qdec_ref.shape[0]
    group = bd_ref.shape[0]
    heads_per_group = group // HEAD_DIM

    def per_head_rows(t):
        return jnp.concatenate(
            [jnp.where(_head_lanes(t.shape, hh), t, jnp.zeros_like(t)) for hh in range(heads_per_group)], axis=0)

    def retention_scores(c0, gi):
        cols = slice(gi * group, (gi + 1) * group)
        sc = lax.dot_general(rq_s[c0:c0 + ch, cols], per_head_rows(rk_s[c0:c0 + ch, cols]),
                             (((1,), (1,)), ((), ())), preferred_element_type=F32)
        return (sc * dmat_ref[gi]).astype(BF16)

    def retention_chunk(c0, gi, scores):
        cols = slice(gi * group, (gi + 1) * group)
        q = rq_s[c0:c0 + ch, cols]
        k = rk_s[c0:c0 + ch, cols]
        v = rv_s[c0:c0 + ch, cols]
        state = state_ref[gi]
        qd = (q.astype(F32) * qdec_ref[:, cols]).astype(BF16)
        out = (jnp.dot(qd, state.astype(BF16), preferred_element_type=F32)
               + jnp.dot(scores, per_head_rows(v), preferred_element_type=F32))
        kd = (k.astype(F32) * kdec_ref[:, cols]).astype(BF16)
        kv = lax.dot_general(kd, v, (((0,), (0,)), ((), ())), preferred_element_type=F32)
        state_ref[gi] = state * cdec_ref[:, cols] + kv * bd_ref[...]
        return out

    def retention_norm(c0, gi, out):
        cols = slice(gi * group, (gi + 1) * group)
        ss = jnp.dot((out * out).astype(BF16), ones_ref[...], preferred_element_type=F32)
        yn = out * lax.rsqrt(ss * (1.0 / HEAD_DIM) + EPS) * rg_ref[row, cols]
        yret_ref[0, c0:c0 + ch, cols] = (yn * sg_s[c0:c0 + ch, cols].astype(F32)).astype(yret_ref.dtype)

    sb_scores, sb_weights, sb_values, top_carry = _sb_stages(q_s, k_all, vm_all, sfx_ref, n_sb_heads)
    n_sub = tm // blk
    n_sweep = SB_LOOKBACK + 1
    sb_stage1, sb_stage2, sb_tops = {}, {}, []

    def sb_krow(s):
        return pl.multiple_of(sb_pad + t0 + (s - SB_LOOKBACK) * blk, blk)

    def sb_step(step):
        if step < n_sub:
            valid = [has_earlier_tile if step - SB_LOOKBACK + j < 0 else None for j in range(n_sweep)]
            sb_stage1[step] = sb_scores(step * blk, sb_krow(step), n_sweep, True, valid)
        s = step - 1
        if 0 <= s < n_sub:
            sb_stage2[s] = sb_weights(*sb_stage1.pop(s), n_sweep, [None] * n_sb_heads)
        s = step - 2
        if 0 <= s < n_sub:
            ws, carries = sb_stage2.pop(s)
            pv = sb_values(ws, sb_krow(s), n_sweep)
            acc_ref[s] = pv
            ysb_ref[0, s * blk:(s + 1) * blk, :] = pv.astype(ysb_ref.dtype)
            for hh in range(n_sb_heads):
                c_ref[s, hh] = carries[hh]
            sb_tops.append(top_carry(carries))

    ret_chunks = [(c0, gi) for c0 in range(0, tm, ch) for gi in range(retw // group)]
    project_conv()
    project_sb_qk()
    project_sb_v()
    conv_starts = list(range(0, tm, CONV_BLOCK))
    projections = [project_ret_q, project_ret_k, project_ret_v, project_ret_gate]
    sb_steps = list(range(n_sub + 2))
    for i in range(max(len(conv_starts), len(projections))):
        if i < len(conv_starts):
            conv_block(conv_starts[i])
        if i < len(projections):
            projections[i]()
    pad_ref[0:halo, :] = pad_ref[tm:tm + halo, :]
    scores, outs = {}, {}
    for i in range(len(ret_chunks) + 2):
        if i < len(ret_chunks):
            scores[i] = retention_scores(*ret_chunks[i])
        if sb_steps:
            sb_step(sb_steps.pop(0))
        if 0 <= i - 1 < len(ret_chunks):
            outs[i - 1] = retention_chunk(*ret_chunks[i - 1], scores.pop(i - 1))
        if 0 <= i - 2 < len(ret_chunks):
            retention_norm(*ret_chunks[i - 2], outs.pop(i - 2))
    for step in sb_steps:
        sb_step(step)

    @pl.when(jnp.max(top_carry(sb_tops)) > SB_DEAD_LOG2)
    def _():
        def sub_block(s, _):
            off = pl.multiple_of(s * blk, blk)
            q0 = t0 + off

            def cond(state):
                i, live = state
                return jnp.logical_and(i * blk <= q0, live)

            def body(state):
                i, _ = state
                krow = pl.multiple_of(sb_pad + q0 - i * blk, blk)
                ws, carries = sb_weights(*sb_scores(off, krow, 1, False), 1,
                                         [c_ref[s, hh] for hh in range(n_sb_heads)])
                acc_ref[s] += sb_values(ws, krow, 1)
                for hh in range(n_sb_heads):
                    c_ref[s, hh] = carries[hh]
                return i + 1, jnp.max(top_carry(carries)) > SB_DEAD_LOG2

            live = jnp.max(top_carry([c_ref[s, hh] for hh in range(n_sb_heads)])) > SB_DEAD_LOG2
            lax.while_loop(cond, body, (jnp.int32(n_sweep), live))
            ysb_ref[0, pl.ds(off, blk), :] = acc_ref[s].astype(ysb_ref.dtype)
            return 0

        lax.fori_loop(0, n_sub, sub_block, 0)


def _slab_specs(w, layer, n_steps, steps_per_row):
    rows, cols = w.shape[1:]
    assert rows % n_steps == 0 and (rows // n_steps) % 16 == 0
    slab = rows // n_steps
    return (pl.BlockSpec((None, slab, cols), lambda i, j: (layer, i * steps_per_row + j, 0)),
            pl.BlockSpec((slab, cols), lambda i, j: (i * steps_per_row + j, 0)))


def _mixers(layer, x, g, w, ones_bd, cos_t, sin_t, gq, gk, pb, shifts, dw_w, dw_b, ln_g, ln_b,
            qdec, kdec, dmat, cdec, bd, ret_g, sfx, next_weights, widths):
    B, T, D = x.shape
    sbw, retw, cw = widths
    tm = min(TOKEN_TILE, T)
    blk = SB_BLOCK
    assert tm % qdec.shape[0] == 0 and tm % CONV_BLOCK == 0 and tm % blk == 0
    n_sb_heads = sbw // HEAD_DIM
    key_rows = SB_LOOKBACK * blk + T
    grid = (B, T // tm)
    tok = lambda width: pl.BlockSpec((1, tm, width), lambda b, t: (b, t, 0))
    tab = pl.BlockSpec((tm, cos_t.shape[1]), lambda b, t: (t, 0))
    per_layer = lambda a: _layer_spec(a.shape, layer)
    shared = lambda a: _const_spec(a.shape)
    slabs = [_slab_specs(a, layer, grid[0] * grid[1], grid[1]) for a in next_weights]
    out_widths = (sbw, retw, cw)
    vmem = (2 * tm * D * 4 + w.size * 2 + dmat.size * 4 + 4 * tm * retw * 2 + (1 + n_sb_heads) * key_rows * sbw * 2
            + 2 * sum(out_widths) * tm * 2 + MIXERS_TEMP_BYTES)
    return pl.pallas_call(
        functools.partial(_mixers_kernel, layer=layer, widths=widths),
        grid=grid,
        in_specs=[tok(D), shared(g), shared(w), shared(ones_bd), tab, tab,
                  shared(gq), shared(gk), shared(pb), shared(shifts), per_layer(dw_w), shared(dw_b),
                  shared(ln_g), shared(ln_b), shared(qdec), shared(kdec), shared(dmat), shared(cdec),
                  shared(bd), shared(ret_g), shared(sfx)] + [s[0] for s in slabs],
        out_specs=[tok(wd) for wd in out_widths] + [s[1] for s in slabs],
        out_shape=[jax.ShapeDtypeStruct((B, T, wd), BF16) for wd in out_widths]
                  + [jax.ShapeDtypeStruct(a.shape[1:], BF16) for a in next_weights],
        scratch_shapes=[pltpu.VMEM((tm + CONV_WINDOW - CONV_BLOCK, cw), BF16),
                        pltpu.VMEM((retw // bd.shape[0], bd.shape[0], bd.shape[0]), F32)]
                       + [pltpu.VMEM((tm, retw), BF16)] * 4
                       + [pltpu.VMEM((tm, sbw), BF16), pltpu.VMEM((key_rows, sbw), BF16),
                          pltpu.VMEM((n_sb_heads, key_rows, sbw), BF16),
                          pltpu.VMEM((tm // blk, blk, sbw), F32),
                          pltpu.VMEM((tm // blk, n_sb_heads, blk, blk), F32)],
        compiler_params=pltpu.CompilerParams(
            dimension_semantics=("arbitrary", "arbitrary"), vmem_limit_bytes=_vmem_limit(vmem)),
        name="mixers",
    )(x, g, w, ones_bd, cos_t, sin_t, gq, gk, pb, shifts, dw_w, dw_b, ln_g, ln_b,
      qdec, kdec, dmat, cdec, bd, ret_g, sfx, *next_weights)


def _out_mlp_kernel(x_ref, ysb_ref, yret_ref, yconv_ref, wo_ref, g_ref, w1_ref, w2_ref, *rest, layer):
    if len(rest) == 1:
        (o_ref,) = rest
    else:
        w_next_f_ref, col_scale_ref, o_ref, w_next_b_ref = rest
        w_next_b_ref[...] = (w_next_f_ref[...] * col_scale_ref[...]).astype(w_next_b_ref.dtype)
    mixed = jnp.concatenate([ysb_ref[0], yret_ref[0], yconv_ref[0]], axis=1)
    x1 = x_ref[0] + jnp.dot(mixed, wo_ref[...], preferred_element_type=F32)
    ms = jnp.mean(x1 * x1, axis=-1, keepdims=True)
    h = (x1 * lax.rsqrt(ms + EPS) * g_ref[layer:layer + 1, :]).astype(BF16)
    acc = x1
    for c0 in range(0, w1_ref.shape[1], FF_CHUNK):
        f = jnp.maximum(jnp.dot(h, w1_ref[:, c0:c0 + FF_CHUNK], preferred_element_type=F32), 0.0)
        acc = acc + jnp.dot((f * f).astype(BF16), w2_ref[c0:c0 + FF_CHUNK, :], preferred_element_type=F32)
    o_ref[0] = acc


def _out_mlp(layer, x, ysb, yret, yconv, wo, g, w1, w2, w_in, col_scale):
    B, T, D = x.shape
    tm = min(MLP_TILE, T)
    grid = (B, T // tm)
    tok = lambda width: pl.BlockSpec((1, tm, width), lambda b, t: (b, t, 0))
    dff = w1.shape[1]
    vmem = (D * D + 2 * D * dff) * 2 + 4 * tm * D * 4 + 4 * tm * D * 2 + tm * FF_CHUNK * 12 + MLP_TEMP_BYTES
    in_specs = [tok(D), tok(ysb.shape[2]), tok(yret.shape[2]), tok(yconv.shape[2]),
                _const_spec(wo.shape), _const_spec(g.shape), _const_spec(w1.shape), _const_spec(w2.shape)]
    out_specs, out_shape, extra = [tok(D)], [jax.ShapeDtypeStruct((B, T, D), F32)], ()
    if layer + 1 < w_in.shape[0]:
        f_spec, b_spec = _slab_specs(w_in, layer + 1, grid[0] * grid[1], grid[1])
        in_specs += [f_spec, _const_spec(col_scale.shape)]
        out_specs.append(b_spec)
        out_shape.append(jax.ShapeDtypeStruct(w_in.shape[1:], BF16))
        extra = (w_in, col_scale)
    return pl.pallas_call(
        functools.partial(_out_mlp_kernel, layer=layer),
        grid=grid,
        in_specs=in_specs,
        out_specs=out_specs,
        out_shape=out_shape,
        compiler_params=pltpu.CompilerParams(
            dimension_semantics=("arbitrary", "arbitrary"), vmem_limit_bytes=_vmem_limit(vmem)),
        name="out_mlp",
    )(x, ysb, yret, yconv, wo, g, w1, w2, *extra)


def _block_diag_ones(n, block):
    i = np.arange(n)
    return (i[:, None] // block == i[None, :] // block).astype(np.float32)


def _shift_matrices():
    r = np.arange(CONV_BLOCK + CONV_HALO)[None, :, None]
    j = np.arange(CONV_WINDOW)[None, None, :]
    b = np.arange(1, CONV_SHIFTS)[:, None, None]
    return (j == r + b).astype(np.float32)


def _suffix_sum_matrix(blk):
    j = np.arange(2 * blk)[:, None] % blk
    s = np.arange(2 * blk)[None, :]
    return -((s >= blk) | (j > s)).astype(np.float32)


def _rotary_tables(T):
    inv = 1.0 / (ROPE_BASE ** np.linspace(0.0, 1.0, HEAD_DIM // 2))
    ang = np.arange(T)[:, None] * inv[None, :]
    cos = np.repeat(np.cos(ang), 2, axis=1)
    sin = np.repeat(np.sin(ang), 2, axis=1)
    sign = np.where(np.arange(HEAD_DIM) % 2 == 0, -1.0, 1.0)
    reps = (1, V7X_LANES // HEAD_DIM)
    return jnp.asarray(np.tile(cos, reps), F32), jnp.asarray(np.tile(sin * sign[None, :], reps), F32)


def _retention_tables(n_heads, ch):
    log_g = np.log(1.0 - np.exp2(-5.0 - np.arange(n_heads)))
    j = np.arange(ch, dtype=np.float64)
    per_lane = lambda t: np.repeat(t, HEAD_DIM, axis=-1)
    qdec = per_lane(np.exp(log_g[None, :] * (j + 1.0)[:, None]))
    kdec = per_lane(np.exp(log_g[None, :] * (ch - 1.0 - j)[:, None]))
    cdec = per_lane(np.exp(log_g * ch)[None, :])
    rel = j[:, None] - j[None, :]
    dmat = np.where(rel >= 0, np.exp(log_g[:, None, None] * np.maximum(rel, 0.0)), 0.0)
    per_group = V7X_MXU_DIM // HEAD_DIM
    dmat = dmat.reshape(n_heads // per_group, per_group, ch, ch).transpose(0, 2, 1, 3)
    dmat = dmat.reshape(n_heads // per_group, ch, per_group * ch)
    return tuple(jnp.asarray(t, F32) for t in (qdec, kdec, dmat, cdec))


def kernel(x, mix_norm_g, w_in, sb_q_norm_g, sb_k_norm_g, ret_norm_g, conv_pw_b, conv_dw_w, conv_dw_b,
           conv_ln_g, conv_ln_b, w_out, mlp_norm_g, w_ff1, w_ff2):
    B, T, D = x.shape
    depth = w_in.shape[0]
    sbw = (D // 256) * HEAD_DIM
    retw = (D // 128) * HEAD_DIM
    cw = D - sbw - retw
    assert all(T % tile == 0 or T < tile for tile in (TOKEN_TILE, MLP_TILE))
    assert T % RET_CHUNK == 0 and T % SB_BLOCK == 0
    widths = (sbw, retw, cw)
    n_ret_heads = retw // HEAD_DIM

    ones_bd = jnp.asarray(_block_diag_ones(V7X_MXU_DIM, HEAD_DIM), BF16)
    bd_mask = jnp.asarray(_block_diag_ones(V7X_MXU_DIM, HEAD_DIM), F32)
    sfx = jnp.asarray(_suffix_sum_matrix(SB_BLOCK), BF16)
    shifts = jnp.asarray(_shift_matrices(), BF16)
    cos_t, sin_t = _rotary_tables(T)
    qdec, kdec, dmat, cdec = _retention_tables(n_ret_heads, RET_CHUNK)
    rk0 = 3 * sbw + retw
    col_scale = np.ones((1, w_in.shape[2]), np.float32)
    col_scale[:, rk0:rk0 + retw] = HEAD_DIM ** -0.5
    col_scale = jnp.asarray(col_scale)

    f32 = lambda a: a.astype(F32)
    mix_g, mlp_g, ret_g, pw_b = f32(mix_norm_g), f32(mlp_norm_g), f32(ret_norm_g), f32(conv_pw_b)
    gq, gk = (f32(jnp.tile(g, (1, sbw // HEAD_DIM))) for g in (sb_q_norm_g, sb_k_norm_g))
    dw_w, dw_b, ln_g, ln_b = f32(conv_dw_w), f32(conv_dw_b), f32(conv_ln_g), f32(conv_ln_b)
    w_in_b = (w_in[0] * col_scale).astype(BF16)
    for l in range(depth):
        y_sb, y_ret, y_conv, w_out_b, w_ff1_b, w_ff2_b = _mixers(
            l, x, mix_g, w_in_b, ones_bd, cos_t, sin_t, gq, gk, pw_b, shifts, dw_w, dw_b, ln_g, ln_b,
            qdec, kdec, dmat, cdec, bd_mask, ret_g, sfx, (w_out, w_ff1, w_ff2), widths)
        x, *w_next = _out_mlp(l, x, y_sb, y_ret, y_conv, w_out_b, mlp_g, w_ff1_b, w_ff2_b, w_in, col_scale)
        if w_next:
            (w_in_b,) = w_next
    return x
```

```python
import functools

import jax
import jax.numpy as jnp
import numpy as np
from jax import lax
from jax.experimental import pallas as pl
from jax.experimental.pallas import tpu as pltpu

F32 = jnp.float32
BF16 = jnp.bfloat16

HEAD_DIM = 64
CONV_KERNEL = 31
EPS = 1e-6
ROPE_BASE = 10000.0
LOG2E = 1.4426950408889634
LN2 = 0.6931471805599453

V7X_LANES = 128
V7X_MXU_DIM = 256
V7X_VMEM_BYTES = 64 * 1024 * 1024
MIB = 1024 * 1024
VMEM_RESERVE_BYTES = 4 * MIB
MIXERS_TEMP_BYTES = 28 * MIB
MLP_TEMP_BYTES = 12 * MIB

SB_DEAD_LOG2 = -150.0
SB_MASKED_SCORE = -1e30

TOKEN_TILE = 1024
MLP_TILE = 1024
SB_BLOCK = 128
SB_LOOKBACK = 2
RET_CHUNK = 128
CONV_HALO = 32
CONV_BLOCK = 128
CONV_WINDOW = 256
CONV_SHIFTS = 8
FF_CHUNK = 1024


def _vmem_limit(nbytes):
    return int(min(nbytes, V7X_VMEM_BYTES - VMEM_RESERVE_BYTES))


def _const_spec(shape):
    nd = len(shape)
    return pl.BlockSpec(shape, lambda *_: (0,) * nd, pipeline_mode=pl.Buffered(1))


def _layer_spec(shape, layer):
    nd = len(shape)
    return pl.BlockSpec((None,) + tuple(shape[1:]), lambda *_: (layer,) + (0,) * (nd - 1),
                        pipeline_mode=pl.Buffered(1))


def _head_lanes(shape, h):
    lane = lax.broadcasted_iota(jnp.int32, shape, 1)
    return (lane >= h * HEAD_DIM) & (lane < (h + 1) * HEAD_DIM)


def _sb_stages(q_ref, k_ref, vm_ref, m_ref, n_heads):
    blk = SB_BLOCK
    heads = tuple(range(n_heads))
    causal = (lax.broadcasted_iota(jnp.int32, (blk, blk), 1) < lax.broadcasted_iota(jnp.int32, (blk, blk), 0))

    def masked_q(off):
        q = q_ref[pl.ds(off, blk), :]
        return jnp.concatenate([jnp.where(_head_lanes(q.shape, h), q, jnp.zeros_like(q)) for h in heads], axis=0)

    def scores(off, krow, nblk, diag_last, valid=None):
        kwin = k_ref[pl.ds(krow, nblk * blk), :]
        z_all = lax.dot_general(masked_q(off), kwin, (((1,), (1,)), ((), ())), preferred_element_type=F32)
        log_beta, split = {}, []
        for h in heads:
            for j in range(nblk):
                z = z_all[h * blk:(h + 1) * blk, j * blk:(j + 1) * blk]
                if diag_last and j == nblk - 1:
                    z = jnp.where(causal, z, SB_MASKED_SCORE)
                elif valid is not None and valid[j] is not None:
                    z = jnp.where(valid[j], z, SB_MASKED_SCORE)
                sp = jnp.maximum(z, 0.0) + jnp.log(1.0 + jnp.exp2(-jnp.abs(z))) * (1.0 / LN2)
                log_beta[h, j] = z - sp
                hi = sp.astype(BF16)
                lo = (sp - hi.astype(F32)).astype(BF16)
                split.append(jnp.concatenate([hi, lo], axis=1))
        return jnp.concatenate(split, axis=0), log_beta

    def weights(split, log_beta, nblk, carries):
        r = jnp.dot(split, m_ref[...], preferred_element_type=F32)
        ws, out_carries = [], []
        for h in heads:
            carry = carries[h]
            w = [None] * nblk
            for j in reversed(range(nblk)):
                rr = r[(h * nblk + j) * blk:(h * nblk + j + 1) * blk, :]
                arg = rr[:, :blk] + log_beta[h, j]
                if carry is not None:
                    arg = arg + carry
                w[j] = jnp.exp2(arg).astype(BF16)
                carry = rr[:, blk:] if carry is None else carry + rr[:, blk:]
            out_carries.append(carry)
            ws.extend(w)
        return jnp.concatenate(ws, axis=1), out_carries

    def values(ws, krow, nblk):
        vs = [vm_ref[h, pl.ds(krow, nblk * blk), :] for h in heads]
        return jnp.dot(ws, jnp.concatenate(vs, axis=0), preferred_element_type=F32)

    def top_carry(carries):
        top = carries[0]
        for c in carries[1:]:
            top = jnp.maximum(top, c)
        return top

    return scores, weights, values, top_carry


def _mixers_kernel(x_ref, g_ref, w_ref, ones_ref, cos_ref, sin_ref, gq_ref, gk_ref, pb_ref,
                   shift_ref, dw_ref, db_ref, lg_ref, lb_ref,
                   qdec_ref, kdec_ref, dmat_ref, cdec_ref, bd_ref, rg_ref, sfx_ref, wo_f_ref, w1_f_ref, w2_f_ref,
                   ysb_ref, yret_ref, yconv_ref, wo_b_ref, w1_b_ref, w2_b_ref,
                   pad_ref, state_ref, rq_s, rk_s, rv_s, sg_s, q_s, k_all, vm_all, acc_ref, c_ref,
                   *, layer, widths):
    for f_ref, b_ref in ((wo_f_ref, wo_b_ref), (w1_f_ref, w1_b_ref), (w2_f_ref, w2_b_ref)):
        b_ref[...] = f_ref[...].astype(b_ref.dtype)
    row = slice(layer, layer + 1)
    sbw, retw, cw = widths
    tm = x_ref.shape[1]
    halo = CONV_HALO
    blk = SB_BLOCK
    sb_pad = SB_LOOKBACK * blk
    n_sb_heads = sbw // HEAD_DIM
    t0 = pl.multiple_of(pl.program_id(1) * tm, tm)
    has_earlier_tile = pl.program_id(1) > 0

    @pl.when(pl.program_id(1) == 0)
    def _():
        state_ref[...] = jnp.zeros_like(state_ref)
        pad_ref[0:halo, :] = jnp.zeros((halo, cw), pad_ref.dtype)
        pad_ref[halo + tm:, :] = jnp.zeros((pad_ref.shape[0] - halo - tm, cw), pad_ref.dtype)
        k_all[0:sb_pad, :] = jnp.zeros((sb_pad, sbw), k_all.dtype)
        vm_all[:, 0:sb_pad, :] = jnp.zeros((n_sb_heads, sb_pad, sbw), vm_all.dtype)

    x = x_ref[0]
    ms = jnp.mean(x * x, axis=-1, keepdims=True)
    h = (x * lax.rsqrt(ms + EPS) * g_ref[row, :]).astype(BF16)

    def proj(c0, width):
        return jnp.dot(h, w_ref[:, c0:c0 + width], preferred_element_type=F32)

    def head_norm(u, gain):
        ss = jnp.dot((u * u).astype(BF16), ones_ref[...], preferred_element_type=F32)
        return u * lax.rsqrt(ss * (1.0 / HEAD_DIM) + EPS) * gain

    def rotary(u):
        n = cos_ref.shape[1]
        cos, sin = cos_ref[...], sin_ref[...]
        even = (lax.broadcasted_iota(jnp.int32, cos.shape, 1) & 1) == 0
        out = []
        for c0 in range(0, u.shape[1], n):
            uc = u[:, c0:c0 + n]
            nxt = pltpu.roll(uc, n - 1, axis=1)
            prv = pltpu.roll(uc, 1, axis=1)
            out.append(uc * cos + jnp.where(even, nxt, prv) * sin)
        return jnp.concatenate(out, axis=1)

    c_q, c_v = 0, 2 * sbw
    c_rq, c_rk, c_rv, c_rg = (3 * sbw + i * retw for i in range(4))
    c_a = 3 * sbw + 4 * retw

    def project_conv():
        ag = proj(c_a, 2 * cw) + pb_ref[row, :]
        pad_ref[halo:halo + tm, :] = (ag[:, :cw] * jax.nn.sigmoid(ag[:, cw:])).astype(pad_ref.dtype)

    def project_ret_gate():
        gate = proj(c_rg, retw)
        sg_s[...] = (gate * jax.nn.sigmoid(gate)).astype(BF16)

    def project_ret_q():
        rq_s[...] = rotary(proj(c_rq, retw)).astype(BF16)

    def project_ret_k():
        rk_s[...] = rotary(proj(c_rk, retw)).astype(BF16)

    def project_ret_v():
        rv_s[...] = proj(c_rv, retw).astype(BF16)

    def project_sb_qk():
        qk = proj(c_q, 2 * sbw)
        q_s[...] = (head_norm(qk[:, :sbw], gq_ref[row, :]) * (LOG2E * HEAD_DIM ** -0.5)).astype(BF16)
        k_all[pl.ds(sb_pad + t0, tm), :] = head_norm(qk[:, sbw:], gk_ref[row, :]).astype(BF16)

    def project_sb_v():
        v = proj(c_v, sbw).astype(BF16)
        for hh in range(n_sb_heads):
            vm_all[hh, pl.ds(sb_pad + t0, tm), :] = jnp.where(_head_lanes(v.shape, hh), v, jnp.zeros_like(v))

    n_shift = CONV_SHIFTS
    first = halo - (CONV_KERNEL - 1)

    def conv_block(r0):
        window = pad_ref[r0:r0 + CONV_WINDOW, :]
        acc = jnp.zeros((CONV_BLOCK, cw), F32) + db_ref[row, :]
        for shift in range(n_shift):
            if shift == 0:
                moved = window[:CONV_BLOCK + halo].astype(F32)
            else:
                moved = jnp.dot(shift_ref[shift - 1], window, preferred_element_type=F32)
            for tap in range(CONV_KERNEL):
                if (first + tap) % n_shift == shift:
                    a = (first + tap) // n_shift * n_shift
                    acc = acc + moved[a:a + CONV_BLOCK, :] * dw_ref[tap:tap + 1, :]
        mu = jnp.mean(acc, axis=-1, keepdims=True)
        xc = acc - mu
        var = jnp.mean(xc * xc, axis=-1, keepdims=True)
        yn = xc * lax.rsqrt(var + EPS) * lg_ref[row, :] + lb_ref[row, :]
        yconv_ref[0, r0:r0 + CONV_BLOCK, :] = (yn * jax.nn.sigmoid(yn)).astype(yconv_ref.dtype)

    ch = qdec_ref.shape[0]
    group = bd_ref.shape[0]
    heads_per_group = group // HEAD_DIM

    def per_head_rows(t):
        return jnp.concatenate(
            [jnp.where(_head_lanes(t.shape, hh), t, jnp.zeros_like(t)) for hh in range(heads_per_group)], axis=0)

    def retention_scores(c0, gi):
        cols = slice(gi * group, (gi + 1) * group)
        sc = lax.dot_general(rq_s[c0:c0 + ch, cols], per_head_rows(rk_s[c0:c0 + ch, cols]),
                             (((1,), (1,)), ((), ())), preferred_element_type=F32)
        return (sc * dmat_ref[gi]).astype(BF16)

    def retention_chunk(c0, gi, scores):
        cols = slice(gi * group, (gi + 1) * group)
        q = rq_s[c0:c0 + ch, cols]
        k = rk_s[c0:c0 + ch, cols]
        v = rv_s[c0:c0 + ch, cols]
        state = state_ref[gi]
        qd = (q.astype(F32) * qdec_ref[:, cols]).astype(BF16)
        out = (jnp.dot(qd, state.astype(BF16), preferred_element_type=F32)
               + jnp.dot(scores, per_head_rows(v), preferred_element_type=F32))
        kd = (k.astype(F32) * kdec_ref[:, cols]).astype(BF16)
        kv = lax.dot_general(kd, v, (((0,), (0,)), ((), ())), preferred_element_type=F32)
        state_ref[gi] = state * cdec_ref[:, cols] + kv * bd_ref[...]
        return out

    def retention_norm(c0, gi, out):
        cols = slice(gi * group, (gi + 1) * group)
        ss = jnp.dot((out * out).astype(BF16), ones_ref[...], preferred_element_type=F32)
        yn = out * lax.rsqrt(ss * (1.0 / HEAD_DIM) + EPS) * rg_ref[row, cols]
        yret_ref[0, c0:c0 + ch, cols] = (yn * sg_s[c0:c0 + ch, cols].astype(F32)).astype(yret_ref.dtype)

    sb_scores, sb_weights, sb_values, top_carry = _sb_stages(q_s, k_all, vm_all, sfx_ref, n_sb_heads)
    n_sub = tm // blk
    n_sweep = SB_LOOKBACK + 1
    sb_stage1, sb_stage2, sb_tops = {}, {}, []

    def sb_krow(s):
        return pl.multiple_of(sb_pad + t0 + (s - SB_LOOKBACK) * blk, blk)

    def sb_step(step):
        if step < n_sub:
            valid = [has_earlier_tile if step - SB_LOOKBACK + j < 0 else None for j in range(n_sweep)]
            sb_stage1[step] = sb_scores(step * blk, sb_krow(step), n_sweep, True, valid)
        s = step - 1
        if 0 <= s < n_sub:
            sb_stage2[s] = sb_weights(*sb_stage1.pop(s), n_sweep, [None] * n_sb_heads)
        s = step - 2
        if 0 <= s < n_sub:
            ws, carries = sb_stage2.pop(s)
            pv = sb_values(ws, sb_krow(s), n_sweep)
            acc_ref[s] = pv
            ysb_ref[0, s * blk:(s + 1) * blk, :] = pv.astype(ysb_ref.dtype)
            for hh in range(n_sb_heads):
                c_ref[s, hh] = carries[hh]
            sb_tops.append(top_carry(carries))

    ret_chunks = [(c0, gi) for c0 in range(0, tm, ch) for gi in range(retw // group)]
    project_conv()
    project_sb_qk()
    project_sb_v()
    conv_starts = list(range(0, tm, CONV_BLOCK))
    projections = [project_ret_q, project_ret_k, project_ret_v, project_ret_gate]
    sb_steps = list(range(n_sub + 2))
    for i in range(max(len(conv_starts), len(projections))):
        if i < len(conv_starts):
            conv_block(conv_starts[i])
        if i < len(projections):
            projections[i]()
    pad_ref[0:halo, :] = pad_ref[tm:tm + halo, :]
    scores, outs, alive = {}, {}, None
    for i in range(len(ret_chunks) + 2):
        if i < len(ret_chunks):
            scores[i] = retention_scores(*ret_chunks[i])
        if sb_steps:
            sb_step(sb_steps.pop(0))
        elif alive is None:
            alive = jnp.max(top_carry(sb_tops)) > SB_DEAD_LOG2
        if 0 <= i - 1 < len(ret_chunks):
            outs[i - 1] = retention_chunk(*ret_chunks[i - 1], scores.pop(i - 1))
        if 0 <= i - 2 < len(ret_chunks):
            retention_norm(*ret_chunks[i - 2], outs.pop(i - 2))
    for step in sb_steps:
        sb_step(step)
    if alive is None:
        alive = jnp.max(top_carry(sb_tops)) > SB_DEAD_LOG2

    @pl.when(alive)
    def _():
        def sub_block(s, _):
            off = pl.multiple_of(s * blk, blk)
            q0 = t0 + off

            def cond(state):
                i, live = state
                return jnp.logical_and(i * blk <= q0, live)

            def body(state):
                i, _ = state
                krow = pl.multiple_of(sb_pad + q0 - i * blk, blk)
                ws, carries = sb_weights(*sb_scores(off, krow, 1, False), 1,
                                         [c_ref[s, hh] for hh in range(n_sb_heads)])
                acc_ref[s] += sb_values(ws, krow, 1)
                for hh in range(n_sb_heads):
                    c_ref[s, hh] = carries[hh]
                return i + 1, jnp.max(top_carry(carries)) > SB_DEAD_LOG2

            live = jnp.max(top_carry([c_ref[s, hh] for hh in range(n_sb_heads)])) > SB_DEAD_LOG2
            lax.while_loop(cond, body, (jnp.int32(n_sweep), live))
            ysb_ref[0, pl.ds(off, blk), :] = acc_ref[s].astype(ysb_ref.dtype)
            return 0

        lax.fori_loop(0, n_sub, sub_block, 0)


def _slab_specs(w, layer, n_steps, steps_per_row):
    rows, cols = w.shape[1:]
    assert rows % n_steps == 0 and (rows // n_steps) % 16 == 0
    slab = rows // n_steps
    return (pl.BlockSpec((None, slab, cols), lambda i, j: (layer, i * steps_per_row + j, 0)),
            pl.BlockSpec((slab, cols), lambda i, j: (i * steps_per_row + j, 0)))


def _mixers(layer, x, g, w, ones_bd, cos_t, sin_t, gq, gk, pb, shifts, dw_w, dw_b, ln_g, ln_b,
            qdec, kdec, dmat, cdec, bd, ret_g, sfx, next_weights, widths):
    B, T, D = x.shape
    sbw, retw, cw = widths
    tm = min(TOKEN_TILE, T)
    blk = SB_BLOCK
    assert tm % qdec.shape[0] == 0 and tm % CONV_BLOCK == 0 and tm % blk == 0
    n_sb_heads = sbw // HEAD_DIM
    key_rows = SB_LOOKBACK * blk + T
    grid = (B, T // tm)
    tok = lambda width: pl.BlockSpec((1, tm, width), lambda b, t: (b, t, 0))
    tab = pl.BlockSpec((tm, cos_t.shape[1]), lambda b, t: (t, 0))
    per_layer = lambda a: _layer_spec(a.shape, layer)
    shared = lambda a: _const_spec(a.shape)
    slabs = [_slab_specs(a, layer, grid[0] * grid[1], grid[1]) for a in next_weights]
    out_widths = (sbw, retw, cw)
    vmem = (2 * tm * D * 4 + w.size * 2 + dmat.size * 4 + 4 * tm * retw * 2 + (1 + n_sb_heads) * key_rows * sbw * 2
            + 2 * sum(out_widths) * tm * 2 + MIXERS_TEMP_BYTES)
    return pl.pallas_call(
        functools.partial(_mixers_kernel, layer=layer, widths=widths),
        grid=grid,
        in_specs=[tok(D), shared(g), shared(w), shared(ones_bd), tab, tab,
                  shared(gq), shared(gk), shared(pb), shared(shifts), per_layer(dw_w), shared(dw_b),
                  shared(ln_g), shared(ln_b), shared(qdec), shared(kdec), shared(dmat), shared(cdec),
                  shared(bd), shared(ret_g), shared(sfx)] + [s[0] for s in slabs],
        out_specs=[tok(wd) for wd in out_widths] + [s[1] for s in slabs],
        out_shape=[jax.ShapeDtypeStruct((B, T, wd), BF16) for wd in out_widths]
                  + [jax.ShapeDtypeStruct(a.shape[1:], BF16) for a in next_weights],
        scratch_shapes=[pltpu.VMEM((tm + CONV_WINDOW - CONV_BLOCK, cw), BF16),
                        pltpu.VMEM((retw // bd.shape[0], bd.shape[0], bd.shape[0]), F32)]
                       + [pltpu.VMEM((tm, retw), BF16)] * 4
                       + [pltpu.VMEM((tm, sbw), BF16), pltpu.VMEM((key_rows, sbw), BF16),
                          pltpu.VMEM((n_sb_heads, key_rows, sbw), BF16),
                          pltpu.VMEM((tm // blk, blk, sbw), F32),
                          pltpu.VMEM((tm // blk, n_sb_heads, blk, blk), F32)],
        compiler_params=pltpu.CompilerParams(
            dimension_semantics=("arbitrary", "arbitrary"), vmem_limit_bytes=_vmem_limit(vmem)),
        name="mixers",
    )(x, g, w, ones_bd, cos_t, sin_t, gq, gk, pb, shifts, dw_w, dw_b, ln_g, ln_b,
      qdec, kdec, dmat, cdec, bd, ret_g, sfx, *next_weights)


def _out_mlp_kernel(x_ref, ysb_ref, yret_ref, yconv_ref, wo_ref, g_ref, w1_ref, w2_ref, *rest, layer):
    if len(rest) == 1:
        (o_ref,) = rest
    else:
        w_next_f_ref, col_scale_ref, o_ref, w_next_b_ref = rest
        w_next_b_ref[...] = (w_next_f_ref[...] * col_scale_ref[...]).astype(w_next_b_ref.dtype)
    mixed = jnp.concatenate([ysb_ref[0], yret_ref[0], yconv_ref[0]], axis=1)
    x1 = x_ref[0] + jnp.dot(mixed, wo_ref[...], preferred_element_type=F32)
    ms = jnp.mean(x1 * x1, axis=-1, keepdims=True)
    h = (x1 * lax.rsqrt(ms + EPS) * g_ref[layer:layer + 1, :]).astype(BF16)
    acc = x1
    for c0 in range(0, w1_ref.shape[1], FF_CHUNK):
        f = jnp.maximum(jnp.dot(h, w1_ref[:, c0:c0 + FF_CHUNK], preferred_element_type=F32), 0.0)
        acc = acc + jnp.dot((f * f).astype(BF16), w2_ref[c0:c0 + FF_CHUNK, :], preferred_element_type=F32)
    o_ref[0] = acc


def _out_mlp(layer, x, ysb, yret, yconv, wo, g, w1, w2, w_in, col_scale):
    B, T, D = x.shape
    tm = min(MLP_TILE, T)
    grid = (B, T // tm)
    tok = lambda width: pl.BlockSpec((1, tm, width), lambda b, t: (b, t, 0))
    dff = w1.shape[1]
    vmem = (D * D + 2 * D * dff) * 2 + 4 * tm * D * 4 + 4 * tm * D * 2 + tm * FF_CHUNK * 12 + MLP_TEMP_BYTES
    in_specs = [tok(D), tok(ysb.shape[2]), tok(yret.shape[2]), tok(yconv.shape[2]),
                _const_spec(wo.shape), _const_spec(g.shape), _const_spec(w1.shape), _const_spec(w2.shape)]
    out_specs, out_shape, extra = [tok(D)], [jax.ShapeDtypeStruct((B, T, D), F32)], ()
    if layer + 1 < w_in.shape[0]:
        f_spec, b_spec = _slab_specs(w_in, layer + 1, grid[0] * grid[1], grid[1])
        in_specs += [f_spec, _const_spec(col_scale.shape)]
        out_specs.append(b_spec)
        out_shape.append(jax.ShapeDtypeStruct(w_in.shape[1:], BF16))
        extra = (w_in, col_scale)
    return pl.pallas_call(
        functools.partial(_out_mlp_kernel, layer=layer),
        grid=grid,
        in_specs=in_specs,
        out_specs=out_specs,
        out_shape=out_shape,
        compiler_params=pltpu.CompilerParams(
            dimension_semantics=("arbitrary", "arbitrary"), vmem_limit_bytes=_vmem_limit(vmem)),
        name="out_mlp",
    )(x, ysb, yret, yconv, wo, g, w1, w2, *extra)


def _block_diag_ones(n, block):
    i = np.arange(n)
    return (i[:, None] // block == i[None, :] // block).astype(np.float32)


def _shift_matrices():
    r = np.arange(CONV_BLOCK + CONV_HALO)[None, :, None]
    j = np.arange(CONV_WINDOW)[None, None, :]
    b = np.arange(1, CONV_SHIFTS)[:, None, None]
    return (j == r + b).astype(np.float32)


def _suffix_sum_matrix(blk):
    j = np.arange(2 * blk)[:, None] % blk
    s = np.arange(2 * blk)[None, :]
    return -((s >= blk) | (j > s)).astype(np.float32)


def _rotary_tables(T):
    inv = 1.0 / (ROPE_BASE ** np.linspace(0.0, 1.0, HEAD_DIM // 2))
    ang = np.arange(T)[:, None] * inv[None, :]
    cos = np.repeat(np.cos(ang), 2, axis=1)
    sin = np.repeat(np.sin(ang), 2, axis=1)
    sign = np.where(np.arange(HEAD_DIM) % 2 == 0, -1.0, 1.0)
    reps = (1, V7X_LANES // HEAD_DIM)
    return jnp.asarray(np.tile(cos, reps), F32), jnp.asarray(np.tile(sin * sign[None, :], reps), F32)


def _retention_tables(n_heads, ch):
    log_g = np.log(1.0 - np.exp2(-5.0 - np.arange(n_heads)))
    j = np.arange(ch, dtype=np.float64)
    per_lane = lambda t: np.repeat(t, HEAD_DIM, axis=-1)
    qdec = per_lane(np.exp(log_g[None, :] * (j + 1.0)[:, None]))
    kdec = per_lane(np.exp(log_g[None, :] * (ch - 1.0 - j)[:, None]))
    cdec = per_lane(np.exp(log_g * ch)[None, :])
    rel = j[:, None] - j[None, :]
    dmat = np.where(rel >= 0, np.exp(log_g[:, None, None] * np.maximum(rel, 0.0)), 0.0)
    per_group = V7X_MXU_DIM // HEAD_DIM
    dmat = dmat.reshape(n_heads // per_group, per_group, ch, ch).transpose(0, 2, 1, 3)
    dmat = dmat.reshape(n_heads // per_group, ch, per_group * ch)
    return tuple(jnp.asarray(t, F32) for t in (qdec, kdec, dmat, cdec))


def kernel(x, mix_norm_g, w_in, sb_q_norm_g, sb_k_norm_g, ret_norm_g, conv_pw_b, conv_dw_w, conv_dw_b,
           conv_ln_g, conv_ln_b, w_out, mlp_norm_g, w_ff1, w_ff2):
    B, T, D = x.shape
    depth = w_in.shape[0]
    sbw = (D // 256) * HEAD_DIM
    retw = (D // 128) * HEAD_DIM
    cw = D - sbw - retw
    assert all(T % tile == 0 or T < tile for tile in (TOKEN_TILE, MLP_TILE))
    assert T % RET_CHUNK == 0 and T % SB_BLOCK == 0
    widths = (sbw, retw, cw)
    n_ret_heads = retw // HEAD_DIM

    ones_bd = jnp.asarray(_block_diag_ones(V7X_MXU_DIM, HEAD_DIM), BF16)
    bd_mask = jnp.asarray(_block_diag_ones(V7X_MXU_DIM, HEAD_DIM), F32)
    sfx = jnp.asarray(_suffix_sum_matrix(SB_BLOCK), BF16)
    shifts = jnp.asarray(_shift_matrices(), BF16)
    cos_t, sin_t = _rotary_tables(T)
    qdec, kdec, dmat, cdec = _retention_tables(n_ret_heads, RET_CHUNK)
    rk0 = 3 * sbw + retw
    col_scale = np.ones((1, w_in.shape[2]), np.float32)
    col_scale[:, rk0:rk0 + retw] = HEAD_DIM ** -0.5
    col_scale = jnp.asarray(col_scale)

    f32 = lambda a: a.astype(F32)
    mix_g, mlp_g, ret_g, pw_b = f32(mix_norm_g), f32(mlp_norm_g), f32(ret_norm_g), f32(conv_pw_b)
    gq, gk = (f32(jnp.tile(g, (1, sbw // HEAD_DIM))) for g in (sb_q_norm_g, sb_k_norm_g))
    dw_w, dw_b, ln_g, ln_b = f32(conv_dw_w), f32(conv_dw_b), f32(conv_ln_g), f32(conv_ln_b)
    w_in_b = (w_in[0] * col_scale).astype(BF16)
    for l in range(depth):
        y_sb, y_ret, y_conv, w_out_b, w_ff1_b, w_ff2_b = _mixers(
            l, x, mix_g, w_in_b, ones_bd, cos_t, sin_t, gq, gk, pw_b, shifts, dw_w, dw_b, ln_g, ln_b,
            qdec, kdec, dmat, cdec, bd_mask, ret_g, sfx, (w_out, w_ff1, w_ff2), widths)
        x, *w_next = _out_mlp(l, x, y_sb, y_ret, y_conv, w_out_b, mlp_g, w_ff1_b, w_ff2_b, w_in, col_scale)
        if w_next:
            (w_in_b,) = w_next
    return x
```
